```python
import functools
import jax, jax.numpy as jnp
from jax import lax
import numpy as np

D_MODEL = 2048
BATCH = 1
SEQ = 8192
DEPTH = 1
DEC_BATCH = 16
DEC_SEQ = 16
PAST_LEN = 2048

CHUNK = 64
N_META = 16
GLA_HEADS = 4
GLA_DK = D_MODEL // 2
GLA_DV = D_MODEL
HEAD_K = GLA_DK // GLA_HEADS
HEAD_V = GLA_DV // GLA_HEADS
GATE_RANK = 16
GATE_TEMP = 16.0
CONV_WIDTH = 3
CONV_DIM = D_MODEL
D_FF = -(-(8 * D_MODEL) // (3 * 256)) * 256
ALPHA = (2.0 * DEPTH) ** 0.25
BETA = (8.0 * DEPTH) ** -0.25
LN_EPS = 1e-5
RMS_EPS = 1e-6
IN_SIZES = (GLA_DK, GLA_DK, GLA_DV, GLA_DV, GATE_RANK, CONV_DIM, CONV_DIM, CONV_DIM, D_MODEL, D_MODEL)
D_IN_TOTAL = 2 * GLA_DK + 2 * GLA_DV + GATE_RANK + 3 * CONV_DIM + 2 * D_MODEL

kernel_name = "hybrid_gla_shortconv_streaming_step"


def _split_points(sizes):
    pts, acc = [], 0
    for s in sizes[:-1]:
        acc += s
        pts.append(acc)
    return pts


def layer_norm(x, g, b):
    xf = x.astype(jnp.float32)
    mu = jnp.mean(xf, axis=-1, keepdims=True)
    var = jnp.mean(jnp.square(xf - mu), axis=-1, keepdims=True)
    y = (xf - mu) * lax.rsqrt(var + LN_EPS)
    return (y * g.astype(jnp.float32) + b.astype(jnp.float32)).astype(x.dtype)


def to_heads(t, n_heads):
    B, L, W = t.shape
    return t.reshape(B, L, n_heads, W // n_heads).transpose(0, 2, 1, 3)


def from_heads(t):
    B, H, L, d = t.shape
    return t.transpose(0, 2, 1, 3).reshape(B, L, H * d)


def gla_chunk_step(S, q, k, v, g):
    S = S.astype(jnp.float32)
    qf, kf, vf = q.astype(jnp.float32), k.astype(jnp.float32), v.astype(jnp.float32)
    b = jnp.cumsum(g.astype(jnp.float32), axis=2)
    C = q.shape[2]
    causal = jnp.tril(jnp.ones((C, C), dtype=bool))
    diff = b[:, :, :, None, :] - b[:, :, None, :, :]
    decay = jnp.exp(jnp.where(causal[None, None, :, :, None], diff, -jnp.inf))
    scores = jnp.einsum('bhid,bhjd,bhijd->bhij', qf, kf, decay)
    o = jnp.einsum('bhij,bhjv->bhiv', scores, vf) + jnp.einsum('bhid,bhdv->bhiv', qf * jnp.exp(b), S)
    b_last = b[:, :, -1:, :]
    k_dec = kf * jnp.exp(b_last - b)
    S_new = jnp.exp(b_last[:, :, 0, :])[..., None] * S + jnp.einsum('bhjd,bhjv->bhdv', k_dec, vf)
    return S_new, o


def gla_prompt(q, k, v, g):
    B, H, L, _ = q.shape
    pad = (-L) % CHUNK
    padf = lambda t: jnp.pad(t, ((0, 0), (0, 0), (pad, 0), (0, 0)))
    n_blk = (L + pad) // CHUNK
    blk = lambda t: padf(t).reshape(B, H, n_blk, CHUNK, t.shape[-1]).transpose(2, 0, 1, 3, 4)
    S0 = jnp.zeros((B, H, HEAD_K, HEAD_V), jnp.float32)
    S_fin, o = lax.scan(lambda S, xs: gla_chunk_step(S, *xs), S0, (blk(q), blk(k), blk(v), blk(g)))
    o = o.transpose(1, 2, 0, 3, 4).reshape(B, H, n_blk * CHUNK, HEAD_V)[:, :, pad:]
    return o, S_fin


def gla_from_state(S0, q, k, v, g):
    S_new, o = gla_chunk_step(S0, q, k, v, g)
    return o, S_new


def short_conv(xc, buf, conv_w):
    L = xc.shape[1]
    xp = jnp.concatenate([buf.astype(xc.dtype), xc], axis=1)
    y = conv_w[0] * xp[:, 0:L]
    for i in range(1, CONV_WIDTH):
        y = y + conv_w[i] * xp[:, i:i + L]
    return y, xp[:, L:]


def trunk_layer(xn, gla_fn, conv_buf, w_in, w_gate_up, b_gate, gla_norm, conv_w, w_out,
                ln1_g, ln1_b, w_ffn_in, w_ffn_out, ln2_g, ln2_b):
    u = xn @ w_in
    q, k, v, r, a, cb, cc, cx, ga, gb = jnp.split(u, _split_points(IN_SIZES), axis=-1)
    g = jax.nn.log_sigmoid((a @ w_gate_up + b_gate).astype(jnp.float32)) / GATE_TEMP
    o, S_new = gla_fn(to_heads(q * (HEAD_K ** -0.5), GLA_HEADS), to_heads(k, GLA_HEADS),
                      to_heads(v, GLA_HEADS), to_heads(g, GLA_HEADS))
    o = o * lax.rsqrt(jnp.mean(o * o, axis=-1, keepdims=True) + RMS_EPS)
    o = from_heads(o) * gla_norm.astype(jnp.float32)
    y_a = (o * jax.nn.silu(r.astype(jnp.float32))).astype(xn.dtype)
    conv_y, conv_new = short_conv(cc * cx, conv_buf, conv_w)
    y_b = cb * conv_y
    m = jax.nn.sigmoid(ga) * y_a + jax.nn.sigmoid(gb) * y_b
    mix = m @ w_out
    h = layer_norm(ALPHA * xn + mix, ln1_g, ln1_b)
    gate, up = jnp.split(h @ w_ffn_in, [D_FF], axis=-1)
    f = (jax.nn.silu(gate) * up) @ w_ffn_out
    out = layer_norm(ALPHA * h + f, ln2_g, ln2_b)
    return out, S_new, conv_new


def setup_inputs(seed: int = 0) -> dict:
    key = jax.random.key(seed)
    ks = jax.random.split(key, 24)
    nrm = lambda k, shape, s: jax.random.normal(k, shape, jnp.float32) * s
    D = D_MODEL
    return {
        "x_prompt": nrm(ks[0], (BATCH, SEQ, D), 1.0),
        "x_sample": nrm(ks[1], (DEC_BATCH, DEC_SEQ, D), 1.0),
        "state_gla": nrm(ks[2], (DEPTH, DEC_BATCH, GLA_HEADS, HEAD_K, HEAD_V), 1.0),
        "cache_conv": nrm(ks[3], (DEPTH, DEC_BATCH, CONV_WIDTH - 1, D), 1.0),
        "meta_tokens": nrm(ks[4], (N_META, D), 1.0),
        "ln_in_g": 1.0 + nrm(ks[5], (D,), 0.02),
        "ln_in_b": nrm(ks[6], (D,), 0.02),
        "w_in": nrm(ks[7], (DEPTH, D, D_IN_TOTAL), D ** -0.5),
        "w_gate_up": nrm(ks[8], (DEPTH, GATE_RANK, GLA_DK), GATE_RANK ** -0.5),
        "b_gate": nrm(ks[9], (DEPTH, GLA_DK), 0.1),
        "gla_norm": 1.0 + nrm(ks[10], (DEPTH, GLA_DV), 0.02),
        "conv_w": nrm(ks[11], (DEPTH, CONV_WIDTH, CONV_DIM), CONV_WIDTH ** -0.5),
        "w_out": nrm(ks[12], (DEPTH, D, D), BETA * D ** -0.5),
        "ln1_g": 1.0 + nrm(ks[13], (DEPTH, D), 0.02),
        "ln1_b": nrm(ks[14], (DEPTH, D), 0.02),
        "w_ffn_in": nrm(ks[15], (DEPTH, D, 2 * D_FF), D ** -0.5),
        "w_ffn_out": nrm(ks[16], (DEPTH, D_FF, D), BETA * D_FF ** -0.5),
        "ln2_g": 1.0 + nrm(ks[17], (DEPTH, D), 0.02),
        "ln2_b": nrm(ks[18], (DEPTH, D), 0.02),
    }


def reference(x_prompt, x_sample, state_gla, cache_conv, meta_tokens, ln_in_g, ln_in_b,
              w_in, w_gate_up, b_gate, gla_norm, conv_w, w_out, ln1_g, ln1_b,
              w_ffn_in, w_ffn_out, ln2_g, ln2_b):
    B = x_prompt.shape[0]
    meta = jnp.broadcast_to(meta_tokens[None].astype(x_prompt.dtype), (B, N_META, D_MODEL))
    h_p = layer_norm(jnp.concatenate([meta, x_prompt], axis=1), ln_in_g, ln_in_b)
    h_s = layer_norm(x_sample, ln_in_g, ln_in_b)
    S_p_all, c_p_all, S_s_all, c_s_all = [], [], [], []
    for l in range(DEPTH):
        lw = (w_in[l], w_gate_up[l], b_gate[l], gla_norm[l], conv_w[l], w_out[l],
              ln1_g[l], ln1_b[l], w_ffn_in[l], w_ffn_out[l], ln2_g[l], ln2_b[l])
        zero_buf = jnp.zeros((B, CONV_WIDTH - 1, CONV_DIM), h_p.dtype)
        h_p, S_p, c_p = trunk_layer(h_p, gla_prompt, zero_buf, *lw)
        h_s, S_s, c_s = trunk_layer(h_s, functools.partial(gla_from_state, state_gla[l]), cache_conv[l], *lw)
        S_p_all.append(S_p.astype(x_prompt.dtype))
        c_p_all.append(c_p)
        S_s_all.append(S_s.astype(state_gla.dtype))
        c_s_all.append(c_s)
    y_prompt = h_p[:, N_META:]
    y_sample = h_s
    state_gla_prompt = jnp.stack(S_p_all, axis=0)
    cache_conv_prompt = jnp.stack(c_p_all, axis=0)
    state_gla_sample = jnp.stack(S_s_all, axis=0)
    cache_conv_sample = jnp.stack(c_s_all, axis=0)
    return (y_prompt, y_sample, state_gla_prompt, cache_conv_prompt, state_gla_sample, cache_conv_sample)
```

```python
import functools

import jax
import jax.numpy as jnp
from jax import lax
from jax.experimental import pallas as pl
from jax.experimental.pallas import tpu as pltpu

D_MODEL = 2048
SEQ = 8192
DEC_BATCH = 16
DEC_SEQ = 16
CHUNK = 64
N_META = 16
GLA_HEADS = 4
HEAD_K = 256
HEAD_V = 512
GLA_DK = GLA_HEADS * HEAD_K
GATE_RANK = 16
GATE_TEMP = 16.0
D_FF = 5632
ALPHA = 2.0 ** 0.25
LN_EPS = 1e-5
RMS_EPS = 1e-6

ROW_TILE = 512
N_PROMPT_TILES = SEQ // ROW_TILE
N_TILES = N_PROMPT_TILES + 1
ROWS = N_TILES * ROW_TILE
META_ROW0 = CHUNK - N_META
SAMPLE_ROW0 = CHUNK
SAMPLE_ROWS = DEC_BATCH * DEC_SEQ
LANE = 128
D_MAIN = 16384
VMEM_LIMIT = 56 * 1024 * 1024

_NT = (((1,), (1,)), ((), ()))
_TN = (((0,), (0,)), ((), ()))


def _tile_first_tail(i):
    return (i + N_PROMPT_TILES) % N_TILES


def _layer_norm(x, g, b):
    mu = jnp.mean(x, axis=-1, keepdims=True)
    xc = x - mu
    var = jnp.mean(xc * xc, axis=-1, keepdims=True)
    return xc * lax.rsqrt(var + LN_EPS) * g + b


def _log_sigmoid(x):
    return jnp.minimum(x, 0.0) - jnp.log1p(jnp.exp(-jnp.abs(x)))


def _sigmoid(x):
    return 1.0 / (1.0 + jnp.exp(-x))


def _ln_gate_kernel(x_ref, lg_ref, lb_ref, wa_ref, wgu_ref, bg_ref, xn_ref, xb_ref, g_ref):
    xn = _layer_norm(x_ref[...], lg_ref[...], lb_ref[...])
    xn_ref[...] = xn
    xb = xn.astype(jnp.bfloat16)
    xb_ref[...] = xb
    a = jnp.dot(xb, wa_ref[...], preferred_element_type=jnp.float32)
    z = jnp.dot(a.astype(jnp.bfloat16), wgu_ref[...], preferred_element_type=jnp.float32)
    g_ref[...] = _log_sigmoid(z + bg_ref[...]) * (1.0 / GATE_TEMP)


def _ln_gate(x_all, ln_g, ln_b, w_a, w_gu, b_gate):
    row = lambda i: (i, 0)
    full = lambda i: (0, 0)
    return pl.pallas_call(
        _ln_gate_kernel,
        grid=(N_TILES,),
        in_specs=[
            pl.BlockSpec((ROW_TILE, D_MODEL), row),
            pl.BlockSpec((1, D_MODEL), full),
            pl.BlockSpec((1, D_MODEL), full),
            pl.BlockSpec((D_MODEL, LANE), full),
            pl.BlockSpec((LANE, GLA_DK), full),
            pl.BlockSpec((1, GLA_DK), full),
        ],
        out_specs=[
            pl.BlockSpec((ROW_TILE, D_MODEL), row),
            pl.BlockSpec((ROW_TILE, D_MODEL), row),
            pl.BlockSpec((ROW_TILE, GLA_DK), row),
        ],
        out_shape=[
            jax.ShapeDtypeStruct((ROWS, D_MODEL), jnp.float32),
            jax.ShapeDtypeStruct((ROWS, D_MODEL), jnp.bfloat16),
            jax.ShapeDtypeStruct((ROWS, GLA_DK), jnp.float32),
        ],
        compiler_params=pltpu.CompilerParams(
            dimension_semantics=("arbitrary",), vmem_limit_bytes=VMEM_LIMIT),
        name="ln_gate",
    )(x_all, ln_g, ln_b, w_a, w_gu, b_gate)


PROJ_TN = 1024


def _proj_kernel(x_ref, w_ref, o_ref):
    o_ref[...] = jnp.dot(x_ref[...], w_ref[...], preferred_element_type=jnp.float32)


def _proj(xb, w_main):
    return pl.pallas_call(
        _proj_kernel,
        grid=(D_MAIN // PROJ_TN, N_TILES),
        in_specs=[
            pl.BlockSpec((ROW_TILE, D_MODEL), lambda j, i: (i, 0)),
            pl.BlockSpec((D_MODEL, PROJ_TN), lambda j, i: (0, j)),
        ],
        out_specs=pl.BlockSpec((ROW_TILE, PROJ_TN), lambda j, i: (i, j)),
        out_shape=jax.ShapeDtypeStruct((ROWS, D_MAIN), jnp.float32),
        compiler_params=pltpu.CompilerParams(
            dimension_semantics=("arbitrary", "arbitrary"), vmem_limit_bytes=VMEM_LIMIT),
        name="proj_in",
    )(xb, w_main)


_HALF_SIZES = (32, 16, 8, 4, 2, 1)


def _row_of_block(b, block, r):
    n, w = b.shape
    b3 = b.reshape(n // block, block, w)
    return jnp.broadcast_to(b3[:, r:r + 1, :], b3.shape).reshape(n, w)


def _gla_prepare(q, k, g):
    n, w = q.shape
    row = lax.broadcasted_iota(jnp.int32, (n, w), 0)
    pos = row & (CHUNK - 1)
    b = g
    for sh in (1, 2, 4, 8, 16, 32):
        b = b + jnp.where(pos >= sh, pltpu.roll(b, sh, 0), 0.0)
    qs = q * (HEAD_K ** -0.5)
    b_last = _row_of_block(b, CHUNK, CHUNK - 1)
    q_inter = (qs * jnp.exp(b)).astype(jnp.bfloat16)
    k_dec = (k * jnp.exp(b_last - b)).astype(jnp.bfloat16)
    levels = []
    for s in _HALF_SIZES:
        if s >= 4:
            b_mid = _row_of_block(b, 2 * s, s - 1)
            upper = (row & (2 * s - 1)) >= s
            d = jnp.where(upper, b - b_mid, b_mid - b)
        elif s == 2:
            p4 = row & 3
            g_prev = pltpu.roll(g, 1, 0)
            g_next = pltpu.roll(g, n - 1, 0)
            d = jnp.where(p4 == 0, g_next, jnp.where(p4 == 1, 0.0, jnp.where(p4 == 2, g, g + g_prev)))
        else:
            d = jnp.where((row & 1) == 1, g, 0.0)
        e = jnp.exp(d)
        levels.append(((qs * e).astype(jnp.bfloat16), (k * e).astype(jnp.bfloat16)))
    levels.append((qs.astype(jnp.bfloat16), k.astype(jnp.bfloat16)))
    return q_inter, k_dec, b, levels


def _score_masks():
    ii = lax.broadcasted_iota(jnp.int32, (CHUNK, CHUNK), 0)
    jj = lax.broadcasted_iota(jnp.int32, (CHUNK, CHUNK), 1)
    masks = []
    for s in _HALF_SIZES:
        blk = 2 * s
        same = (ii & ~(blk - 1)) == (jj & ~(blk - 1))
        masks.append(same & ((ii & (blk - 1)) >= s) & ((jj & (blk - 1)) < s))
    masks.append(ii == jj)
    return masks


def _gla_chunk(c, q_inter, k_dec, b, levels, v_bf, masks, s_val):
    r0 = c * CHUNK
    sl = slice(r0, r0 + CHUNK)
    scores = jnp.zeros((CHUNK, CHUNK), jnp.float32)
    for (qe, ke), m in zip(levels, masks):
        p = lax.dot_general(qe[sl], ke[sl], _NT, preferred_element_type=jnp.float32)
        scores = jnp.where(m, p, scores)
    vc = v_bf[sl]
    o = jnp.dot(scores.astype(jnp.bfloat16), vc, preferred_element_type=jnp.float32)
    o = o + jnp.dot(q_inter[sl], s_val.astype(jnp.bfloat16), preferred_element_type=jnp.float32)
    eb = jnp.exp(b[r0 + CHUNK - 1:r0 + CHUNK, :])
    eb_t = jnp.transpose(jnp.broadcast_to(eb, (LANE, HEAD_K)))
    decay = jnp.concatenate([eb_t] * (HEAD_V // LANE), axis=1)
    ds = lax.dot_general(k_dec[sl], vc, _TN, preferred_element_type=jnp.float32)
    return o, decay * s_val + ds


def _gla_finish(o, r, gn):
    o = o * lax.rsqrt(jnp.mean(o * o, axis=-1, keepdims=True) + RMS_EPS)
    return o * gn * (r * _sigmoid(r))


def _gla_kernel(q_ref, k_ref, v_ref, r_ref, g_ref, gn_ref, sin_ref,
                y_ref, sp_ref, ss_ref, s_scr):
    i = pl.program_id(1)
    masks = _score_masks()
    gn = gn_ref[...]

    @pl.when(i == 0)
    def _tail():
        def padded(x):
            parts = [x[0:CHUNK]]
            zeros = jnp.zeros((META_ROW0, x.shape[1]), x.dtype)
            for s in range(DEC_BATCH):
                parts += [zeros, x[SAMPLE_ROW0 + s * DEC_SEQ:SAMPLE_ROW0 + (s + 1) * DEC_SEQ]]
            return jnp.concatenate(parts, axis=0)

        n_chunks = 1 + DEC_BATCH
        n = n_chunks * CHUNK
        q = padded(q_ref[...])
        k = padded(k_ref[...])
        g = padded(g_ref[...])
        v = padded(v_ref[...])
        live = (lax.broadcasted_iota(jnp.int32, (n, HEAD_K), 0) & (CHUNK - 1)) >= META_ROW0
        k = jnp.where(live, k, 0.0)
        g = jnp.where(live, g, 0.0)
        q_inter, k_dec, b, levels = _gla_prepare(q, k, g)
        v_bf = v.astype(jnp.bfloat16)
        outs = [jnp.zeros((META_ROW0, HEAD_V), jnp.float32)]
        o, s_meta = _gla_chunk(0, q_inter, k_dec, b, levels, v_bf, masks,
                               jnp.zeros((HEAD_K, HEAD_V), jnp.float32))
        s_scr[...] = s_meta
        outs.append(o[META_ROW0:])
        for s in range(DEC_BATCH):
            o, s_new = _gla_chunk(1 + s, q_inter, k_dec, b, levels, v_bf, masks, sin_ref[s])
            ss_ref[s] = s_new
            outs.append(o[META_ROW0:])
        outs.append(jnp.zeros((ROW_TILE - SAMPLE_ROW0 - SAMPLE_ROWS, HEAD_V), jnp.float32))
        y_ref[...] = _gla_finish(jnp.concatenate(outs, axis=0), r_ref[...], gn)

    @pl.when(i > 0)
    def _prompt():
        q_inter, k_dec, b, levels = _gla_prepare(q_ref[...], k_ref[...], g_ref[...])
        v_bf = v_ref[...].astype(jnp.bfloat16)
        outs = []
        for c in range(ROW_TILE // CHUNK):
            o, s_new = _gla_chunk(c, q_inter, k_dec, b, levels, v_bf, masks, s_scr[...])
            s_scr[...] = s_new
            outs.append(o)
        y_ref[...] = _gla_finish(jnp.concatenate(outs, axis=0), r_ref[...], gn)

    @pl.when(i == N_TILES - 1)
    def _final_state():
        sp_ref[...] = s_scr[...]


def _gla(u, g, gla_norm, state_gla):
    rb = _tile_first_tail
    kq = GLA_DK // HEAD_K
    return pl.pallas_call(
        _gla_kernel,
        grid=(GLA_HEADS, N_TILES),
        in_specs=[
            pl.BlockSpec((ROW_TILE, HEAD_K), lambda h, i: (rb(i), h)),
            pl.BlockSpec((ROW_TILE, HEAD_K), lambda h, i: (rb(i), kq + h)),
            pl.BlockSpec((ROW_TILE, HEAD_V), lambda h, i: (rb(i), kq + h)),
            pl.BlockSpec((ROW_TILE, HEAD_V), lambda h, i: (rb(i), 2 * kq + h)),
            pl.BlockSpec((ROW_TILE, HEAD_K), lambda h, i: (rb(i), h)),
            pl.BlockSpec((1, HEAD_V), lambda h, i: (0, h)),
            pl.BlockSpec((DEC_BATCH, None, HEAD_K, HEAD_V), lambda h, i: (0, h, 0, 0)),
        ],
        out_specs=[
            pl.BlockSpec((ROW_TILE, HEAD_V), lambda h, i: (rb(i), h)),
            pl.BlockSpec((None, HEAD_K, HEAD_V), lambda h, i: (h, 0, 0)),
            pl.BlockSpec((DEC_BATCH, None, HEAD_K, HEAD_V), lambda h, i: (0, h, 0, 0)),
        ],
        out_shape=[
            jax.ShapeDtypeStruct((ROWS, D_MODEL), jnp.float32),
            jax.ShapeDtypeStruct((GLA_HEADS, HEAD_K, HEAD_V), jnp.float32),
            jax.ShapeDtypeStruct((DEC_BATCH, GLA_HEADS, HEAD_K, HEAD_V), jnp.float32),
        ],
        scratch_shapes=[pltpu.VMEM((HEAD_K, HEAD_V), jnp.float32)],
        compiler_params=pltpu.CompilerParams(
            dimension_semantics=("arbitrary", "arbitrary"), vmem_limit_bytes=VMEM_LIMIT),
        name="gla",
    )(u, u, u, u, g, gla_norm, state_gla)


MIX_TN = 512


def _mix_kernel(ya_ref, cb_ref, cc_ref, cx_ref, ga_ref, gb_ref, cw_ref, cache_ref,
                m_ref, ctail_ref, clast_ref, carry):
    i = pl.program_id(1)
    c = cc_ref[...] * cx_ref[...]
    row = lax.broadcasted_iota(jnp.int32, (ROW_TILE, MIX_TN), 0)
    prev1 = pltpu.roll(c, 1, 0)
    prev2 = pltpu.roll(c, 2, 0)
    w0 = cw_ref[0:1, :]
    w1 = cw_ref[1:2, :]
    w2 = cw_ref[2:3, :]

    def finish(p1, p2):
        conv = w0 * p2 + w1 * p1 + w2 * c
        m = _sigmoid(ga_ref[...]) * ya_ref[...] + _sigmoid(gb_ref[...]) * (cb_ref[...] * conv)
        m_ref[...] = m.astype(jnp.bfloat16)

    @pl.when(i == 0)
    def _tail():
        def stream_rows(j):
            parts = [jnp.zeros((SAMPLE_ROW0, MIX_TN), jnp.float32)]
            for s in range(DEC_BATCH):
                parts.append(jnp.broadcast_to(cache_ref[2 * s + j:2 * s + j + 1, :], (DEC_SEQ, MIX_TN)))
            parts.append(jnp.zeros((ROW_TILE - SAMPLE_ROW0 - SAMPLE_ROWS, MIX_TN), jnp.float32))
            return jnp.concatenate(parts, axis=0)

        old = stream_rows(0)
        new = stream_rows(1)
        in_sample = (row >= SAMPLE_ROW0) & (row < SAMPLE_ROW0 + SAMPLE_ROWS)
        p16 = (row - SAMPLE_ROW0) & (DEC_SEQ - 1)
        first = in_sample & (p16 == 0)
        second = in_sample & (p16 == 1)
        p1 = jnp.where(row == META_ROW0, 0.0, jnp.where(first, new, prev1))
        p2 = jnp.where((row == META_ROW0) | (row == META_ROW0 + 1), 0.0,
                       jnp.where(first, old, jnp.where(second, new, prev2)))
        finish(p1, p2)
        ctail_ref[...] = c
        carry[...] = c[CHUNK - 8:CHUNK]

    @pl.when(i > 0)
    def _prompt():
        hist = carry[...]
        p1 = jnp.where(row == 0, jnp.broadcast_to(hist[7:8], c.shape), prev1)
        p2 = jnp.where(row == 0, jnp.broadcast_to(hist[6:7], c.shape),
                       jnp.where(row == 1, jnp.broadcast_to(hist[7:8], c.shape), prev2))
        finish(p1, p2)
        carry[...] = c[ROW_TILE - 8:]

    @pl.when(i == N_TILES - 1)
    def _last():
        clast_ref[...] = c[ROW_TILE - 8:]


def _mix(ya, u, conv_w, cache_rows):
    rb = _tile_first_tail
    nc = D_MODEL // MIX_TN
    cb0 = (GLA_DK * 2 + D_MODEL * 2) // D_MODEL
    col = lambda grp: (lambda cj, i: (rb(i), grp * nc + cj))
    top = lambda cj, i: (0, cj)
    return pl.pallas_call(
        _mix_kernel,
        grid=(nc, N_TILES),
        in_specs=[
            pl.BlockSpec((ROW_TILE, MIX_TN), col(0)),
            pl.BlockSpec((ROW_TILE, MIX_TN), col(cb0)),
            pl.BlockSpec((ROW_TILE, MIX_TN), col(cb0 + 1)),
            pl.BlockSpec((ROW_TILE, MIX_TN), col(cb0 + 2)),
            pl.BlockSpec((ROW_TILE, MIX_TN), col(cb0 + 3)),
            pl.BlockSpec((ROW_TILE, MIX_TN), col(cb0 + 4)),
            pl.BlockSpec((3, MIX_TN), top),
            pl.BlockSpec((2 * DEC_BATCH, MIX_TN), top),
        ],
        out_specs=[
            pl.BlockSpec((ROW_TILE, MIX_TN), col(0)),
            pl.BlockSpec((ROW_TILE, MIX_TN), top),
            pl.BlockSpec((8, MIX_TN), top),
        ],
        out_shape=[
            jax.ShapeDtypeStruct((ROWS, D_MODEL), jnp.bfloat16),
            jax.ShapeDtypeStruct((ROW_TILE, D_MODEL), jnp.float32),
            jax.ShapeDtypeStruct((8, D_MODEL), jnp.float32),
        ],
        scratch_shapes=[pltpu.VMEM((8, MIX_TN), jnp.float32)],
        compiler_params=pltpu.CompilerParams(
            dimension_semantics=("arbitrary", "arbitrary"), vmem_limit_bytes=VMEM_LIMIT),
        name="conv_mix",
    )(ya, u, u, u, u, u, conv_w, cache_rows)


def _out_proj_kernel(m_ref, xn_ref, w_ref, lg_ref, lb_ref, h_ref, hb_ref):
    mix = jnp.dot(m_ref[...], w_ref[...], preferred_element_type=jnp.float32)
    h = _layer_norm(ALPHA * xn_ref[...] + mix, lg_ref[...], lb_ref[...])
    h_ref[...] = h
    hb_ref[...] = h.astype(jnp.bfloat16)


def _out_proj(m, xn, w_out, ln_g, ln_b):
    row = lambda i: (i, 0)
    full = lambda i: (0, 0)
    return pl.pallas_call(
        _out_proj_kernel,
        grid=(N_TILES,),
        in_specs=[
            pl.BlockSpec((ROW_TILE, D_MODEL), row),
            pl.BlockSpec((ROW_TILE, D_MODEL), row),
            pl.BlockSpec((D_MODEL, D_MODEL), full),
            pl.BlockSpec((1, D_MODEL), full),
            pl.BlockSpec((1, D_MODEL), full),
        ],
        out_specs=[pl.BlockSpec((ROW_TILE, D_MODEL), row), pl.BlockSpec((ROW_TILE, D_MODEL), row)],
        out_shape=[
            jax.ShapeDtypeStruct((ROWS, D_MODEL), jnp.float32),
            jax.ShapeDtypeStruct((ROWS, D_MODEL), jnp.bfloat16),
        ],
        compiler_params=pltpu.CompilerParams(
            dimension_semantics=("arbitrary",), vmem_limit_bytes=VMEM_LIMIT),
        name="out_proj_ln",
    )(m, xn, w_out, ln_g, ln_b)


FF_TILE = 512
N_FF_TILES = D_FF // FF_TILE


def _ffn_kernel(hb_ref, h_ref, wg_ref, wu_ref, wd_ref, lg_ref, lb_ref, o_ref, acc):
    j = pl.program_id(1)
    hb = hb_ref[...]
    gate = jnp.dot(hb, wg_ref[...], preferred_element_type=jnp.float32)
    up = jnp.dot(hb, wu_ref[...], preferred_element_type=jnp.float32)
    act = (gate * _sigmoid(gate) * up).astype(jnp.bfloat16)
    part = jnp.dot(act, wd_ref[...], preferred_element_type=jnp.float32)

    @pl.when(j == 0)
    def _init():
        acc[...] = part

    @pl.when(j > 0)
    def _accumulate():
        acc[...] += part

    @pl.when(j == N_FF_TILES - 1)
    def _finish():
        o_ref[...] = _layer_norm(ALPHA * h_ref[...] + acc[...], lg_ref[...], lb_ref[...])


def _ffn(hb, h, w_ffn_in, w_ffn_out, ln_g, ln_b):
    row = lambda i, j: (i, 0)
    full = lambda i, j: (0, 0)
    return pl.pallas_call(
        _ffn_kernel,
        grid=(N_TILES, N_FF_TILES),
        in_specs=[
            pl.BlockSpec((ROW_TILE, D_MODEL), row),
            pl.BlockSpec((ROW_TILE, D_MODEL), row),
            pl.BlockSpec((D_MODEL, FF_TILE), lambda i, j: (0, j)),
            pl.BlockSpec((D_MODEL, FF_TILE), lambda i, j: (0, N_FF_TILES + j)),
            pl.BlockSpec((FF_TILE, D_MODEL), lambda i, j: (j, 0)),
            pl.BlockSpec((1, D_MODEL), full),
            pl.BlockSpec((1, D_MODEL), full),
        ],
        out_specs=pl.BlockSpec((ROW_TILE, D_MODEL), row),
        out_shape=jax.ShapeDtypeStruct((ROWS, D_MODEL), jnp.float32),
        scratch_shapes=[pltpu.VMEM((ROW_TILE, D_MODEL), jnp.float32)],
        compiler_params=pltpu.CompilerParams(
            dimension_semantics=("arbitrary", "arbitrary"), vmem_limit_bytes=VMEM_LIMIT),
        name="ffn_ln",
    )(hb, h, w_ffn_in, w_ffn_in, w_ffn_out, ln_g, ln_b)


def kernel(x_prompt, x_sample, state_gla, cache_conv, meta_tokens, ln_in_g, ln_in_b, w_in, w_gate_up, b_gate, gla_norm, conv_w, w_out, ln1_g, ln1_b, w_ffn_in, w_ffn_out, ln2_g, ln2_b):
    f32, bf16 = jnp.float32, jnp.bfloat16
    tail = jnp.concatenate([
        jnp.zeros((META_ROW0, D_MODEL), f32),
        meta_tokens.astype(f32),
        x_sample.reshape(SAMPLE_ROWS, D_MODEL),
        jnp.zeros((ROW_TILE - SAMPLE_ROW0 - SAMPLE_ROWS, D_MODEL), f32),
    ], axis=0)
    x_all = jnp.concatenate([x_prompt[0], tail], axis=0)

    a0 = 2 * GLA_DK + 2 * D_MODEL
    w_in0 = w_in[0]
    w_a = jnp.pad(w_in0[:, a0:a0 + GATE_RANK], ((0, 0), (0, LANE - GATE_RANK))).astype(bf16)
    w_main = jnp.concatenate([w_in0[:, :a0], w_in0[:, a0 + GATE_RANK:]], axis=1).astype(bf16)
    w_gu = jnp.pad(w_gate_up[0], ((0, LANE - GATE_RANK), (0, 0))).astype(bf16)
    vec = lambda p: p.reshape(1, -1).astype(f32)

    xn, xb, g = _ln_gate(x_all, vec(ln_in_g), vec(ln_in_b), w_a, w_gu, vec(b_gate[0]))
    u = _proj(xb, w_main)
    ya, s_prompt, s_sample = _gla(u, g, vec(gla_norm[0]), state_gla[0])
    m, c_tail, c_last = _mix(ya, u, conv_w[0], cache_conv[0].reshape(2 * DEC_BATCH, D_MODEL))
    h, hb = _out_proj(m, xn, w_out[0].astype(bf16), vec(ln1_g[0]), vec(ln1_b[0]))
    out = _ffn(hb, h, w_ffn_in[0].astype(bf16), w_ffn_out[0].astype(bf16), vec(ln2_g[0]), vec(ln2_b[0]))

    y_prompt = out[:SEQ][None]
    y_sample = out[SEQ + SAMPLE_ROW0:SEQ + SAMPLE_ROW0 + SAMPLE_ROWS].reshape(DEC_BATCH, DEC_SEQ, D_MODEL)
    cache_prompt = c_last[6:8][None, None]
    cache_sample = c_tail[SAMPLE_ROW0:SAMPLE_ROW0 + SAMPLE_ROWS].reshape(
        DEC_BATCH, DEC_SEQ, D_MODEL)[:, DEC_SEQ - 2:][None]
    return (y_prompt, y_sample, s_prompt[None, None], cache_prompt, s_sample[None], cache_sample)
```

```python
import functools

import jax
import jax.numpy as jnp
from jax import lax
from jax.experimental import pallas as pl
from jax.experimental.pallas import tpu as pltpu

D_MODEL = 2048
SEQ = 8192
DEC_BATCH = 16
DEC_SEQ = 16
CHUNK = 64
N_META = 16
GLA_HEADS = 4
HEAD_K = 256
HEAD_V = 512
GLA_DK = GLA_HEADS * HEAD_K
GATE_RANK = 16
GATE_TEMP = 16.0
D_FF = 5632
ALPHA = 2.0 ** 0.25
LN_EPS = 1e-5
RMS_EPS = 1e-6

ROW_TILE = 512
N_PROMPT_TILES = SEQ // ROW_TILE
N_TILES = N_PROMPT_TILES + 1
ROWS = N_TILES * ROW_TILE
META_ROW0 = CHUNK - N_META
SAMPLE_ROW0 = CHUNK
SAMPLE_ROWS = DEC_BATCH * DEC_SEQ
LANE = 128
D_MAIN = 16384
GATE_COL0 = 2 * GLA_DK + 2 * D_MODEL
VMEM_LIMIT = 56 * 1024 * 1024

_NT = (((1,), (1,)), ((), ()))
_TN = (((0,), (0,)), ((), ()))


def _tile_first_tail(i):
    return (i + N_PROMPT_TILES) % N_TILES


def _layer_norm(x, g, b):
    mu = jnp.mean(x, axis=-1, keepdims=True)
    xc = x - mu
    var = jnp.mean(xc * xc, axis=-1, keepdims=True)
    return xc * lax.rsqrt(var + LN_EPS) * g + b


def _log_sigmoid(x):
    return jnp.minimum(x, 0.0) - jnp.log1p(jnp.exp(-jnp.abs(x)))


def _sigmoid(x):
    return 1.0 / (1.0 + jnp.exp(-x))


def _ln_gate_kernel(xp_ref, xt_ref, lg_ref, lb_ref, wa_ref, wgu_ref, bg_ref, xn_ref, xb_ref, g_ref):
    x = jnp.where(pl.program_id(0) < N_PROMPT_TILES, xp_ref[...], xt_ref[...])
    xn = _layer_norm(x, lg_ref[...], lb_ref[...])
    xn_ref[...] = xn
    xb = xn.astype(jnp.bfloat16)
    xb_ref[...] = xb
    a = jnp.dot(xb, wa_ref[...].astype(jnp.bfloat16), preferred_element_type=jnp.float32)
    z = jnp.dot(a.astype(jnp.bfloat16), wgu_ref[...], preferred_element_type=jnp.float32)
    g_ref[...] = _log_sigmoid(z + bg_ref[...]) * (1.0 / GATE_TEMP)


def _ln_gate(x_prompt, x_tail, ln_g, ln_b, w_in, w_gu, b_gate):
    row = lambda i: (i, 0)
    full = lambda i: (0, 0)
    return pl.pallas_call(
        _ln_gate_kernel,
        grid=(N_TILES,),
        in_specs=[
            pl.BlockSpec((None, ROW_TILE, D_MODEL), lambda i: (0, jnp.minimum(i, N_PROMPT_TILES - 1), 0)),
            pl.BlockSpec((ROW_TILE, D_MODEL), full),
            pl.BlockSpec((1, D_MODEL), full),
            pl.BlockSpec((1, D_MODEL), full),
            pl.BlockSpec((None, D_MODEL, LANE), lambda i: (0, 0, GATE_COL0 // LANE)),
            pl.BlockSpec((LANE, GLA_DK), full),
            pl.BlockSpec((1, GLA_DK), full),
        ],
        out_specs=[
            pl.BlockSpec((ROW_TILE, D_MODEL), row),
            pl.BlockSpec((ROW_TILE, D_MODEL), row),
            pl.BlockSpec((ROW_TILE, GLA_DK), row),
        ],
        out_shape=[
            jax.ShapeDtypeStruct((ROWS, D_MODEL), jnp.float32),
            jax.ShapeDtypeStruct((ROWS, D_MODEL), jnp.bfloat16),
            jax.ShapeDtypeStruct((ROWS, GLA_DK), jnp.float32),
        ],
        compiler_params=pltpu.CompilerParams(
            dimension_semantics=("arbitrary",), vmem_limit_bytes=VMEM_LIMIT),
        name="ln_gate",
    )(x_prompt, x_tail, ln_g, ln_b, w_in, w_gu, b_gate)


PROJ_TN = 1024
N_PROJ_TILES = D_MAIN // PROJ_TN
N_ALIGNED_PROJ_TILES = GATE_COL0 // PROJ_TN
W_SLAB = 256


def _proj_kernel(x_ref, w_ref, wx_ref, o_ref, wbf):
    j = pl.program_id(0)
    i = pl.program_id(1)

    @pl.when((i == 0) & (j < N_ALIGNED_PROJ_TILES))
    def _cast():
        for r in range(0, D_MODEL, W_SLAB):
            wbf[r:r + W_SLAB, :] = w_ref[r:r + W_SLAB, :].astype(jnp.bfloat16)

    @pl.when((i == 0) & (j >= N_ALIGNED_PROJ_TILES))
    def _cast_shifted():
        for r in range(0, D_MODEL, W_SLAB):
            wide = jnp.concatenate([w_ref[r:r + W_SLAB, :], wx_ref[r:r + W_SLAB, :]], axis=1)
            wbf[r:r + W_SLAB, :] = wide[:, GATE_RANK:GATE_RANK + PROJ_TN].astype(jnp.bfloat16)

    o_ref[...] = jnp.dot(x_ref[...], wbf[...], preferred_element_type=jnp.float32)


def _proj(xb, w_in):
    return pl.pallas_call(
        _proj_kernel,
        grid=(N_PROJ_TILES, N_TILES),
        in_specs=[
            pl.BlockSpec((ROW_TILE, D_MODEL), lambda j, i: (i, 0)),
            pl.BlockSpec((None, D_MODEL, PROJ_TN), lambda j, i: (0, 0, j)),
            pl.BlockSpec((None, D_MODEL, LANE), lambda j, i: (0, 0, (j + 1) * (PROJ_TN // LANE))),
        ],
        out_specs=pl.BlockSpec((ROW_TILE, PROJ_TN), lambda j, i: (i, j)),
        out_shape=jax.ShapeDtypeStruct((ROWS, D_MAIN), jnp.float32),
        scratch_shapes=[pltpu.VMEM((D_MODEL, PROJ_TN), jnp.bfloat16)],
        compiler_params=pltpu.CompilerParams(
            dimension_semantics=("arbitrary", "arbitrary"), vmem_limit_bytes=VMEM_LIMIT),
        name="proj_in",
    )(xb, w_in, w_in)


_HALF_SIZES = (32, 16, 8, 4, 2, 1)


def _row_of_block(b, block, r):
    n, w = b.shape
    b3 = b.reshape(n // block, block, w)
    return jnp.broadcast_to(b3[:, r:r + 1, :], b3.shape).reshape(n, w)


def _gla_prepare(q, k, g):
    n, w = q.shape
    row = lax.broadcasted_iota(jnp.int32, (n, w), 0)
    pos = row & (CHUNK - 1)
    b = g
    for sh in (1, 2, 4, 8, 16, 32):
        b = b + jnp.where(pos >= sh, pltpu.roll(b, sh, 0), 0.0)
    qs = q * (HEAD_K ** -0.5)
    b_last = _row_of_block(b, CHUNK, CHUNK - 1)
    q_inter = (qs * jnp.exp(b)).astype(jnp.bfloat16)
    k_dec = (k * jnp.exp(b_last - b)).astype(jnp.bfloat16)
    levels = []
    for s in _HALF_SIZES:
        if s >= 4:
            b_mid = _row_of_block(b, 2 * s, s - 1)
            upper = (row & (2 * s - 1)) >= s
            d = jnp.where(upper, b - b_mid, b_mid - b)
        elif s == 2:
            p4 = row & 3
            g_prev = pltpu.roll(g, 1, 0)
            g_next = pltpu.roll(g, n - 1, 0)
            d = jnp.where(p4 == 0, g_next, jnp.where(p4 == 1, 0.0, jnp.where(p4 == 2, g, g + g_prev)))
        else:
            d = jnp.where((row & 1) == 1, g, 0.0)
        e = jnp.exp(d)
        levels.append(((qs * e).astype(jnp.bfloat16), (k * e).astype(jnp.bfloat16)))
    levels.append((qs.astype(jnp.bfloat16), k.astype(jnp.bfloat16)))
    return q_inter, k_dec, b, levels


def _score_masks():
    ii = lax.broadcasted_iota(jnp.int32, (CHUNK, CHUNK), 0)
    jj = lax.broadcasted_iota(jnp.int32, (CHUNK, CHUNK), 1)
    masks = []
    for s in _HALF_SIZES:
        blk = 2 * s
        same = (ii & ~(blk - 1)) == (jj & ~(blk - 1))
        masks.append(same & ((ii & (blk - 1)) >= s) & ((jj & (blk - 1)) < s))
    masks.append(ii == jj)
    return masks


def _gla_chunk(c, q_inter, k_dec, b, levels, v_bf, masks, s_val):
    r0 = c * CHUNK
    sl = slice(r0, r0 + CHUNK)
    scores = jnp.zeros((CHUNK, CHUNK), jnp.float32)
    for (qe, ke), m in zip(levels, masks):
        p = lax.dot_general(qe[sl], ke[sl], _NT, preferred_element_type=jnp.float32)
        scores = jnp.where(m, p, scores)
    vc = v_bf[sl]
    o = jnp.dot(scores.astype(jnp.bfloat16), vc, preferred_element_type=jnp.float32)
    o = o + jnp.dot(q_inter[sl], s_val.astype(jnp.bfloat16), preferred_element_type=jnp.float32)
    eb = jnp.exp(b[r0 + CHUNK - 1:r0 + CHUNK, :])
    eb_t = jnp.transpose(jnp.broadcast_to(eb, (LANE, HEAD_K)))
    decay = jnp.concatenate([eb_t] * (HEAD_V // LANE), axis=1)
    ds = lax.dot_general(k_dec[sl], vc, _TN, preferred_element_type=jnp.float32)
    return o, decay * s_val + ds


def _gla_finish(o, r, gn):
    o = o * lax.rsqrt(jnp.mean(o * o, axis=-1, keepdims=True) + RMS_EPS)
    return o * gn * (r * _sigmoid(r))


def _gla_kernel(q_ref, k_ref, v_ref, r_ref, g_ref, gn_ref, sin_ref,
                y_ref, sp_ref, ss_ref, s_scr):
    i = pl.program_id(1)
    masks = _score_masks()
    gn = gn_ref[...]

    @pl.when(i == 0)
    def _tail():
        def padded(x):
            parts = [x[0:CHUNK]]
            zeros = jnp.zeros((META_ROW0, x.shape[1]), x.dtype)
            for s in range(DEC_BATCH):
                parts += [zeros, x[SAMPLE_ROW0 + s * DEC_SEQ:SAMPLE_ROW0 + (s + 1) * DEC_SEQ]]
            return jnp.concatenate(parts, axis=0)

        n_chunks = 1 + DEC_BATCH
        n = n_chunks * CHUNK
        q = padded(q_ref[...])
        k = padded(k_ref[...])
        g = padded(g_ref[...])
        v = padded(v_ref[...])
        live = (lax.broadcasted_iota(jnp.int32, (n, HEAD_K), 0) & (CHUNK - 1)) >= META_ROW0
        k = jnp.where(live, k, 0.0)
        g = jnp.where(live, g, 0.0)
        q_inter, k_dec, b, levels = _gla_prepare(q, k, g)
        v_bf = v.astype(jnp.bfloat16)
        outs = [jnp.zeros((META_ROW0, HEAD_V), jnp.float32)]
        o, s_meta = _gla_chunk(0, q_inter, k_dec, b, levels, v_bf, masks,
                               jnp.zeros((HEAD_K, HEAD_V), jnp.float32))
        s_scr[...] = s_meta
        outs.append(o[META_ROW0:])
        for s in range(DEC_BATCH):
            o, s_new = _gla_chunk(1 + s, q_inter, k_dec, b, levels, v_bf, masks, sin_ref[s])
            ss_ref[s] = s_new
            outs.append(o[META_ROW0:])
        outs.append(jnp.zeros((ROW_TILE - SAMPLE_ROW0 - SAMPLE_ROWS, HEAD_V), jnp.float32))
        y_ref[...] = _gla_finish(jnp.concatenate(outs, axis=0), r_ref[...], gn)

    @pl.when(i > 0)
    def _prompt():
        q_inter, k_dec, b, levels = _gla_prepare(q_ref[...], k_ref[...], g_ref[...])
        v_bf = v_ref[...].astype(jnp.bfloat16)
        outs = []
        for c in range(ROW_TILE // CHUNK):
            o, s_new = _gla_chunk(c, q_inter, k_dec, b, levels, v_bf, masks, s_scr[...])
            s_scr[...] = s_new
            outs.append(o)
        y_ref[...] = _gla_finish(jnp.concatenate(outs, axis=0), r_ref[...], gn)

    @pl.when(i == N_TILES - 1)
    def _final_state():
        sp_ref[...] = s_scr[...]


def _gla(u, g, gla_norm, state_gla):
    rb = _tile_first_tail
    kq = GLA_DK // HEAD_K
    return pl.pallas_call(
        _gla_kernel,
        grid=(GLA_HEADS, N_TILES),
        in_specs=[
            pl.BlockSpec((ROW_TILE, HEAD_K), lambda h, i: (rb(i), h)),
            pl.BlockSpec((ROW_TILE, HEAD_K), lambda h, i: (rb(i), kq + h)),
            pl.BlockSpec((ROW_TILE, HEAD_V), lambda h, i: (rb(i), kq + h)),
            pl.BlockSpec((ROW_TILE, HEAD_V), lambda h, i: (rb(i), 2 * kq + h)),
            pl.BlockSpec((ROW_TILE, HEAD_K), lambda h, i: (rb(i), h)),
            pl.BlockSpec((1, HEAD_V), lambda h, i: (0, h)),
            pl.BlockSpec((DEC_BATCH, None, HEAD_K, HEAD_V), lambda h, i: (0, h, 0, 0)),
        ],
        out_specs=[
            pl.BlockSpec((ROW_TILE, HEAD_V), lambda h, i: (rb(i), h)),
            pl.BlockSpec((None, HEAD_K, HEAD_V), lambda h, i: (h, 0, 0)),
            pl.BlockSpec((DEC_BATCH, None, HEAD_K, HEAD_V), lambda h, i: (0, h, 0, 0)),
        ],
        out_shape=[
            jax.ShapeDtypeStruct((ROWS, D_MODEL), jnp.float32),
            jax.ShapeDtypeStruct((GLA_HEADS, HEAD_K, HEAD_V), jnp.float32),
            jax.ShapeDtypeStruct((DEC_BATCH, GLA_HEADS, HEAD_K, HEAD_V), jnp.float32),
        ],
        scratch_shapes=[pltpu.VMEM((HEAD_K, HEAD_V), jnp.float32)],
        compiler_params=pltpu.CompilerParams(
            dimension_semantics=("arbitrary", "arbitrary"), vmem_limit_bytes=VMEM_LIMIT),
        name="gla",
    )(u, u, u, u, g, gla_norm, state_gla)


MIX_TN = 512


def _mix_kernel(ya_ref, cb_ref, cc_ref, cx_ref, ga_ref, gb_ref, cw_ref, cache_ref,
                m_ref, ctail_ref, clast_ref, carry):
    i = pl.program_id(1)
    c = cc_ref[...] * cx_ref[...]
    row = lax.broadcasted_iota(jnp.int32, (ROW_TILE, MIX_TN), 0)
    prev1 = pltpu.roll(c, 1, 0)
    prev2 = pltpu.roll(c, 2, 0)
    w0 = cw_ref[0:1, :]
    w1 = cw_ref[1:2, :]
    w2 = cw_ref[2:3, :]

    def finish(p1, p2):
        conv = w0 * p2 + w1 * p1 + w2 * c
        m = _sigmoid(ga_ref[...]) * ya_ref[...] + _sigmoid(gb_ref[...]) * (cb_ref[...] * conv)
        m_ref[...] = m.astype(jnp.bfloat16)

    @pl.when(i == 0)
    def _tail():
        def stream_rows(j):
            parts = [jnp.zeros((SAMPLE_ROW0, MIX_TN), jnp.float32)]
            for s in range(DEC_BATCH):
                parts.append(jnp.broadcast_to(cache_ref[2 * s + j:2 * s + j + 1, :], (DEC_SEQ, MIX_TN)))
            parts.append(jnp.zeros((ROW_TILE - SAMPLE_ROW0 - SAMPLE_ROWS, MIX_TN), jnp.float32))
            return jnp.concatenate(parts, axis=0)

        old = stream_rows(0)
        new = stream_rows(1)
        in_sample = (row >= SAMPLE_ROW0) & (row < SAMPLE_ROW0 + SAMPLE_ROWS)
        p16 = (row - SAMPLE_ROW0) & (DEC_SEQ - 1)
        first = in_sample & (p16 == 0)
        second = in_sample & (p16 == 1)
        p1 = jnp.where(row == META_ROW0, 0.0, jnp.where(first, new, prev1))
        p2 = jnp.where((row == META_ROW0) | (row == META_ROW0 + 1), 0.0,
                       jnp.where(first, old, jnp.where(second, new, prev2)))
        finish(p1, p2)
        ctail_ref[...] = c
        carry[...] = c[CHUNK - 8:CHUNK]

    @pl.when(i > 0)
    def _prompt():
        hist = carry[...]
        p1 = jnp.where(row == 0, jnp.broadcast_to(hist[7:8], c.shape), prev1)
        p2 = jnp.where(row == 0, jnp.broadcast_to(hist[6:7], c.shape),
                       jnp.where(row == 1, jnp.broadcast_to(hist[7:8], c.shape), prev2))
        finish(p1, p2)
        carry[...] = c[ROW_TILE - 8:]

    @pl.when(i == N_TILES - 1)
    def _last():
        clast_ref[...] = c[ROW_TILE - 8:]


def _mix(ya, u, conv_w, cache_rows):
    rb = _tile_first_tail
    nc = D_MODEL // MIX_TN
    cb0 = (GLA_DK * 2 + D_MODEL * 2) // D_MODEL
    col = lambda grp: (lambda cj, i: (rb(i), grp * nc + cj))
    top = lambda cj, i: (0, cj)
    return pl.pallas_call(
        _mix_kernel,
        grid=(nc, N_TILES),
        in_specs=[
            pl.BlockSpec((ROW_TILE, MIX_TN), col(0)),
            pl.BlockSpec((ROW_TILE, MIX_TN), col(cb0)),
            pl.BlockSpec((ROW_TILE, MIX_TN), col(cb0 + 1)),
            pl.BlockSpec((ROW_TILE, MIX_TN), col(cb0 + 2)),
            pl.BlockSpec((ROW_TILE, MIX_TN), col(cb0 + 3)),
            pl.BlockSpec((ROW_TILE, MIX_TN), col(cb0 + 4)),
            pl.BlockSpec((3, MIX_TN), top),
            pl.BlockSpec((2 * DEC_BATCH, MIX_TN), top),
        ],
        out_specs=[
            pl.BlockSpec((ROW_TILE, MIX_TN), col(0)),
            pl.BlockSpec((ROW_TILE, MIX_TN), top),
            pl.BlockSpec((8, MIX_TN), top),
        ],
        out_shape=[
            jax.ShapeDtypeStruct((ROWS, D_MODEL), jnp.bfloat16),
            jax.ShapeDtypeStruct((ROW_TILE, D_MODEL), jnp.float32),
            jax.ShapeDtypeStruct((8, D_MODEL), jnp.float32),
        ],
        scratch_shapes=[pltpu.VMEM((8, MIX_TN), jnp.float32)],
        compiler_params=pltpu.CompilerParams(
            dimension_semantics=("arbitrary", "arbitrary"), vmem_limit_bytes=VMEM_LIMIT),
        name="conv_mix",
    )(ya, u, u, u, u, u, conv_w, cache_rows)


SUB_ROWS = 128


def _out_proj_kernel(m_ref, xn_ref, w_ref, lg_ref, lb_ref, h_ref, hb_ref):
    for r in range(0, ROW_TILE, SUB_ROWS):
        rows = slice(r, r + SUB_ROWS)
        mix = jnp.dot(m_ref[rows, :], w_ref[...], preferred_element_type=jnp.float32)
        h = _layer_norm(ALPHA * xn_ref[rows, :] + mix, lg_ref[...], lb_ref[...])
        h_ref[rows, :] = h
        hb_ref[rows, :] = h.astype(jnp.bfloat16)


def _out_proj(m, xn, w_out, ln_g, ln_b):
    row = lambda i: (i, 0)
    full = lambda i: (0, 0)
    return pl.pallas_call(
        _out_proj_kernel,
        grid=(N_TILES,),
        in_specs=[
            pl.BlockSpec((ROW_TILE, D_MODEL), row),
            pl.BlockSpec((ROW_TILE, D_MODEL), row),
            pl.BlockSpec((D_MODEL, D_MODEL), full),
            pl.BlockSpec((1, D_MODEL), full),
            pl.BlockSpec((1, D_MODEL), full),
        ],
        out_specs=[pl.BlockSpec((ROW_TILE, D_MODEL), row), pl.BlockSpec((ROW_TILE, D_MODEL), row)],
        out_shape=[
            jax.ShapeDtypeStruct((ROWS, D_MODEL), jnp.float32),
            jax.ShapeDtypeStruct((ROWS, D_MODEL), jnp.bfloat16),
        ],
        compiler_params=pltpu.CompilerParams(
            dimension_semantics=("arbitrary",), vmem_limit_bytes=VMEM_LIMIT),
        name="out_proj_ln",
    )(m, xn, w_out, ln_g, ln_b)


FF_TILE = 512
N_FF_TILES = D_FF // FF_TILE


def _ffn_up_kernel(hb_ref, wg_ref, wu_ref, act_ref, wg_bf, wu_bf):
    @pl.when(pl.program_id(1) == 0)
    def _cast():
        for r in range(0, D_MODEL, W_SLAB):
            wg_bf[r:r + W_SLAB, :] = wg_ref[r:r + W_SLAB, :].astype(jnp.bfloat16)
            wu_bf[r:r + W_SLAB, :] = wu_ref[r:r + W_SLAB, :].astype(jnp.bfloat16)

    hb = hb_ref[...]
    gate = jnp.dot(hb, wg_bf[...], preferred_element_type=jnp.float32)
    up = jnp.dot(hb, wu_bf[...], preferred_element_type=jnp.float32)
    act_ref[...] = (gate * _sigmoid(gate) * up).astype(jnp.bfloat16)


def _ffn_up(hb, w_ffn_in):
    return pl.pallas_call(
        _ffn_up_kernel,
        grid=(N_FF_TILES, N_TILES),
        in_specs=[
            pl.BlockSpec((ROW_TILE, D_MODEL), lambda j, i: (i, 0)),
            pl.BlockSpec((None, D_MODEL, FF_TILE), lambda j, i: (0, 0, j)),
            pl.BlockSpec((None, D_MODEL, FF_TILE), lambda j, i: (0, 0, N_FF_TILES + j)),
        ],
        out_specs=pl.BlockSpec((ROW_TILE, FF_TILE), lambda j, i: (i, j)),
        out_shape=jax.ShapeDtypeStruct((ROWS, D_FF), jnp.bfloat16),
        scratch_shapes=[pltpu.VMEM((D_MODEL, FF_TILE), jnp.bfloat16),
                        pltpu.VMEM((D_MODEL, FF_TILE), jnp.bfloat16)],
        compiler_params=pltpu.CompilerParams(
            dimension_semantics=("arbitrary", "arbitrary"), vmem_limit_bytes=VMEM_LIMIT),
        name="ffn_up",
    )(hb, w_ffn_in, w_ffn_in)


DOWN_TILE = 256
N_DOWN_PROMPT = SEQ // DOWN_TILE
N_DOWN_TILES = ROWS // DOWN_TILE


def _ffn_down_kernel(act_ref, h_ref, wd_ref, lg_ref, lb_ref, yp_ref, yt_ref):
    i = pl.program_id(0)

    def rows_out(o_ref):
        for r in range(0, DOWN_TILE, SUB_ROWS):
            rows = slice(r, r + SUB_ROWS)
            f = jnp.dot(act_ref[rows, :], wd_ref[...], preferred_element_type=jnp.float32)
            o_ref[rows, :] = _layer_norm(ALPHA * h_ref[rows, :] + f, lg_ref[...], lb_ref[...])

    @pl.when(i < N_DOWN_PROMPT)
    def _prompt():
        rows_out(yp_ref)

    @pl.when(i >= N_DOWN_PROMPT)
    def _tail():
        rows_out(yt_ref)


def _ffn_down(act, h, w_down, ln_g, ln_b):
    row = lambda i: (i, 0)
    full = lambda i: (0, 0)
    return pl.pallas_call(
        _ffn_down_kernel,
        grid=(N_DOWN_TILES,),
        in_specs=[
            pl.BlockSpec((DOWN_TILE, D_FF), row),
            pl.BlockSpec((DOWN_TILE, D_MODEL), row),
            pl.BlockSpec((D_FF, D_MODEL), full, pipeline_mode=pl.Buffered(1)),
            pl.BlockSpec((1, D_MODEL), full),
            pl.BlockSpec((1, D_MODEL), full),
        ],
        out_specs=[
            pl.BlockSpec((None, DOWN_TILE, D_MODEL), lambda i: (0, jnp.minimum(i, N_DOWN_PROMPT - 1), 0)),
            pl.BlockSpec((DOWN_TILE, D_MODEL), lambda i: (jnp.maximum(i - N_DOWN_PROMPT, 0), 0)),
        ],
        out_shape=[
            jax.ShapeDtypeStruct((1, SEQ, D_MODEL), jnp.float32),
            jax.ShapeDtypeStruct((ROW_TILE, D_MODEL), jnp.float32),
        ],
        compiler_params=pltpu.CompilerParams(
            dimension_semantics=("arbitrary",), vmem_limit_bytes=VMEM_LIMIT),
        name="ffn_down_ln",
    )(act, h, w_down, ln_g, ln_b)


def kernel(x_prompt, x_sample, state_gla, cache_conv, meta_tokens, ln_in_g, ln_in_b, w_in, w_gate_up, b_gate, gla_norm, conv_w, w_out, ln1_g, ln1_b, w_ffn_in, w_ffn_out, ln2_g, ln2_b):
    f32, bf16 = jnp.float32, jnp.bfloat16
    tail = jnp.concatenate([
        jnp.zeros((META_ROW0, D_MODEL), f32),
        meta_tokens.astype(f32),
        x_sample.reshape(SAMPLE_ROWS, D_MODEL),
        jnp.zeros((ROW_TILE - SAMPLE_ROW0 - SAMPLE_ROWS, D_MODEL), f32),
    ], axis=0)
    w_gu = jnp.pad(w_gate_up[0], ((0, LANE - GATE_RANK), (0, 0))).astype(bf16)
    vec = lambda p: p.reshape(1, -1).astype(f32)

    xn, xb, g = _ln_gate(x_prompt, tail, vec(ln_in_g), vec(ln_in_b), w_in, w_gu, vec(b_gate[0]))
    u = _proj(xb, w_in)
    ya, s_prompt, s_sample = _gla(u, g, vec(gla_norm[0]), state_gla[0])
    m, c_tail, c_last = _mix(ya, u, conv_w[0], cache_conv[0].reshape(2 * DEC_BATCH, D_MODEL))
    h, hb = _out_proj(m, xn, w_out[0].astype(bf16), vec(ln1_g[0]), vec(ln1_b[0]))
    act = _ffn_up(hb, w_ffn_in)
    y_prompt, y_tail = _ffn_down(act, h, w_ffn_out[0].astype(bf16), vec(ln2_g[0]), vec(ln2_b[0]))

    y_sample = y_tail[SAMPLE_ROW0:SAMPLE_ROW0 + SAMPLE_ROWS].reshape(DEC_BATCH, DEC_SEQ, D_MODEL)
    cache_prompt = c_last[6:8][None, None]
    cache_sample = c_tail[SAMPLE_ROW0:SAMPLE_ROW0 + SAMPLE_ROWS].reshape(
        DEC_BATCH, DEC_SEQ, D_MODEL)[:, DEC_SEQ - 2:][None]
    return (y_prompt, y_sample, s_prompt[None, None], cache_prompt, s_sample[None], cache_sample)
```

```python
import functools

import jax
import jax.numpy as jnp
from jax import lax
from jax.experimental import pallas as pl
from jax.experimental.pallas import tpu as pltpu

D_MODEL = 2048
SEQ = 8192
DEC_BATCH = 16
DEC_SEQ = 16
CHUNK = 64
N_META = 16
GLA_HEADS = 4
HEAD_K = 256
HEAD_V = 512
GLA_DK = GLA_HEADS * HEAD_K
GATE_RANK = 16
GATE_TEMP = 16.0
D_FF = 5632
ALPHA = 2.0 ** 0.25
LN_EPS = 1e-5
RMS_EPS = 1e-6

ROW_TILE = 512
N_PROMPT_TILES = SEQ // ROW_TILE
N_TILES = N_PROMPT_TILES + 1
ROWS = N_TILES * ROW_TILE
META_ROW0 = CHUNK - N_META
SAMPLE_ROW0 = CHUNK
SAMPLE_ROWS = DEC_BATCH * DEC_SEQ
LANE = 128
D_MAIN = 16384
GATE_COL0 = 2 * GLA_DK + 2 * D_MODEL
VMEM_LIMIT = 56 * 1024 * 1024

_NT = (((1,), (1,)), ((), ()))
_TN = (((0,), (0,)), ((), ()))


def _tile_first_tail(i):
    return (i + N_PROMPT_TILES) % N_TILES


def _layer_norm(x, g, b):
    mu = jnp.mean(x, axis=-1, keepdims=True)
    xc = x - mu
    var = jnp.mean(xc * xc, axis=-1, keepdims=True)
    return xc * lax.rsqrt(var + LN_EPS) * g + b


def _log_sigmoid(x):
    return jnp.minimum(x, 0.0) - jnp.log1p(jnp.exp(-jnp.abs(x)))


def _sigmoid(x):
    return 1.0 / (1.0 + jnp.exp(-x))


def _ln_gate_kernel(xp_ref, xt_ref, lg_ref, lb_ref, wa_ref, wgu_ref, bg_ref, xn_ref, xb_ref, g_ref):
    x = jnp.where(pl.program_id(0) < N_PROMPT_TILES, xp_ref[...], xt_ref[...])
    xn = _layer_norm(x, lg_ref[...], lb_ref[...])
    xn_ref[...] = xn
    xb = xn.astype(jnp.bfloat16)
    xb_ref[...] = xb
    a = lax.dot_general(xb, wa_ref[...].astype(jnp.bfloat16), _NT, preferred_element_type=jnp.float32)
    z = jnp.dot(a.astype(jnp.bfloat16), wgu_ref[...], preferred_element_type=jnp.float32)
    g_ref[...] = _log_sigmoid(z + bg_ref[...]) * (1.0 / GATE_TEMP)


def _ln_gate(x_prompt, x_tail, ln_g, ln_b, w_in_t, w_gu, b_gate):
    row = lambda i: (i, 0)
    full = lambda i: (0, 0)
    return pl.pallas_call(
        _ln_gate_kernel,
        grid=(N_TILES,),
        in_specs=[
            pl.BlockSpec((None, ROW_TILE, D_MODEL), lambda i: (0, jnp.minimum(i, N_PROMPT_TILES - 1), 0)),
            pl.BlockSpec((ROW_TILE, D_MODEL), full),
            pl.BlockSpec((1, D_MODEL), full),
            pl.BlockSpec((1, D_MODEL), full),
            pl.BlockSpec((None, LANE, D_MODEL), lambda i: (0, GATE_COL0 // LANE, 0)),
            pl.BlockSpec((LANE, GLA_DK), full),
            pl.BlockSpec((1, GLA_DK), full),
        ],
        out_specs=[
            pl.BlockSpec((ROW_TILE, D_MODEL), row),
            pl.BlockSpec((ROW_TILE, D_MODEL), row),
            pl.BlockSpec((ROW_TILE, GLA_DK), row),
        ],
        out_shape=[
            jax.ShapeDtypeStruct((ROWS, D_MODEL), jnp.float32),
            jax.ShapeDtypeStruct((ROWS, D_MODEL), jnp.bfloat16),
            jax.ShapeDtypeStruct((ROWS, GLA_DK), jnp.float32),
        ],
        compiler_params=pltpu.CompilerParams(
            dimension_semantics=("arbitrary",), vmem_limit_bytes=VMEM_LIMIT),
        name="ln_gate",
    )(x_prompt, x_tail, ln_g, ln_b, w_in_t, w_gu, b_gate)


PROJ_TN = 1024
N_PROJ_TILES = D_MAIN // PROJ_TN
N_ALIGNED_PROJ_TILES = GATE_COL0 // PROJ_TN
W_SLAB = 256
W_SHIFT_SLAB = 112
BIG_ROW_TILE = 1088
N_BIG_TILES = ROWS // BIG_ROW_TILE


def _proj_kernel(x_ref, w_ref, wx_ref, o_ref, wbf):
    j = pl.program_id(0)
    i = pl.program_id(1)

    @pl.when((i == 0) & (j < N_ALIGNED_PROJ_TILES))
    def _cast():
        for r in range(0, PROJ_TN, W_SLAB):
            wbf[r:r + W_SLAB, :] = w_ref[r:r + W_SLAB, :].astype(jnp.bfloat16)

    @pl.when((i == 0) & (j >= N_ALIGNED_PROJ_TILES))
    def _cast_shifted():
        for r in range(0, PROJ_TN - GATE_RANK, W_SHIFT_SLAB):
            wbf[r:r + W_SHIFT_SLAB, :] = w_ref[r + GATE_RANK:r + GATE_RANK + W_SHIFT_SLAB, :].astype(jnp.bfloat16)
        wbf[PROJ_TN - GATE_RANK:, :] = wx_ref[...].astype(jnp.bfloat16)

    o_ref[...] = lax.dot_general(x_ref[...], wbf[...], _NT, preferred_element_type=jnp.float32)


def _proj(xb, w_in_t):
    return pl.pallas_call(
        _proj_kernel,
        grid=(N_PROJ_TILES, N_BIG_TILES),
        in_specs=[
            pl.BlockSpec((BIG_ROW_TILE, D_MODEL), lambda j, i: (i, 0)),
            pl.BlockSpec((None, PROJ_TN, D_MODEL), lambda j, i: (0, j, 0)),
            pl.BlockSpec((None, GATE_RANK, D_MODEL), lambda j, i: (0, (j + 1) * (PROJ_TN // GATE_RANK), 0)),
        ],
        out_specs=pl.BlockSpec((BIG_ROW_TILE, PROJ_TN), lambda j, i: (i, j)),
        out_shape=jax.ShapeDtypeStruct((ROWS, D_MAIN), jnp.float32),
        scratch_shapes=[pltpu.VMEM((PROJ_TN, D_MODEL), jnp.bfloat16)],
        compiler_params=pltpu.CompilerParams(
            dimension_semantics=("arbitrary", "arbitrary"), vmem_limit_bytes=VMEM_LIMIT),
        name="proj_in",
    )(xb, w_in_t, w_in_t)


_HALF_SIZES = (32, 16, 8, 4, 2, 1)


def _row_of_block(b, block, r):
    n, w = b.shape
    b3 = b.reshape(n // block, block, w)
    return jnp.broadcast_to(b3[:, r:r + 1, :], b3.shape).reshape(n, w)


def _gla_prepare(q, k, g):
    n, w = q.shape
    row = lax.broadcasted_iota(jnp.int32, (n, w), 0)
    pos = row & (CHUNK - 1)
    b = g
    for sh in (1, 2, 4, 8, 16, 32):
        b = b + jnp.where(pos >= sh, pltpu.roll(b, sh, 0), 0.0)
    qs = q * (HEAD_K ** -0.5)
    b_last = _row_of_block(b, CHUNK, CHUNK - 1)
    q_inter = (qs * jnp.exp(b)).astype(jnp.bfloat16)
    k_dec = (k * jnp.exp(b_last - b)).astype(jnp.bfloat16)
    levels = []
    for s in _HALF_SIZES:
        if s >= 4:
            b_mid = _row_of_block(b, 2 * s, s - 1)
            upper = (row & (2 * s - 1)) >= s
            d = jnp.where(upper, b - b_mid, b_mid - b)
        elif s == 2:
            p4 = row & 3
            g_prev = pltpu.roll(g, 1, 0)
            g_next = pltpu.roll(g, n - 1, 0)
            d = jnp.where(p4 == 0, g_next, jnp.where(p4 == 1, 0.0, jnp.where(p4 == 2, g, g + g_prev)))
        else:
            d = jnp.where((row & 1) == 1, g, 0.0)
        e = jnp.exp(d)
        levels.append(((qs * e).astype(jnp.bfloat16), (k * e).astype(jnp.bfloat16)))
    levels.append((qs.astype(jnp.bfloat16), k.astype(jnp.bfloat16)))
    return q_inter, k_dec, b, levels


def _score_masks():
    ii = lax.broadcasted_iota(jnp.int32, (CHUNK, CHUNK), 0)
    jj = lax.broadcasted_iota(jnp.int32, (CHUNK, CHUNK), 1)
    masks = []
    for s in _HALF_SIZES:
        blk = 2 * s
        same = (ii & ~(blk - 1)) == (jj & ~(blk - 1))
        masks.append(same & ((ii & (blk - 1)) >= s) & ((jj & (blk - 1)) < s))
    masks.append(ii == jj)
    return masks


def _gla_chunk(c, q_inter, k_dec, b, levels, v_bf, masks, s_val):
    r0 = c * CHUNK
    sl = slice(r0, r0 + CHUNK)
    scores = jnp.zeros((CHUNK, CHUNK), jnp.float32)
    for (qe, ke), m in zip(levels, masks):
        p = lax.dot_general(qe[sl], ke[sl], _NT, preferred_element_type=jnp.float32)
        scores = jnp.where(m, p, scores)
    vc = v_bf[sl]
    o = jnp.dot(scores.astype(jnp.bfloat16), vc, preferred_element_type=jnp.float32)
    o = o + jnp.dot(q_inter[sl], s_val.astype(jnp.bfloat16), preferred_element_type=jnp.float32)
    eb = jnp.exp(b[r0 + CHUNK - 1:r0 + CHUNK, :])
    eb_t = jnp.transpose(jnp.broadcast_to(eb, (LANE, HEAD_K)))
    decay = jnp.concatenate([eb_t] * (HEAD_V // LANE), axis=1)
    ds = lax.dot_general(k_dec[sl], vc, _TN, preferred_element_type=jnp.float32)
    return o, decay * s_val + ds


def _gla_finish(o, r, gn):
    o = o * lax.rsqrt(jnp.mean(o * o, axis=-1, keepdims=True) + RMS_EPS)
    return o * gn * (r * _sigmoid(r))


def _gla_kernel(q_ref, k_ref, v_ref, r_ref, g_ref, gn_ref, sin_ref,
                y_ref, sp_ref, ss_ref, s_scr):
    i = pl.program_id(1)
    masks = _score_masks()
    gn = gn_ref[...]

    @pl.when(i == 0)
    def _tail():
        def padded(x):
            parts = [x[0:CHUNK]]
            zeros = jnp.zeros((META_ROW0, x.shape[1]), x.dtype)
            for s in range(DEC_BATCH):
                parts += [zeros, x[SAMPLE_ROW0 + s * DEC_SEQ:SAMPLE_ROW0 + (s + 1) * DEC_SEQ]]
            return jnp.concatenate(parts, axis=0)

        n_chunks = 1 + DEC_BATCH
        n = n_chunks * CHUNK
        q = padded(q_ref[...])
        k = padded(k_ref[...])
        g = padded(g_ref[...])
        v = padded(v_ref[...])
        live = (lax.broadcasted_iota(jnp.int32, (n, HEAD_K), 0) & (CHUNK - 1)) >= META_ROW0
        k = jnp.where(live, k, 0.0)
        g = jnp.where(live, g, 0.0)
        q_inter, k_dec, b, levels = _gla_prepare(q, k, g)
        v_bf = v.astype(jnp.bfloat16)
        outs = [jnp.zeros((META_ROW0, HEAD_V), jnp.float32)]
        o, s_meta = _gla_chunk(0, q_inter, k_dec, b, levels, v_bf, masks,
                               jnp.zeros((HEAD_K, HEAD_V), jnp.float32))
        s_scr[...] = s_meta
        outs.append(o[META_ROW0:])
        for s in range(DEC_BATCH):
            o, s_new = _gla_chunk(1 + s, q_inter, k_dec, b, levels, v_bf, masks, sin_ref[s])
            ss_ref[s] = s_new
            outs.append(o[META_ROW0:])
        outs.append(jnp.zeros((ROW_TILE - SAMPLE_ROW0 - SAMPLE_ROWS, HEAD_V), jnp.float32))
        y_ref[...] = _gla_finish(jnp.concatenate(outs, axis=0), r_ref[...], gn)

    @pl.when(i > 0)
    def _prompt():
        q_inter, k_dec, b, levels = _gla_prepare(q_ref[...], k_ref[...], g_ref[...])
        v_bf = v_ref[...].astype(jnp.bfloat16)
        outs = []
        for c in range(ROW_TILE // CHUNK):
            o, s_new = _gla_chunk(c, q_inter, k_dec, b, levels, v_bf, masks, s_scr[...])
            s_scr[...] = s_new
            outs.append(o)
        y_ref[...] = _gla_finish(jnp.concatenate(outs, axis=0), r_ref[...], gn)

    @pl.when(i == N_TILES - 1)
    def _final_state():
        sp_ref[...] = s_scr[...]


def _gla(u, g, gla_norm, state_gla):
    rb = _tile_first_tail
    kq = GLA_DK // HEAD_K
    return pl.pallas_call(
        _gla_kernel,
        grid=(GLA_HEADS, N_TILES),
        in_specs=[
            pl.BlockSpec((ROW_TILE, HEAD_K), lambda h, i: (rb(i), h)),
            pl.BlockSpec((ROW_TILE, HEAD_K), lambda h, i: (rb(i), kq + h)),
            pl.BlockSpec((ROW_TILE, HEAD_V), lambda h, i: (rb(i), kq + h)),
            pl.BlockSpec((ROW_TILE, HEAD_V), lambda h, i: (rb(i), 2 * kq + h)),
            pl.BlockSpec((ROW_TILE, HEAD_K), lambda h, i: (rb(i), h)),
            pl.BlockSpec((1, HEAD_V), lambda h, i: (0, h)),
            pl.BlockSpec((DEC_BATCH, None, HEAD_K, HEAD_V), lambda h, i: (0, h, 0, 0)),
        ],
        out_specs=[
            pl.BlockSpec((ROW_TILE, HEAD_V), lambda h, i: (rb(i), h)),
            pl.BlockSpec((None, HEAD_K, HEAD_V), lambda h, i: (h, 0, 0)),
            pl.BlockSpec((DEC_BATCH, None, HEAD_K, HEAD_V), lambda h, i: (0, h, 0, 0)),
        ],
        out_shape=[
            jax.ShapeDtypeStruct((ROWS, D_MODEL), jnp.float32),
            jax.ShapeDtypeStruct((GLA_HEADS, HEAD_K, HEAD_V), jnp.float32),
            jax.ShapeDtypeStruct((DEC_BATCH, GLA_HEADS, HEAD_K, HEAD_V), jnp.float32),
        ],
        scratch_shapes=[pltpu.VMEM((HEAD_K, HEAD_V), jnp.float32)],
        compiler_params=pltpu.CompilerParams(
            dimension_semantics=("arbitrary", "arbitrary"), vmem_limit_bytes=VMEM_LIMIT),
        name="gla",
    )(u, u, u, u, g, gla_norm, state_gla)


MIX_TN = 512


def _mix_kernel(ya_ref, cb_ref, cc_ref, cx_ref, ga_ref, gb_ref, cw_ref, cache_ref,
                m_ref, ctail_ref, clast_ref, carry):
    i = pl.program_id(1)
    c = cc_ref[...] * cx_ref[...]
    row = lax.broadcasted_iota(jnp.int32, (ROW_TILE, MIX_TN), 0)
    prev1 = pltpu.roll(c, 1, 0)
    prev2 = pltpu.roll(c, 2, 0)
    w0 = cw_ref[0:1, :]
    w1 = cw_ref[1:2, :]
    w2 = cw_ref[2:3, :]

    def finish(p1, p2):
        conv = w0 * p2 + w1 * p1 + w2 * c
        m = _sigmoid(ga_ref[...]) * ya_ref[...] + _sigmoid(gb_ref[...]) * (cb_ref[...] * conv)
        m_ref[...] = m.astype(jnp.bfloat16)

    @pl.when(i == 0)
    def _tail():
        def stream_rows(j):
            parts = [jnp.zeros((SAMPLE_ROW0, MIX_TN), jnp.float32)]
            for s in range(DEC_BATCH):
                parts.append(jnp.broadcast_to(cache_ref[2 * s + j:2 * s + j + 1, :], (DEC_SEQ, MIX_TN)))
            parts.append(jnp.zeros((ROW_TILE - SAMPLE_ROW0 - SAMPLE_ROWS, MIX_TN), jnp.float32))
            return jnp.concatenate(parts, axis=0)

        old = stream_rows(0)
        new = stream_rows(1)
        in_sample = (row >= SAMPLE_ROW0) & (row < SAMPLE_ROW0 + SAMPLE_ROWS)
        p16 = (row - SAMPLE_ROW0) & (DEC_SEQ - 1)
        first = in_sample & (p16 == 0)
        second = in_sample & (p16 == 1)
        p1 = jnp.where(row == META_ROW0, 0.0, jnp.where(first, new, prev1))
        p2 = jnp.where((row == META_ROW0) | (row == META_ROW0 + 1), 0.0,
                       jnp.where(first, old, jnp.where(second, new, prev2)))
        finish(p1, p2)
        ctail_ref[...] = c
        carry[...] = c[CHUNK - 8:CHUNK]

    @pl.when(i > 0)
    def _prompt():
        hist = carry[...]
        p1 = jnp.where(row == 0, jnp.broadcast_to(hist[7:8], c.shape), prev1)
        p2 = jnp.where(row == 0, jnp.broadcast_to(hist[6:7], c.shape),
                       jnp.where(row == 1, jnp.broadcast_to(hist[7:8], c.shape), prev2))
        finish(p1, p2)
        carry[...] = c[ROW_TILE - 8:]

    @pl.when(i == N_TILES - 1)
    def _last():
        clast_ref[...] = c[ROW_TILE - 8:]


def _mix(ya, u, conv_w, cache_rows):
    rb = _tile_first_tail
    nc = D_MODEL // MIX_TN
    cb0 = (GLA_DK * 2 + D_MODEL * 2) // D_MODEL
    col = lambda grp: (lambda cj, i: (rb(i), grp * nc + cj))
    top = lambda cj, i: (0, cj)
    return pl.pallas_call(
        _mix_kernel,
        grid=(nc, N_TILES),
        in_specs=[
            pl.BlockSpec((ROW_TILE, MIX_TN), col(0)),
            pl.BlockSpec((ROW_TILE, MIX_TN), col(cb0)),
            pl.BlockSpec((ROW_TILE, MIX_TN), col(cb0 + 1)),
            pl.BlockSpec((ROW_TILE, MIX_TN), col(cb0 + 2)),
            pl.BlockSpec((ROW_TILE, MIX_TN), col(cb0 + 3)),
            pl.BlockSpec((ROW_TILE, MIX_TN), col(cb0 + 4)),
            pl.BlockSpec((3, MIX_TN), top),
            pl.BlockSpec((2 * DEC_BATCH, MIX_TN), top),
        ],
        out_specs=[
            pl.BlockSpec((ROW_TILE, MIX_TN), col(0)),
            pl.BlockSpec((ROW_TILE, MIX_TN), top),
            pl.BlockSpec((8, MIX_TN), top),
        ],
        out_shape=[
            jax.ShapeDtypeStruct((ROWS, D_MODEL), jnp.bfloat16),
            jax.ShapeDtypeStruct((ROW_TILE, D_MODEL), jnp.float32),
            jax.ShapeDtypeStruct((8, D_MODEL), jnp.float32),
        ],
        scratch_shapes=[pltpu.VMEM((8, MIX_TN), jnp.float32)],
        compiler_params=pltpu.CompilerParams(
            dimension_semantics=("arbitrary", "arbitrary"), vmem_limit_bytes=VMEM_LIMIT),
        name="conv_mix",
    )(ya, u, u, u, u, u, conv_w, cache_rows)


SUB_ROWS = 128


def _out_proj_kernel(m_ref, xn_ref, w_ref, lg_ref, lb_ref, h_ref, hb_ref):
    for r in range(0, ROW_TILE, SUB_ROWS):
        rows = slice(r, r + SUB_ROWS)
        mix = jnp.dot(m_ref[rows, :], w_ref[...], preferred_element_type=jnp.float32)
        h = _layer_norm(ALPHA * xn_ref[rows, :] + mix, lg_ref[...], lb_ref[...])
        h_ref[rows, :] = h
        hb_ref[rows, :] = h.astype(jnp.bfloat16)


def _out_proj(m, xn, w_out, ln_g, ln_b):
    row = lambda i: (i, 0)
    full = lambda i: (0, 0)
    return pl.pallas_call(
        _out_proj_kernel,
        grid=(N_TILES,),
        in_specs=[
            pl.BlockSpec((ROW_TILE, D_MODEL), row),
            pl.BlockSpec((ROW_TILE, D_MODEL), row),
            pl.BlockSpec((D_MODEL, D_MODEL), full),
            pl.BlockSpec((1, D_MODEL), full),
            pl.BlockSpec((1, D_MODEL), full),
        ],
        out_specs=[pl.BlockSpec((ROW_TILE, D_MODEL), row), pl.BlockSpec((ROW_TILE, D_MODEL), row)],
        out_shape=[
            jax.ShapeDtypeStruct((ROWS, D_MODEL), jnp.float32),
            jax.ShapeDtypeStruct((ROWS, D_MODEL), jnp.bfloat16),
        ],
        compiler_params=pltpu.CompilerParams(
            dimension_semantics=("arbitrary",), vmem_limit_bytes=VMEM_LIMIT),
        name="out_proj_ln",
    )(m, xn, w_out, ln_g, ln_b)


FF_TILE = 512
N_FF_TILES = D_FF // FF_TILE


def _ffn_up_kernel(hb_ref, wg_ref, wu_ref, act_ref, wg_bf, wu_bf):
    @pl.when(pl.program_id(1) == 0)
    def _cast():
        for r in range(0, D_MODEL, W_SLAB):
            wg_bf[r:r + W_SLAB, :] = wg_ref[r:r + W_SLAB, :].astype(jnp.bfloat16)
            wu_bf[r:r + W_SLAB, :] = wu_ref[r:r + W_SLAB, :].astype(jnp.bfloat16)

    hb = hb_ref[...]
    gate = jnp.dot(hb, wg_bf[...], preferred_element_type=jnp.float32)
    up = jnp.dot(hb, wu_bf[...], preferred_element_type=jnp.float32)
    act_ref[...] = (gate * _sigmoid(gate) * up).astype(jnp.bfloat16)


def _ffn_up(hb, w_ffn_in):
    return pl.pallas_call(
        _ffn_up_kernel,
        grid=(N_FF_TILES, N_BIG_TILES),
        in_specs=[
            pl.BlockSpec((BIG_ROW_TILE, D_MODEL), lambda j, i: (i, 0)),
            pl.BlockSpec((None, D_MODEL, FF_TILE), lambda j, i: (0, 0, j)),
            pl.BlockSpec((None, D_MODEL, FF_TILE), lambda j, i: (0, 0, N_FF_TILES + j)),
        ],
        out_specs=pl.BlockSpec((BIG_ROW_TILE, FF_TILE), lambda j, i: (i, j)),
        out_shape=jax.ShapeDtypeStruct((ROWS, D_FF), jnp.bfloat16),
        scratch_shapes=[pltpu.VMEM((D_MODEL, FF_TILE), jnp.bfloat16),
                        pltpu.VMEM((D_MODEL, FF_TILE), jnp.bfloat16)],
        compiler_params=pltpu.CompilerParams(
            dimension_semantics=("arbitrary", "arbitrary"), vmem_limit_bytes=VMEM_LIMIT),
        name="ffn_up",
    )(hb, w_ffn_in, w_ffn_in)


DOWN_TILE = 256
N_DOWN_PROMPT = SEQ // DOWN_TILE
N_DOWN_TILES = ROWS // DOWN_TILE


def _ffn_down_kernel(act_ref, h_ref, wd_ref, lg_ref, lb_ref, yp_ref, yt_ref):
    i = pl.program_id(0)

    def rows_out(o_ref):
        for r in range(0, DOWN_TILE, SUB_ROWS):
            rows = slice(r, r + SUB_ROWS)
            f = jnp.dot(act_ref[rows, :], wd_ref[...], preferred_element_type=jnp.float32)
            o_ref[rows, :] = _layer_norm(ALPHA * h_ref[rows, :] + f, lg_ref[...], lb_ref[...])

    @pl.when(i < N_DOWN_PROMPT)
    def _prompt():
        rows_out(yp_ref)

    @pl.when(i >= N_DOWN_PROMPT)
    def _tail():
        rows_out(yt_ref)


def _ffn_down(act, h, w_down, ln_g, ln_b):
    row = lambda i: (i, 0)
    full = lambda i: (0, 0)
    return pl.pallas_call(
        _ffn_down_kernel,
        grid=(N_DOWN_TILES,),
        in_specs=[
            pl.BlockSpec((DOWN_TILE, D_FF), row),
            pl.BlockSpec((DOWN_TILE, D_MODEL), row),
            pl.BlockSpec((D_FF, D_MODEL), full, pipeline_mode=pl.Buffered(1)),
            pl.BlockSpec((1, D_MODEL), full),
            pl.BlockSpec((1, D_MODEL), full),
        ],
        out_specs=[
            pl.BlockSpec((None, DOWN_TILE, D_MODEL), lambda i: (0, jnp.minimum(i, N_DOWN_PROMPT - 1), 0)),
            pl.BlockSpec((DOWN_TILE, D_MODEL), lambda i: (jnp.maximum(i - N_DOWN_PROMPT, 0), 0)),
        ],
        out_shape=[
            jax.ShapeDtypeStruct((1, SEQ, D_MODEL), jnp.float32),
            jax.ShapeDtypeStruct((ROW_TILE, D_MODEL), jnp.float32),
        ],
        compiler_params=pltpu.CompilerParams(
            dimension_semantics=("arbitrary",), vmem_limit_bytes=VMEM_LIMIT),
        name="ffn_down_ln",
    )(act, h, w_down, ln_g, ln_b)


def kernel(x_prompt, x_sample, state_gla, cache_conv, meta_tokens, ln_in_g, ln_in_b, w_in, w_gate_up, b_gate, gla_norm, conv_w, w_out, ln1_g, ln1_b, w_ffn_in, w_ffn_out, ln2_g, ln2_b):
    f32, bf16 = jnp.float32, jnp.bfloat16
    tail = jnp.concatenate([
        jnp.zeros((META_ROW0, D_MODEL), f32),
        meta_tokens.astype(f32),
        x_sample.reshape(SAMPLE_ROWS, D_MODEL),
        jnp.zeros((ROW_TILE - SAMPLE_ROW0 - SAMPLE_ROWS, D_MODEL), f32),
    ], axis=0)
    w_gu = jnp.pad(w_gate_up[0], ((0, LANE - GATE_RANK), (0, 0))).astype(bf16)
    vec = lambda p: p.reshape(1, -1).astype(f32)

    w_in_t = jnp.swapaxes(w_in, 1, 2)
    xn, xb, g = _ln_gate(x_prompt, tail, vec(ln_in_g), vec(ln_in_b), w_in_t, w_gu, vec(b_gate[0]))
    u = _proj(xb, w_in_t)
    ya, s_prompt, s_sample = _gla(u, g, vec(gla_norm[0]), state_gla[0])
    m, c_tail, c_last = _mix(ya, u, conv_w[0], cache_conv[0].reshape(2 * DEC_BATCH, D_MODEL))
    h, hb = _out_proj(m, xn, w_out[0].astype(bf16), vec(ln1_g[0]), vec(ln1_b[0]))
    act = _ffn_up(hb, w_ffn_in)
    y_prompt, y_tail = _ffn_down(act, h, w_ffn_out[0].astype(bf16), vec(ln2_g[0]), vec(ln2_b[0]))

    y_sample = y_tail[SAMPLE_ROW0:SAMPLE_ROW0 + SAMPLE_ROWS].reshape(DEC_BATCH, DEC_SEQ, D_MODEL)
    cache_prompt = c_last[6:8][None, None]
    cache_sample = c_tail[SAMPLE_ROW0:SAMPLE_ROW0 + SAMPLE_ROWS].reshape(
        DEC_BATCH, DEC_SEQ, D_MODEL)[:, DEC_SEQ - 2:][None]
    return (y_prompt, y_sample, s_prompt[None, None], cache_prompt, s_sample[None], cache_sample)
```

```python
import functools

import jax
import jax.numpy as jnp
from jax import lax
from jax.experimental import pallas as pl
from jax.experimental.pallas import tpu as pltpu

D_MODEL = 2048
SEQ = 8192
DEC_BATCH = 16
DEC_SEQ = 16
CHUNK = 64
N_META = 16
GLA_HEADS = 4
HEAD_K = 256
HEAD_V = 512
GLA_DK = GLA_HEADS * HEAD_K
GATE_RANK = 16
GATE_TEMP = 16.0
D_FF = 5632
ALPHA = 2.0 ** 0.25
LN_EPS = 1e-5
RMS_EPS = 1e-6

ROW_TILE = 512
N_PROMPT_TILES = SEQ // ROW_TILE
N_TILES = N_PROMPT_TILES + 1
ROWS = N_TILES * ROW_TILE
META_ROW0 = CHUNK - N_META
SAMPLE_ROW0 = CHUNK
SAMPLE_ROWS = DEC_BATCH * DEC_SEQ
LANE = 128
GATE_COL0 = 2 * GLA_DK + 2 * D_MODEL
VMEM_LIMIT = 56 * 1024 * 1024

_NT = (((1,), (1,)), ((), ()))
_TN = (((0,), (0,)), ((), ()))


def _tile_first_tail(i):
    return (i + N_PROMPT_TILES) % N_TILES


def _layer_norm(x, g, b):
    mu = jnp.mean(x, axis=-1, keepdims=True)
    xc = x - mu
    var = jnp.mean(xc * xc, axis=-1, keepdims=True)
    return xc * lax.rsqrt(var + LN_EPS) * g + b


def _log_sigmoid(x):
    return jnp.minimum(x, 0.0) - jnp.log1p(jnp.exp(-jnp.abs(x)))


def _sigmoid(x):
    return 1.0 / (1.0 + jnp.exp(-x))


def _ln_gate_kernel(xp_ref, xt_ref, lg_ref, lb_ref, wa_ref, wgu_ref, bg_ref, xn_ref, xb_ref, g_ref):
    x = jnp.where(pl.program_id(0) < N_PROMPT_TILES, xp_ref[...], xt_ref[...])
    xn = _layer_norm(x, lg_ref[...], lb_ref[...])
    xn_ref[...] = xn
    xb = xn.astype(jnp.bfloat16)
    xb_ref[...] = xb
    a = lax.dot_general(xb, wa_ref[...].astype(jnp.bfloat16), _NT, preferred_element_type=jnp.float32)
    z = jnp.dot(a.astype(jnp.bfloat16), wgu_ref[...], preferred_element_type=jnp.float32)
    g_ref[...] = _log_sigmoid(z + bg_ref[...]) * (1.0 / GATE_TEMP)


def _ln_gate(x_prompt, x_tail, ln_g, ln_b, w_in_t, w_gu, b_gate):
    row = lambda i: (i, 0)
    full = lambda i: (0, 0)
    return pl.pallas_call(
        _ln_gate_kernel,
        grid=(N_TILES,),
        in_specs=[
            pl.BlockSpec((None, ROW_TILE, D_MODEL), lambda i: (0, jnp.minimum(i, N_PROMPT_TILES - 1), 0)),
            pl.BlockSpec((ROW_TILE, D_MODEL), full),
            pl.BlockSpec((1, D_MODEL), full),
            pl.BlockSpec((1, D_MODEL), full),
            pl.BlockSpec((None, LANE, D_MODEL), lambda i: (0, GATE_COL0 // LANE, 0)),
            pl.BlockSpec((LANE, GLA_DK), full),
            pl.BlockSpec((1, GLA_DK), full),
        ],
        out_specs=[
            pl.BlockSpec((ROW_TILE, D_MODEL), row),
            pl.BlockSpec((ROW_TILE, D_MODEL), row),
            pl.BlockSpec((ROW_TILE, GLA_DK), row),
        ],
        out_shape=[
            jax.ShapeDtypeStruct((ROWS, D_MODEL), jnp.float32),
            jax.ShapeDtypeStruct((ROWS, D_MODEL), jnp.bfloat16),
            jax.ShapeDtypeStruct((ROWS, GLA_DK), jnp.float32),
        ],
        compiler_params=pltpu.CompilerParams(
            dimension_semantics=("arbitrary",), vmem_limit_bytes=VMEM_LIMIT),
        name="ln_gate",
    )(x_prompt, x_tail, ln_g, ln_b, w_in_t, w_gu, b_gate)


PROJ_TN = 1024
N_PROJ_TILES = GATE_COL0 // PROJ_TN
W_SLAB = 256
BIG_ROW_TILE = 1088
N_BIG_TILES = ROWS // BIG_ROW_TILE


def _proj_kernel(x_ref, w_ref, o_ref, wbf):
    @pl.when(pl.program_id(1) == 0)
    def _cast():
        for r in range(0, PROJ_TN, W_SLAB):
            wbf[r:r + W_SLAB, :] = w_ref[r:r + W_SLAB, :].astype(jnp.bfloat16)

    o_ref[...] = lax.dot_general(x_ref[...], wbf[...], _NT, preferred_element_type=jnp.float32)


def _proj(xb, w_in_t):
    return pl.pallas_call(
        _proj_kernel,
        grid=(N_PROJ_TILES, N_BIG_TILES),
        in_specs=[
            pl.BlockSpec((BIG_ROW_TILE, D_MODEL), lambda j, i: (i, 0)),
            pl.BlockSpec((None, PROJ_TN, D_MODEL), lambda j, i: (0, j, 0)),
        ],
        out_specs=pl.BlockSpec((BIG_ROW_TILE, PROJ_TN), lambda j, i: (i, j)),
        out_shape=jax.ShapeDtypeStruct((ROWS, GATE_COL0), jnp.float32),
        scratch_shapes=[pltpu.VMEM((PROJ_TN, D_MODEL), jnp.bfloat16)],
        compiler_params=pltpu.CompilerParams(
            dimension_semantics=("arbitrary", "arbitrary"), vmem_limit_bytes=VMEM_LIMIT),
        name="proj_qkvr",
    )(xb, w_in_t)


_HALF_SIZES = (32, 16, 8, 4, 2, 1)


def _row_of_block(b, block, r):
    n, w = b.shape
    b3 = b.reshape(n // block, block, w)
    return jnp.broadcast_to(b3[:, r:r + 1, :], b3.shape).reshape(n, w)


def _gla_prepare(q, k, g):
    n, w = q.shape
    row = lax.broadcasted_iota(jnp.int32, (n, w), 0)
    pos = row & (CHUNK - 1)
    b = g
    for sh in (1, 2, 4, 8, 16, 32):
        b = b + jnp.where(pos >= sh, pltpu.roll(b, sh, 0), 0.0)
    qs = q * (HEAD_K ** -0.5)
    b_last = _row_of_block(b, CHUNK, CHUNK - 1)
    q_inter = (qs * jnp.exp(b)).astype(jnp.bfloat16)
    k_dec = (k * jnp.exp(b_last - b)).astype(jnp.bfloat16)
    levels = []
    for s in _HALF_SIZES:
        if s >= 4:
            b_mid = _row_of_block(b, 2 * s, s - 1)
            upper = (row & (2 * s - 1)) >= s
            d = jnp.where(upper, b - b_mid, b_mid - b)
        elif s == 2:
            p4 = row & 3
            g_prev = pltpu.roll(g, 1, 0)
            g_next = pltpu.roll(g, n - 1, 0)
            d = jnp.where(p4 == 0, g_next, jnp.where(p4 == 1, 0.0, jnp.where(p4 == 2, g, g + g_prev)))
        else:
            d = jnp.where((row & 1) == 1, g, 0.0)
        e = jnp.exp(d)
        levels.append(((qs * e).astype(jnp.bfloat16), (k * e).astype(jnp.bfloat16)))
    levels.append((qs.astype(jnp.bfloat16), k.astype(jnp.bfloat16)))
    return q_inter, k_dec, b, levels


def _score_masks():
    ii = lax.broadcasted_iota(jnp.int32, (CHUNK, CHUNK), 0)
    jj = lax.broadcasted_iota(jnp.int32, (CHUNK, CHUNK), 1)
    masks = []
    for s in _HALF_SIZES:
        blk = 2 * s
        same = (ii & ~(blk - 1)) == (jj & ~(blk - 1))
        masks.append(same & ((ii & (blk - 1)) >= s) & ((jj & (blk - 1)) < s))
    masks.append(ii == jj)
    return masks


def _gla_chunk(c, q_inter, k_dec, b, levels, v_bf, masks, s_val):
    r0 = c * CHUNK
    sl = slice(r0, r0 + CHUNK)
    scores = jnp.zeros((CHUNK, CHUNK), jnp.float32)
    for (qe, ke), m in zip(levels, masks):
        p = lax.dot_general(qe[sl], ke[sl], _NT, preferred_element_type=jnp.float32)
        scores = jnp.where(m, p, scores)
    vc = v_bf[sl]
    o = jnp.dot(scores.astype(jnp.bfloat16), vc, preferred_element_type=jnp.float32)
    o = o + jnp.dot(q_inter[sl], s_val.astype(jnp.bfloat16), preferred_element_type=jnp.float32)
    eb = jnp.exp(b[r0 + CHUNK - 1:r0 + CHUNK, :])
    eb_t = jnp.transpose(jnp.broadcast_to(eb, (LANE, HEAD_K)))
    decay = jnp.concatenate([eb_t] * (HEAD_V // LANE), axis=1)
    ds = lax.dot_general(k_dec[sl], vc, _TN, preferred_element_type=jnp.float32)
    return o, decay * s_val + ds


def _gla_finish(o, r, gn):
    o = o * lax.rsqrt(jnp.mean(o * o, axis=-1, keepdims=True) + RMS_EPS)
    return o * gn * (r * _sigmoid(r))


def _gla_kernel(q_ref, k_ref, v_ref, r_ref, g_ref, gn_ref, sin_ref,
                y_ref, sp_ref, ss_ref, s_scr):
    i = pl.program_id(1)
    masks = _score_masks()
    gn = gn_ref[...]

    @pl.when(i == 0)
    def _tail():
        def padded(x):
            parts = [x[0:CHUNK]]
            zeros = jnp.zeros((META_ROW0, x.shape[1]), x.dtype)
            for s in range(DEC_BATCH):
                parts += [zeros, x[SAMPLE_ROW0 + s * DEC_SEQ:SAMPLE_ROW0 + (s + 1) * DEC_SEQ]]
            return jnp.concatenate(parts, axis=0)

        n_chunks = 1 + DEC_BATCH
        n = n_chunks * CHUNK
        q = padded(q_ref[...])
        k = padded(k_ref[...])
        g = padded(g_ref[...])
        v = padded(v_ref[...])
        live = (lax.broadcasted_iota(jnp.int32, (n, HEAD_K), 0) & (CHUNK - 1)) >= META_ROW0
        k = jnp.where(live, k, 0.0)
        g = jnp.where(live, g, 0.0)
        q_inter, k_dec, b, levels = _gla_prepare(q, k, g)
        v_bf = v.astype(jnp.bfloat16)
        outs = [jnp.zeros((META_ROW0, HEAD_V), jnp.float32)]
        o, s_meta = _gla_chunk(0, q_inter, k_dec, b, levels, v_bf, masks,
                               jnp.zeros((HEAD_K, HEAD_V), jnp.float32))
        s_scr[...] = s_meta
        outs.append(o[META_ROW0:])
        for s in range(DEC_BATCH):
            o, s_new = _gla_chunk(1 + s, q_inter, k_dec, b, levels, v_bf, masks, sin_ref[s])
            ss_ref[s] = s_new
            outs.append(o[META_ROW0:])
        outs.append(jnp.zeros((ROW_TILE - SAMPLE_ROW0 - SAMPLE_ROWS, HEAD_V), jnp.float32))
        y_ref[...] = _gla_finish(jnp.concatenate(outs, axis=0), r_ref[...], gn)

    @pl.when(i > 0)
    def _prompt():
        q_inter, k_dec, b, levels = _gla_prepare(q_ref[...], k_ref[...], g_ref[...])
        v_bf = v_ref[...].astype(jnp.bfloat16)
        outs = []
        for c in range(ROW_TILE // CHUNK):
            o, s_new = _gla_chunk(c, q_inter, k_dec, b, levels, v_bf, masks, s_scr[...])
            s_scr[...] = s_new
            outs.append(o)
        y_ref[...] = _gla_finish(jnp.concatenate(outs, axis=0), r_ref[...], gn)

    @pl.when(i == N_TILES - 1)
    def _final_state():
        sp_ref[...] = s_scr[...]


def _gla(u, g, gla_norm, state_gla):
    rb = _tile_first_tail
    kq = GLA_DK // HEAD_K
    return pl.pallas_call(
        _gla_kernel,
        grid=(GLA_HEADS, N_TILES),
        in_specs=[
            pl.BlockSpec((ROW_TILE, HEAD_K), lambda h, i: (rb(i), h)),
            pl.BlockSpec((ROW_TILE, HEAD_K), lambda h, i: (rb(i), kq + h)),
            pl.BlockSpec((ROW_TILE, HEAD_V), lambda h, i: (rb(i), kq + h)),
            pl.BlockSpec((ROW_TILE, HEAD_V), lambda h, i: (rb(i), 2 * kq + h)),
            pl.BlockSpec((ROW_TILE, HEAD_K), lambda h, i: (rb(i), h)),
            pl.BlockSpec((1, HEAD_V), lambda h, i: (0, h)),
            pl.BlockSpec((DEC_BATCH, None, HEAD_K, HEAD_V), lambda h, i: (0, h, 0, 0)),
        ],
        out_specs=[
            pl.BlockSpec((ROW_TILE, HEAD_V), lambda h, i: (rb(i), h)),
            pl.BlockSpec((None, HEAD_K, HEAD_V), lambda h, i: (h, 0, 0)),
            pl.BlockSpec((DEC_BATCH, None, HEAD_K, HEAD_V), lambda h, i: (0, h, 0, 0)),
        ],
        out_shape=[
            jax.ShapeDtypeStruct((ROWS, D_MODEL), jnp.float32),
            jax.ShapeDtypeStruct((GLA_HEADS, HEAD_K, HEAD_V), jnp.float32),
            jax.ShapeDtypeStruct((DEC_BATCH, GLA_HEADS, HEAD_K, HEAD_V), jnp.float32),
        ],
        scratch_shapes=[pltpu.VMEM((HEAD_K, HEAD_V), jnp.float32)],
        compiler_params=pltpu.CompilerParams(
            dimension_semantics=("arbitrary", "arbitrary"), vmem_limit_bytes=VMEM_LIMIT),
        name="gla",
    )(u, u, u, u, g, gla_norm, state_gla)


MIX_TN = 256
N_MIX_GROUPS = 5


def _mix_kernel(x_ref, ya_ref, wcb_ref, wcc_ref, wcx_ref, wga_ref, wgb_ref, cw_ref, cache_ref,
                m_ref, ctail_ref, clast_ref, wbf, carry):
    i = pl.program_id(1)

    @pl.when(i == 0)
    def _cast():
        for grp, w_ref in enumerate((wcb_ref, wcc_ref, wcx_ref, wga_ref, wgb_ref)):
            wbf[grp] = w_ref[0].astype(jnp.bfloat16)

    x = x_ref[...]
    cb, cc, cx, ga, gb = [
        lax.dot_general(x, wbf[grp], _NT, preferred_element_type=jnp.float32)
        for grp in range(N_MIX_GROUPS)]
    c = cc * cx
    row = lax.broadcasted_iota(jnp.int32, (ROW_TILE, MIX_TN), 0)
    prev1 = pltpu.roll(c, 1, 0)
    prev2 = pltpu.roll(c, 2, 0)
    w0 = cw_ref[0:1, :]
    w1 = cw_ref[1:2, :]
    w2 = cw_ref[2:3, :]

    def finish(p1, p2):
        conv = w0 * p2 + w1 * p1 + w2 * c
        m = _sigmoid(ga) * ya_ref[...] + _sigmoid(gb) * (cb * conv)
        m_ref[...] = m.astype(jnp.bfloat16)

    @pl.when(i == 0)
    def _tail():
        def stream_rows(j):
            parts = [jnp.zeros((SAMPLE_ROW0, MIX_TN), jnp.float32)]
            for s in range(DEC_BATCH):
                parts.append(jnp.broadcast_to(cache_ref[2 * s + j:2 * s + j + 1, :], (DEC_SEQ, MIX_TN)))
            parts.append(jnp.zeros((ROW_TILE - SAMPLE_ROW0 - SAMPLE_ROWS, MIX_TN), jnp.float32))
            return jnp.concatenate(parts, axis=0)

        old = stream_rows(0)
        new = stream_rows(1)
        in_sample = (row >= SAMPLE_ROW0) & (row < SAMPLE_ROW0 + SAMPLE_ROWS)
        p16 = (row - SAMPLE_ROW0) & (DEC_SEQ - 1)
        first = in_sample & (p16 == 0)
        second = in_sample & (p16 == 1)
        p1 = jnp.where(row == META_ROW0, 0.0, jnp.where(first, new, prev1))
        p2 = jnp.where((row == META_ROW0) | (row == META_ROW0 + 1), 0.0,
                       jnp.where(first, old, jnp.where(second, new, prev2)))
        finish(p1, p2)
        ctail_ref[...] = c
        carry[...] = c[CHUNK - 8:CHUNK]

    @pl.when(i > 0)
    def _prompt():
        hist = carry[...]
        p1 = jnp.where(row == 0, jnp.broadcast_to(hist[7:8], c.shape), prev1)
        p2 = jnp.where(row == 0, jnp.broadcast_to(hist[6:7], c.shape),
                       jnp.where(row == 1, jnp.broadcast_to(hist[7:8], c.shape), prev2))
        finish(p1, p2)
        carry[...] = c[ROW_TILE - 8:]

    @pl.when(i == N_TILES - 1)
    def _last():
        clast_ref[...] = c[ROW_TILE - 8:]


def _mix(xb, ya, w_in_t, conv_w, cache_rows):
    rb = _tile_first_tail
    tile = lambda cj, i: (rb(i), cj)
    top = lambda cj, i: (0, cj)
    w_spec = lambda grp: pl.BlockSpec(
        (pl.Element(1), pl.Element(MIX_TN), pl.Element(D_MODEL)),
        lambda cj, i: (0, pl.multiple_of(GATE_COL0 + GATE_RANK + grp * D_MODEL + cj * MIX_TN, GATE_RANK), 0))
    return pl.pallas_call(
        _mix_kernel,
        grid=(D_MODEL // MIX_TN, N_TILES),
        in_specs=[
            pl.BlockSpec((ROW_TILE, D_MODEL), lambda cj, i: (rb(i), 0)),
            pl.BlockSpec((ROW_TILE, MIX_TN), tile),
            *[w_spec(grp) for grp in range(N_MIX_GROUPS)],
            pl.BlockSpec((3, MIX_TN), top),
            pl.BlockSpec((2 * DEC_BATCH, MIX_TN), top),
        ],
        out_specs=[
            pl.BlockSpec((ROW_TILE, MIX_TN), tile),
            pl.BlockSpec((ROW_TILE, MIX_TN), top),
            pl.BlockSpec((8, MIX_TN), top),
        ],
        out_shape=[
            jax.ShapeDtypeStruct((ROWS, D_MODEL), jnp.bfloat16),
            jax.ShapeDtypeStruct((ROW_TILE, D_MODEL), jnp.float32),
            jax.ShapeDtypeStruct((8, D_MODEL), jnp.float32),
        ],
        scratch_shapes=[pltpu.VMEM((N_MIX_GROUPS, MIX_TN, D_MODEL), jnp.bfloat16),
                        pltpu.VMEM((8, MIX_TN), jnp.float32)],
        compiler_params=pltpu.CompilerParams(
            dimension_semantics=("arbitrary", "arbitrary"), vmem_limit_bytes=VMEM_LIMIT),
        name="proj_conv_mix",
    )(xb, ya, *([w_in_t] * N_MIX_GROUPS), conv_w, cache_rows)


SUB_ROWS = 128


def _out_proj_kernel(m_ref, xn_ref, w_ref, lg_ref, lb_ref, h_ref, hb_ref):
    for r in range(0, ROW_TILE, SUB_ROWS):
        rows = slice(r, r + SUB_ROWS)
        mix = jnp.dot(m_ref[rows, :], w_ref[...], preferred_element_type=jnp.float32)
        h = _layer_norm(ALPHA * xn_ref[rows, :] + mix, lg_ref[...], lb_ref[...])
        h_ref[rows, :] = h
        hb_ref[rows, :] = h.astype(jnp.bfloat16)


def _out_proj(m, xn, w_out, ln_g, ln_b):
    row = lambda i: (i, 0)
    full = lambda i: (0, 0)
    return pl.pallas_call(
        _out_proj_kernel,
        grid=(N_TILES,),
        in_specs=[
            pl.BlockSpec((ROW_TILE, D_MODEL), row),
            pl.BlockSpec((ROW_TILE, D_MODEL), row),
            pl.BlockSpec((D_MODEL, D_MODEL), full),
            pl.BlockSpec((1, D_MODEL), full),
            pl.BlockSpec((1, D_MODEL), full),
        ],
        out_specs=[pl.BlockSpec((ROW_TILE, D_MODEL), row), pl.BlockSpec((ROW_TILE, D_MODEL), row)],
        out_shape=[
            jax.ShapeDtypeStruct((ROWS, D_MODEL), jnp.float32),
            jax.ShapeDtypeStruct((ROWS, D_MODEL), jnp.bfloat16),
        ],
        compiler_params=pltpu.CompilerParams(
            dimension_semantics=("arbitrary",), vmem_limit_bytes=VMEM_LIMIT),
        name="out_proj_ln",
    )(m, xn, w_out, ln_g, ln_b)


FF_TILE = 512
N_FF_TILES = D_FF // FF_TILE


def _ffn_up_kernel(hb_ref, wg_ref, wu_ref, act_ref, wg_bf, wu_bf):
    @pl.when(pl.program_id(1) == 0)
    def _cast():
        for r in range(0, D_MODEL, W_SLAB):
            wg_bf[r:r + W_SLAB, :] = wg_ref[r:r + W_SLAB, :].astype(jnp.bfloat16)
            wu_bf[r:r + W_SLAB, :] = wu_ref[r:r + W_SLAB, :].astype(jnp.bfloat16)

    hb = hb_ref[...]
    gate = jnp.dot(hb, wg_bf[...], preferred_element_type=jnp.float32)
    up = jnp.dot(hb, wu_bf[...], preferred_element_type=jnp.float32)
    act_ref[...] = (gate * _sigmoid(gate) * up).astype(jnp.bfloat16)


def _ffn_up(hb, w_ffn_in):
    return pl.pallas_call(
        _ffn_up_kernel,
        grid=(N_FF_TILES, N_BIG_TILES),
        in_specs=[
            pl.BlockSpec((BIG_ROW_TILE, D_MODEL), lambda j, i: (i, 0)),
            pl.BlockSpec((None, D_MODEL, FF_TILE), lambda j, i: (0, 0, j)),
            pl.BlockSpec((None, D_MODEL, FF_TILE), lambda j, i: (0, 0, N_FF_TILES + j)),
        ],
        out_specs=pl.BlockSpec((BIG_ROW_TILE, FF_TILE), lambda j, i: (i, j)),
        out_shape=jax.ShapeDtypeStruct((ROWS, D_FF), jnp.bfloat16),
        scratch_shapes=[pltpu.VMEM((D_MODEL, FF_TILE), jnp.bfloat16),
                        pltpu.VMEM((D_MODEL, FF_TILE), jnp.bfloat16)],
        compiler_params=pltpu.CompilerParams(
            dimension_semantics=("arbitrary", "arbitrary"), vmem_limit_bytes=VMEM_LIMIT),
        name="ffn_up",
    )(hb, w_ffn_in, w_ffn_in)


DOWN_TILE = 256
N_DOWN_PROMPT = SEQ // DOWN_TILE
N_DOWN_TILES = ROWS // DOWN_TILE


def _ffn_down_kernel(act_ref, h_ref, wd_ref, lg_ref, lb_ref, yp_ref, yt_ref):
    i = pl.program_id(0)

    def rows_out(o_ref):
        for r in range(0, DOWN_TILE, SUB_ROWS):
            rows = slice(r, r + SUB_ROWS)
            f = jnp.dot(act_ref[rows, :], wd_ref[...], preferred_element_type=jnp.float32)
            o_ref[rows, :] = _layer_norm(ALPHA * h_ref[rows, :] + f, lg_ref[...], lb_ref[...])

    @pl.when(i < N_DOWN_PROMPT)
    def _prompt():
        rows_out(yp_ref)

    @pl.when(i >= N_DOWN_PROMPT)
    def _tail():
        rows_out(yt_ref)


def _ffn_down(act, h, w_down, ln_g, ln_b):
    row = lambda i: (i, 0)
    full = lambda i: (0, 0)
    return pl.pallas_call(
        _ffn_down_kernel,
        grid=(N_DOWN_TILES,),
        in_specs=[
            pl.BlockSpec((DOWN_TILE, D_FF), row),
            pl.BlockSpec((DOWN_TILE, D_MODEL), row),
            pl.BlockSpec((D_FF, D_MODEL), full, pipeline_mode=pl.Buffered(1)),
            pl.BlockSpec((1, D_MODEL), full),
            pl.BlockSpec((1, D_MODEL), full),
        ],
        out_specs=[
            pl.BlockSpec((None, DOWN_TILE, D_MODEL), lambda i: (0, jnp.minimum(i, N_DOWN_PROMPT - 1), 0)),
            pl.BlockSpec((DOWN_TILE, D_MODEL), lambda i: (jnp.maximum(i - N_DOWN_PROMPT, 0), 0)),
        ],
        out_shape=[
            jax.ShapeDtypeStruct((1, SEQ, D_MODEL), jnp.float32),
            jax.ShapeDtypeStruct((ROW_TILE, D_MODEL), jnp.float32),
        ],
        compiler_params=pltpu.CompilerParams(
            dimension_semantics=("arbitrary",), vmem_limit_bytes=VMEM_LIMIT),
        name="ffn_down_ln",
    )(act, h, w_down, ln_g, ln_b)


def kernel(x_prompt, x_sample, state_gla, cache_conv, meta_tokens, ln_in_g, ln_in_b, w_in, w_gate_up, b_gate, gla_norm, conv_w, w_out, ln1_g, ln1_b, w_ffn_in, w_ffn_out, ln2_g, ln2_b):
    f32, bf16 = jnp.float32, jnp.bfloat16
    tail = jnp.concatenate([
        jnp.zeros((META_ROW0, D_MODEL), f32),
        meta_tokens.astype(f32),
        x_sample.reshape(SAMPLE_ROWS, D_MODEL),
        jnp.zeros((ROW_TILE - SAMPLE_ROW0 - SAMPLE_ROWS, D_MODEL), f32),
    ], axis=0)
    w_gu = jnp.pad(w_gate_up[0], ((0, LANE - GATE_RANK), (0, 0))).astype(bf16)
    vec = lambda p: p.reshape(1, -1).astype(f32)

    w_in_t = jnp.swapaxes(w_in, 1, 2)
    xn, xb, g = _ln_gate(x_prompt, tail, vec(ln_in_g), vec(ln_in_b), w_in_t, w_gu, vec(b_gate[0]))
    u = _proj(xb, w_in_t)
    ya, s_prompt, s_sample = _gla(u, g, vec(gla_norm[0]), state_gla[0])
    m, c_tail, c_last = _mix(xb, ya, w_in_t, conv_w[0], cache_conv[0].reshape(2 * DEC_BATCH, D_MODEL))
    h, hb = _out_proj(m, xn, w_out[0].astype(bf16), vec(ln1_g[0]), vec(ln1_b[0]))
    act = _ffn_up(hb, w_ffn_in)
    y_prompt, y_tail = _ffn_down(act, h, w_ffn_out[0].astype(bf16), vec(ln2_g[0]), vec(ln2_b[0]))

    y_sample = y_tail[SAMPLE_ROW0:SAMPLE_ROW0 + SAMPLE_ROWS].reshape(DEC_BATCH, DEC_SEQ, D_MODEL)
    cache_prompt = c_last[6:8][None, None]
    cache_sample = c_tail[SAMPLE_ROW0:SAMPLE_ROW0 + SAMPLE_ROWS].reshape(
        DEC_BATCH, DEC_SEQ, D_MODEL)[:, DEC_SEQ - 2:][None]
    return (y_prompt, y_sample, s_prompt[None, None], cache_prompt, s_sample[None], cache_sample)
```

```python
import functools

import jax
import jax.numpy as jnp
from jax import lax
from jax.experimental import pallas as pl
from jax.experimental.pallas import tpu as pltpu

D_MODEL = 2048
SEQ = 8192
DEC_BATCH = 16
DEC_SEQ = 16
CHUNK = 64
N_META = 16
GLA_HEADS = 4
HEAD_K = 256
HEAD_V = 512
GLA_DK = GLA_HEADS * HEAD_K
GATE_RANK = 16
GATE_TEMP = 16.0
D_FF = 5632
ALPHA = 2.0 ** 0.25
LN_EPS = 1e-5
RMS_EPS = 1e-6

ROW_TILE = 512
N_PROMPT_TILES = SEQ // ROW_TILE
N_TILES = N_PROMPT_TILES + 1
ROWS = N_TILES * ROW_TILE
META_ROW0 = CHUNK - N_META
SAMPLE_ROW0 = CHUNK
SAMPLE_ROWS = DEC_BATCH * DEC_SEQ
LANE = 128
GATE_COL0 = 2 * GLA_DK + 2 * D_MODEL
VMEM_LIMIT = 56 * 1024 * 1024

_NT = (((1,), (1,)), ((), ()))
_TN = (((0,), (0,)), ((), ()))


def _tile_first_tail(i):
    return (i + N_PROMPT_TILES) % N_TILES


def _layer_norm(x, g, b):
    mu = jnp.mean(x, axis=-1, keepdims=True)
    xc = x - mu
    var = jnp.mean(xc * xc, axis=-1, keepdims=True)
    return xc * lax.rsqrt(var + LN_EPS) * g + b


def _log_sigmoid(x):
    return jnp.minimum(x, 0.0) - jnp.log1p(jnp.exp(-jnp.abs(x)))


def _sigmoid(x):
    return 1.0 / (1.0 + jnp.exp(-x))


def _ln_gate_kernel(xp_ref, xt_ref, lg_ref, lb_ref, wa_ref, wgu_ref, bg_ref, xn_ref, xb_ref, g_ref):
    x = jnp.where(pl.program_id(0) < N_PROMPT_TILES, xp_ref[...], xt_ref[...])
    xn = _layer_norm(x, lg_ref[...], lb_ref[...])
    xn_ref[...] = xn
    xb = xn.astype(jnp.bfloat16)
    xb_ref[...] = xb
    a = lax.dot_general(xb, wa_ref[...].astype(jnp.bfloat16), _NT, preferred_element_type=jnp.float32)
    z = jnp.dot(a.astype(jnp.bfloat16), wgu_ref[...], preferred_element_type=jnp.float32)
    g_ref[...] = _log_sigmoid(z + bg_ref[...]) * (1.0 / GATE_TEMP)


def _ln_gate(x_prompt, x_tail, ln_g, ln_b, w_in_t, w_gu, b_gate):
    row = lambda i: (i, 0)
    full = lambda i: (0, 0)
    return pl.pallas_call(
        _ln_gate_kernel,
        grid=(N_TILES,),
        in_specs=[
            pl.BlockSpec((None, ROW_TILE, D_MODEL), lambda i: (0, jnp.minimum(i, N_PROMPT_TILES - 1), 0)),
            pl.BlockSpec((ROW_TILE, D_MODEL), full),
            pl.BlockSpec((1, D_MODEL), full),
            pl.BlockSpec((1, D_MODEL), full),
            pl.BlockSpec((None, LANE, D_MODEL), lambda i: (0, GATE_COL0 // LANE, 0)),
            pl.BlockSpec((LANE, GLA_DK), full),
            pl.BlockSpec((1, GLA_DK), full),
        ],
        out_specs=[
            pl.BlockSpec((ROW_TILE, D_MODEL), row),
            pl.BlockSpec((ROW_TILE, D_MODEL), row),
            pl.BlockSpec((ROW_TILE, GLA_DK), row),
        ],
        out_shape=[
            jax.ShapeDtypeStruct((ROWS, D_MODEL), jnp.float32),
            jax.ShapeDtypeStruct((ROWS, D_MODEL), jnp.bfloat16),
            jax.ShapeDtypeStruct((ROWS, GLA_DK), jnp.float32),
        ],
        compiler_params=pltpu.CompilerParams(
            dimension_semantics=("arbitrary",), vmem_limit_bytes=VMEM_LIMIT),
        name="ln_gate",
    )(x_prompt, x_tail, ln_g, ln_b, w_in_t, w_gu, b_gate)


W_SLAB = 256
BIG_ROW_TILE = 1088
N_BIG_TILES = ROWS // BIG_ROW_TILE


_HALF_SIZES = (32, 16, 8, 4, 2, 1)
HEAD_ROWS = 2 * HEAD_K + 2 * HEAD_V


def _cast_head_weights(w_refs, wbf):
    off = 0
    for w_ref in w_refs:
        for r in range(0, w_ref.shape[0], W_SLAB):
            wbf[off + r:off + r + W_SLAB, :] = w_ref[r:r + W_SLAB, :].astype(jnp.bfloat16)
        off += w_ref.shape[0]


def _project_head(x, wbf):
    qk = lax.dot_general(x, wbf[0:2 * HEAD_K, :], _NT, preferred_element_type=jnp.float32)
    vr = lax.dot_general(x, wbf[2 * HEAD_K:, :], _NT, preferred_element_type=jnp.float32)
    return qk[:, :HEAD_K], qk[:, HEAD_K:], vr[:, :HEAD_V], vr[:, HEAD_V:]


def _head_weight_specs(**kw):
    kq = GLA_DK // HEAD_K
    return [
        pl.BlockSpec((None, HEAD_K, D_MODEL), lambda h, *_: (0, h, 0), **kw),
        pl.BlockSpec((None, HEAD_K, D_MODEL), lambda h, *_: (0, kq + h, 0), **kw),
        pl.BlockSpec((None, HEAD_V, D_MODEL), lambda h, *_: (0, kq + h, 0), **kw),
        pl.BlockSpec((None, HEAD_V, D_MODEL), lambda h, *_: (0, 2 * kq + h, 0), **kw),
    ]


def _row_of_block(b, block, r):
    n, w = b.shape
    b3 = b.reshape(n // block, block, w)
    return jnp.broadcast_to(b3[:, r:r + 1, :], b3.shape).reshape(n, w)


def _gla_prepare(q, k, g):
    n, w = q.shape
    row = lax.broadcasted_iota(jnp.int32, (n, w), 0)
    pos = row & (CHUNK - 1)
    b = g
    for sh in (1, 2, 4, 8, 16, 32):
        b = b + jnp.where(pos >= sh, pltpu.roll(b, sh, 0), 0.0)
    qs = q * (HEAD_K ** -0.5)
    b_last = _row_of_block(b, CHUNK, CHUNK - 1)
    q_inter = (qs * jnp.exp(b)).astype(jnp.bfloat16)
    k_dec = (k * jnp.exp(b_last - b)).astype(jnp.bfloat16)
    levels = []
    for s in _HALF_SIZES:
        upper = (row & (2 * s - 1)) >= s
        if s >= 4:
            b_mid = _row_of_block(b, 2 * s, s - 1)
            d = jnp.where(upper, b - b_mid, b_mid - b)
        elif s == 2:
            p4 = row & 3
            g_prev = pltpu.roll(g, 1, 0)
            g_next = pltpu.roll(g, n - 1, 0)
            d = jnp.where(p4 == 0, g_next, jnp.where(p4 == 1, 0.0, jnp.where(p4 == 2, g, g + g_prev)))
        else:
            d = jnp.where(upper, g, 0.0)
        x = (jnp.where(upper, qs, k) * jnp.exp(d)).astype(jnp.bfloat16)
        levels.append((x, x))
    levels.append((qs.astype(jnp.bfloat16), k.astype(jnp.bfloat16)))
    return q_inter, k_dec, b, levels


def _score_masks():
    ii = lax.broadcasted_iota(jnp.int32, (CHUNK, CHUNK), 0)
    jj = lax.broadcasted_iota(jnp.int32, (CHUNK, CHUNK), 1)
    masks = []
    for s in _HALF_SIZES:
        blk = 2 * s
        same = (ii & ~(blk - 1)) == (jj & ~(blk - 1))
        masks.append(same & ((ii & (blk - 1)) >= s) & ((jj & (blk - 1)) < s))
    masks.append(ii == jj)
    return masks


def _gla_chunk(c, q_inter, k_dec, b, levels, v_bf, masks, s_val):
    r0 = c * CHUNK
    sl = slice(r0, r0 + CHUNK)
    scores = jnp.zeros((CHUNK, CHUNK), jnp.float32)
    for (qe, ke), m in zip(levels, masks):
        p = lax.dot_general(qe[sl], ke[sl], _NT, preferred_element_type=jnp.float32)
        scores = jnp.where(m, p, scores)
    vc = v_bf[sl]
    o = jnp.dot(scores.astype(jnp.bfloat16), vc, preferred_element_type=jnp.float32)
    o = o + jnp.dot(q_inter[sl], s_val.astype(jnp.bfloat16), preferred_element_type=jnp.float32)
    eb = jnp.exp(b[r0 + CHUNK - 1:r0 + CHUNK, :])
    eb_t = jnp.transpose(jnp.broadcast_to(eb, (LANE, HEAD_K)))
    decay = jnp.concatenate([eb_t] * (HEAD_V // LANE), axis=1)
    ds = lax.dot_general(k_dec[sl], vc, _TN, preferred_element_type=jnp.float32)
    return o, decay * s_val + ds


def _gla_finish(o, r, gn):
    o = o * lax.rsqrt(jnp.mean(o * o, axis=-1, keepdims=True) + RMS_EPS)
    return o * gn * (r * _sigmoid(r))


def _gla_tail_kernel(x_ref, wq_ref, wk_ref, wv_ref, wr_ref, g_ref, gn_ref, sin_ref,
                     y_ref, sm_ref, ss_ref, wbf):
    _cast_head_weights((wq_ref, wk_ref, wv_ref, wr_ref), wbf)
    q, k, v, r = _project_head(x_ref[...], wbf)
    masks = _score_masks()

    def padded(x):
        parts = [x[0:CHUNK]]
        zeros = jnp.zeros((META_ROW0, x.shape[1]), x.dtype)
        for s in range(DEC_BATCH):
            parts += [zeros, x[SAMPLE_ROW0 + s * DEC_SEQ:SAMPLE_ROW0 + (s + 1) * DEC_SEQ]]
        return jnp.concatenate(parts, axis=0)

    n = (1 + DEC_BATCH) * CHUNK
    q = padded(q)
    k = padded(k)
    g = padded(g_ref[...])
    v = padded(v)
    live = (lax.broadcasted_iota(jnp.int32, (n, HEAD_K), 0) & (CHUNK - 1)) >= META_ROW0
    k = jnp.where(live, k, 0.0)
    g = jnp.where(live, g, 0.0)
    q_inter, k_dec, b, levels = _gla_prepare(q, k, g)
    v_bf = v.astype(jnp.bfloat16)
    outs = [jnp.zeros((META_ROW0, HEAD_V), jnp.float32)]
    o, s_meta = _gla_chunk(0, q_inter, k_dec, b, levels, v_bf, masks,
                           jnp.zeros((HEAD_K, HEAD_V), jnp.float32))
    sm_ref[...] = s_meta
    outs.append(o[META_ROW0:])
    for s in range(DEC_BATCH):
        o, s_new = _gla_chunk(1 + s, q_inter, k_dec, b, levels, v_bf, masks, sin_ref[s])
        ss_ref[s] = s_new
        outs.append(o[META_ROW0:])
    outs.append(jnp.zeros((ROW_TILE - SAMPLE_ROW0 - SAMPLE_ROWS, HEAD_V), jnp.float32))
    y_ref[...] = _gla_finish(jnp.concatenate(outs, axis=0), r, gn_ref[...])


def _gla_tail(xb, w_in_t, g, gla_norm, state_gla):
    once = dict(pipeline_mode=pl.Buffered(1))
    return pl.pallas_call(
        _gla_tail_kernel,
        grid=(GLA_HEADS,),
        in_specs=[
            pl.BlockSpec((ROW_TILE, D_MODEL), lambda h: (N_PROMPT_TILES, 0)),
            *_head_weight_specs(**once),
            pl.BlockSpec((ROW_TILE, HEAD_K), lambda h: (N_PROMPT_TILES, h)),
            pl.BlockSpec((1, HEAD_V), lambda h: (0, h)),
            pl.BlockSpec((DEC_BATCH, None, HEAD_K, HEAD_V), lambda h: (0, h, 0, 0), **once),
        ],
        out_specs=[
            pl.BlockSpec((ROW_TILE, HEAD_V), lambda h: (0, h)),
            pl.BlockSpec((None, HEAD_K, HEAD_V), lambda h: (h, 0, 0)),
            pl.BlockSpec((DEC_BATCH, None, HEAD_K, HEAD_V), lambda h: (0, h, 0, 0)),
        ],
        out_shape=[
            jax.ShapeDtypeStruct((ROW_TILE, D_MODEL), jnp.float32),
            jax.ShapeDtypeStruct((GLA_HEADS, HEAD_K, HEAD_V), jnp.float32),
            jax.ShapeDtypeStruct((DEC_BATCH, GLA_HEADS, HEAD_K, HEAD_V), jnp.float32),
        ],
        scratch_shapes=[pltpu.VMEM((HEAD_ROWS, D_MODEL), jnp.bfloat16)],
        compiler_params=pltpu.CompilerParams(
            dimension_semantics=("arbitrary",), vmem_limit_bytes=VMEM_LIMIT),
        name="gla_tail",
    )(xb, *([w_in_t] * 4), g, gla_norm, state_gla)


def _gla_prompt_kernel(x_ref, wq_ref, wk_ref, wv_ref, wr_ref, g_ref, gn_ref, s0_ref,
                       y_ref, sp_ref, wbf, s_scr):
    i = pl.program_id(1)

    @pl.when(i == 0)
    def _start_head():
        _cast_head_weights((wq_ref, wk_ref, wv_ref, wr_ref), wbf)
        s_scr[...] = s0_ref[...]

    q, k, v, r = _project_head(x_ref[...], wbf)
    masks = _score_masks()
    q_inter, k_dec, b, levels = _gla_prepare(q, k, g_ref[...])
    v_bf = v.astype(jnp.bfloat16)
    outs = []
    for c in range(ROW_TILE // CHUNK):
        o, s_new = _gla_chunk(c, q_inter, k_dec, b, levels, v_bf, masks, s_scr[...])
        s_scr[...] = s_new
        outs.append(o)
    y_ref[...] = _gla_finish(jnp.concatenate(outs, axis=0), r, gn_ref[...])

    @pl.when(i == N_PROMPT_TILES - 1)
    def _final_state():
        sp_ref[...] = s_scr[...]


def _gla_prompt(xb, w_in_t, g, gla_norm, s_meta):
    return pl.pallas_call(
        _gla_prompt_kernel,
        grid=(GLA_HEADS, N_PROMPT_TILES),
        in_specs=[
            pl.BlockSpec((ROW_TILE, D_MODEL), lambda h, i: (i, 0)),
            *_head_weight_specs(),
            pl.BlockSpec((ROW_TILE, HEAD_K), lambda h, i: (i, h)),
            pl.BlockSpec((1, HEAD_V), lambda h, i: (0, h)),
            pl.BlockSpec((None, HEAD_K, HEAD_V), lambda h, i: (h, 0, 0)),
        ],
        out_specs=[
            pl.BlockSpec((ROW_TILE, HEAD_V), lambda h, i: (i, h)),
            pl.BlockSpec((None, HEAD_K, HEAD_V), lambda h, i: (h, 0, 0)),
        ],
        out_shape=[
            jax.ShapeDtypeStruct((SEQ, D_MODEL), jnp.float32),
            jax.ShapeDtypeStruct((GLA_HEADS, HEAD_K, HEAD_V), jnp.float32),
        ],
        scratch_shapes=[pltpu.VMEM((HEAD_ROWS, D_MODEL), jnp.bfloat16),
                        pltpu.VMEM((HEAD_K, HEAD_V), jnp.float32)],
        compiler_params=pltpu.CompilerParams(
            dimension_semantics=("arbitrary", "arbitrary"), vmem_limit_bytes=VMEM_LIMIT),
        name="gla_prompt",
    )(xb, *([w_in_t] * 4), g, gla_norm, s_meta)


MIX_TN = 256
N_MIX_GROUPS = 5


def _mix_kernel(x_ref, yap_ref, yat_ref, wcb_ref, wcc_ref, wcx_ref, wga_ref, wgb_ref, cw_ref, cache_ref,
                m_ref, ctail_ref, clast_ref, wbf, carry):
    i = pl.program_id(1)

    @pl.when(i == 0)
    def _cast():
        for grp, w_ref in enumerate((wcb_ref, wcc_ref, wcx_ref, wga_ref, wgb_ref)):
            wbf[grp] = w_ref[0].astype(jnp.bfloat16)

    x = x_ref[...]
    cb, cc, cx, ga, gb = [
        lax.dot_general(x, wbf[grp], _NT, preferred_element_type=jnp.float32)
        for grp in range(N_MIX_GROUPS)]
    c = cc * cx
    row = lax.broadcasted_iota(jnp.int32, (ROW_TILE, MIX_TN), 0)
    prev1 = pltpu.roll(c, 1, 0)
    prev2 = pltpu.roll(c, 2, 0)
    w0 = cw_ref[0:1, :]
    w1 = cw_ref[1:2, :]
    w2 = cw_ref[2:3, :]

    def finish(p1, p2, ya_ref):
        conv = w0 * p2 + w1 * p1 + w2 * c
        m = _sigmoid(ga) * ya_ref[...] + _sigmoid(gb) * (cb * conv)
        m_ref[...] = m.astype(jnp.bfloat16)

    @pl.when(i == 0)
    def _tail():
        def stream_rows(j):
            parts = [jnp.zeros((SAMPLE_ROW0, MIX_TN), jnp.float32)]
            for s in range(DEC_BATCH):
                parts.append(jnp.broadcast_to(cache_ref[2 * s + j:2 * s + j + 1, :], (DEC_SEQ, MIX_TN)))
            parts.append(jnp.zeros((ROW_TILE - SAMPLE_ROW0 - SAMPLE_ROWS, MIX_TN), jnp.float32))
            return jnp.concatenate(parts, axis=0)

        old = stream_rows(0)
        new = stream_rows(1)
        in_sample = (row >= SAMPLE_ROW0) & (row < SAMPLE_ROW0 + SAMPLE_ROWS)
        p16 = (row - SAMPLE_ROW0) & (DEC_SEQ - 1)
        first = in_sample & (p16 == 0)
        second = in_sample & (p16 == 1)
        p1 = jnp.where(row == META_ROW0, 0.0, jnp.where(first, new, prev1))
        p2 = jnp.where((row == META_ROW0) | (row == META_ROW0 + 1), 0.0,
                       jnp.where(first, old, jnp.where(second, new, prev2)))
        finish(p1, p2, yat_ref)
        ctail_ref[...] = c
        carry[...] = c[CHUNK - 8:CHUNK]

    @pl.when(i > 0)
    def _prompt():
        hist = carry[...]
        p1 = jnp.where(row == 0, jnp.broadcast_to(hist[7:8], c.shape), prev1)
        p2 = jnp.where(row == 0, jnp.broadcast_to(hist[6:7], c.shape),
                       jnp.where(row == 1, jnp.broadcast_to(hist[7:8], c.shape), prev2))
        finish(p1, p2, yap_ref)
        carry[...] = c[ROW_TILE - 8:]

    @pl.when(i == N_TILES - 1)
    def _last():
        clast_ref[...] = c[ROW_TILE - 8:]


def _mix(xb, ya_prompt, ya_tail, w_in_t, conv_w, cache_rows):
    rb = _tile_first_tail
    tile = lambda cj, i: (rb(i), cj)
    top = lambda cj, i: (0, cj)
    w_spec = lambda grp: pl.BlockSpec(
        (pl.Element(1), pl.Element(MIX_TN), pl.Element(D_MODEL)),
        lambda cj, i: (0, pl.multiple_of(GATE_COL0 + GATE_RANK + grp * D_MODEL + cj * MIX_TN, GATE_RANK), 0))
    return pl.pallas_call(
        _mix_kernel,
        grid=(D_MODEL // MIX_TN, N_TILES),
        in_specs=[
            pl.BlockSpec((ROW_TILE, D_MODEL), lambda cj, i: (rb(i), 0)),
            pl.BlockSpec((ROW_TILE, MIX_TN), lambda cj, i: (jnp.maximum(i - 1, 0), cj)),
            pl.BlockSpec((ROW_TILE, MIX_TN), top),
            *[w_spec(grp) for grp in range(N_MIX_GROUPS)],
            pl.BlockSpec((3, MIX_TN), top),
            pl.BlockSpec((2 * DEC_BATCH, MIX_TN), top),
        ],
        out_specs=[
            pl.BlockSpec((ROW_TILE, MIX_TN), tile),
            pl.BlockSpec((ROW_TILE, MIX_TN), top),
            pl.BlockSpec((8, MIX_TN), top),
        ],
        out_shape=[
            jax.ShapeDtypeStruct((ROWS, D_MODEL), jnp.bfloat16),
            jax.ShapeDtypeStruct((ROW_TILE, D_MODEL), jnp.float32),
            jax.ShapeDtypeStruct((8, D_MODEL), jnp.float32),
        ],
        scratch_shapes=[pltpu.VMEM((N_MIX_GROUPS, MIX_TN, D_MODEL), jnp.bfloat16),
                        pltpu.VMEM((8, MIX_TN), jnp.float32)],
        compiler_params=pltpu.CompilerParams(
            dimension_semantics=("arbitrary", "arbitrary"), vmem_limit_bytes=VMEM_LIMIT),
        name="proj_conv_mix",
    )(xb, ya_prompt, ya_tail, *([w_in_t] * N_MIX_GROUPS), conv_w, cache_rows)


SUB_ROWS = 128


def _out_proj_kernel(m_ref, xn_ref, w_ref, lg_ref, lb_ref, h_ref, hb_ref):
    for r in range(0, ROW_TILE, SUB_ROWS):
        rows = slice(r, r + SUB_ROWS)
        mix = jnp.dot(m_ref[rows, :], w_ref[...], preferred_element_type=jnp.float32)
        h = _layer_norm(ALPHA * xn_ref[rows, :] + mix, lg_ref[...], lb_ref[...])
        h_ref[rows, :] = h
        hb_ref[rows, :] = h.astype(jnp.bfloat16)


def _out_proj(m, xn, w_out, ln_g, ln_b):
    row = lambda i: (i, 0)
    full = lambda i: (0, 0)
    return pl.pallas_call(
        _out_proj_kernel,
        grid=(N_TILES,),
        in_specs=[
            pl.BlockSpec((ROW_TILE, D_MODEL), row),
            pl.BlockSpec((ROW_TILE, D_MODEL), row),
            pl.BlockSpec((D_MODEL, D_MODEL), full),
            pl.BlockSpec((1, D_MODEL), full),
            pl.BlockSpec((1, D_MODEL), full),
        ],
        out_specs=[pl.BlockSpec((ROW_TILE, D_MODEL), row), pl.BlockSpec((ROW_TILE, D_MODEL), row)],
        out_shape=[
            jax.ShapeDtypeStruct((ROWS, D_MODEL), jnp.float32),
            jax.ShapeDtypeStruct((ROWS, D_MODEL), jnp.bfloat16),
        ],
        compiler_params=pltpu.CompilerParams(
            dimension_semantics=("arbitrary",), vmem_limit_bytes=VMEM_LIMIT),
        name="out_proj_ln",
    )(m, xn, w_out, ln_g, ln_b)


FF_TILE = 512
N_FF_TILES = D_FF // FF_TILE
FFN_UP_ROWS = BIG_ROW_TILE


def _ffn_up_kernel(hb_ref, wg_ref, wu_ref, act_ref, wg_bf, wu_bf):
    @pl.when(pl.program_id(1) == 0)
    def _cast():
        for r in range(0, D_MODEL, W_SLAB):
            wg_bf[r:r + W_SLAB, :] = wg_ref[r:r + W_SLAB, :].astype(jnp.bfloat16)
            wu_bf[r:r + W_SLAB, :] = wu_ref[r:r + W_SLAB, :].astype(jnp.bfloat16)

    hb = hb_ref[...]
    gate = jnp.dot(hb, wg_bf[...], preferred_element_type=jnp.float32)
    up = jnp.dot(hb, wu_bf[...], preferred_element_type=jnp.float32)
    act_ref[...] = (gate * _sigmoid(gate) * up).astype(jnp.bfloat16)


def _ffn_up(hb, w_ffn_in):
    return pl.pallas_call(
        _ffn_up_kernel,
        grid=(N_FF_TILES, ROWS // FFN_UP_ROWS),
        in_specs=[
            pl.BlockSpec((FFN_UP_ROWS, D_MODEL), lambda j, i: (i, 0)),
            pl.BlockSpec((None, D_MODEL, FF_TILE), lambda j, i: (0, 0, j)),
            pl.BlockSpec((None, D_MODEL, FF_TILE), lambda j, i: (0, 0, N_FF_TILES + j)),
        ],
        out_specs=pl.BlockSpec((FFN_UP_ROWS, FF_TILE), lambda j, i: (i, j)),
        out_shape=jax.ShapeDtypeStruct((ROWS, D_FF), jnp.bfloat16),
        scratch_shapes=[pltpu.VMEM((D_MODEL, FF_TILE), jnp.bfloat16),
                        pltpu.VMEM((D_MODEL, FF_TILE), jnp.bfloat16)],
        compiler_params=pltpu.CompilerParams(
            dimension_semantics=("arbitrary", "arbitrary"), vmem_limit_bytes=VMEM_LIMIT),
        name="ffn_up",
    )(hb, w_ffn_in, w_ffn_in)


DOWN_TILE = 256
N_DOWN_PROMPT = SEQ // DOWN_TILE
N_DOWN_TILES = ROWS // DOWN_TILE


def _ffn_down_kernel(act_ref, h_ref, wd_ref, lg_ref, lb_ref, yp_ref, yt_ref):
    i = pl.program_id(0)

    def rows_out(o_ref):
        for r in range(0, DOWN_TILE, SUB_ROWS):
            rows = slice(r, r + SUB_ROWS)
            f = jnp.dot(act_ref[rows, :], wd_ref[...], preferred_element_type=jnp.float32)
            o_ref[rows, :] = _layer_norm(ALPHA * h_ref[rows, :] + f, lg_ref[...], lb_ref[...])

    @pl.when(i < N_DOWN_PROMPT)
    def _prompt():
        rows_out(yp_ref)

    @pl.when(i >= N_DOWN_PROMPT)
    def _tail():
        rows_out(yt_ref)


def _ffn_down(act, h, w_down, ln_g, ln_b):
    row = lambda i: (i, 0)
    full = lambda i: (0, 0)
    return pl.pallas_call(
        _ffn_down_kernel,
        grid=(N_DOWN_TILES,),
        in_specs=[
            pl.BlockSpec((DOWN_TILE, D_FF), row),
            pl.BlockSpec((DOWN_TILE, D_MODEL), row),
            pl.BlockSpec((D_FF, D_MODEL), full, pipeline_mode=pl.Buffered(1)),
            pl.BlockSpec((1, D_MODEL), full),
            pl.BlockSpec((1, D_MODEL), full),
        ],
        out_specs=[
            pl.BlockSpec((None, DOWN_TILE, D_MODEL), lambda i: (0, jnp.minimum(i, N_DOWN_PROMPT - 1), 0)),
            pl.BlockSpec((DOWN_TILE, D_MODEL), lambda i: (jnp.maximum(i - N_DOWN_PROMPT, 0), 0)),
        ],
        out_shape=[
            jax.ShapeDtypeStruct((1, SEQ, D_MODEL), jnp.float32),
            jax.ShapeDtypeStruct((ROW_TILE, D_MODEL), jnp.float32),
        ],
        compiler_params=pltpu.CompilerParams(
            dimension_semantics=("arbitrary",), vmem_limit_bytes=VMEM_LIMIT),
        name="ffn_down_ln",
    )(act, h, w_down, ln_g, ln_b)


def kernel(x_prompt, x_sample, state_gla, cache_conv, meta_tokens, ln_in_g, ln_in_b, w_in, w_gate_up, b_gate, gla_norm, conv_w, w_out, ln1_g, ln1_b, w_ffn_in, w_ffn_out, ln2_g, ln2_b):
    f32, bf16 = jnp.float32, jnp.bfloat16
    tail = jnp.concatenate([
        jnp.zeros((META_ROW0, D_MODEL), f32),
        meta_tokens.astype(f32),
        x_sample.reshape(SAMPLE_ROWS, D_MODEL),
        jnp.zeros((ROW_TILE - SAMPLE_ROW0 - SAMPLE_ROWS, D_MODEL), f32),
    ], axis=0)
    w_gu = jnp.pad(w_gate_up[0], ((0, LANE - GATE_RANK), (0, 0))).astype(bf16)
    vec = lambda p: p.reshape(1, -1).astype(f32)

    w_in_t = jnp.swapaxes(w_in, 1, 2)
    xn, xb, g = _ln_gate(x_prompt, tail, vec(ln_in_g), vec(ln_in_b), w_in_t, w_gu, vec(b_gate[0]))
    ya_tail, s_meta, s_sample = _gla_tail(xb, w_in_t, g, vec(gla_norm[0]), state_gla[0])
    ya_prompt, s_prompt = _gla_prompt(xb, w_in_t, g, vec(gla_norm[0]), s_meta)
    m, c_tail, c_last = _mix(xb, ya_prompt, ya_tail, w_in_t, conv_w[0],
                             cache_conv[0].reshape(2 * DEC_BATCH, D_MODEL))
    h, hb = _out_proj(m, xn, w_out[0].astype(bf16), vec(ln1_g[0]), vec(ln1_b[0]))
    act = _ffn_up(hb, w_ffn_in)
    y_prompt, y_tail = _ffn_down(act, h, w_ffn_out[0].astype(bf16), vec(ln2_g[0]), vec(ln2_b[0]))

    y_sample = y_tail[SAMPLE_ROW0:SAMPLE_ROW0 + SAMPLE_ROWS].reshape(DEC_BATCH, DEC_SEQ, D_MODEL)
    cache_prompt = c_last[6:8][None, None]
    cache_sample = c_tail[SAMPLE_ROW0:SAMPLE_ROW0 + SAMPLE_ROWS].reshape(
        DEC_BATCH, DEC_SEQ, D_MODEL)[:, DEC_SEQ - 2:][None]
    return (y_prompt, y_sample, s_prompt[None, None], cache_prompt, s_sample[None], cache_sample)
```

```python
import functools

import jax
import jax.numpy as jnp
from jax import lax
from jax.experimental import pallas as pl
from jax.experimental.pallas import tpu as pltpu

D_MODEL = 2048
SEQ = 8192
DEC_BATCH = 16
DEC_SEQ = 16
CHUNK = 64
N_META = 16
GLA_HEADS = 4
HEAD_K = 256
HEAD_V = 512
GLA_DK = GLA_HEADS * HEAD_K
GATE_RANK = 16
GATE_TEMP = 16.0
D_FF = 5632
ALPHA = 2.0 ** 0.25
LN_EPS = 1e-5
RMS_EPS = 1e-6

ROW_TILE = 512
N_PROMPT_TILES = SEQ // ROW_TILE
N_TILES = N_PROMPT_TILES + 1
ROWS = N_TILES * ROW_TILE
META_ROW0 = CHUNK - N_META
SAMPLE_ROW0 = CHUNK
SAMPLE_ROWS = DEC_BATCH * DEC_SEQ
LANE = 128
GATE_COL0 = 2 * GLA_DK + 2 * D_MODEL
VMEM_LIMIT = 56 * 1024 * 1024

_NT = (((1,), (1,)), ((), ()))
_TN = (((0,), (0,)), ((), ()))


def _tile_first_tail(i):
    return (i + N_PROMPT_TILES) % N_TILES


def _layer_norm(x, g, b):
    mu = jnp.mean(x, axis=-1, keepdims=True)
    xc = x - mu
    var = jnp.mean(xc * xc, axis=-1, keepdims=True)
    return xc * lax.rsqrt(var + LN_EPS) * g + b


def _log_sigmoid(x):
    return jnp.minimum(x, 0.0) - jnp.log1p(jnp.exp(-jnp.abs(x)))


def _sigmoid(x):
    return 1.0 / (1.0 + jnp.exp(-x))


def _ln_gate_kernel(xp_ref, xt_ref, lg_ref, lb_ref, wa_ref, wgu_ref, bg_ref, xn_ref, xb_ref, g_ref):
    x = jnp.where(pl.program_id(0) < N_PROMPT_TILES, xp_ref[...], xt_ref[...])
    xn = _layer_norm(x, lg_ref[...], lb_ref[...])
    xn_ref[...] = xn
    xb = xn.astype(jnp.bfloat16)
    xb_ref[...] = xb
    a = lax.dot_general(xb, wa_ref[...].astype(jnp.bfloat16), _NT, preferred_element_type=jnp.float32)
    z = jnp.dot(a.astype(jnp.bfloat16), wgu_ref[...], preferred_element_type=jnp.float32)
    g_ref[...] = _log_sigmoid(z + bg_ref[...]) * (1.0 / GATE_TEMP)


def _ln_gate(x_prompt, x_tail, ln_g, ln_b, w_in_t, w_gu, b_gate):
    row = lambda i: (i, 0)
    full = lambda i: (0, 0)
    return pl.pallas_call(
        _ln_gate_kernel,
        grid=(N_TILES,),
        in_specs=[
            pl.BlockSpec((None, ROW_TILE, D_MODEL), lambda i: (0, jnp.minimum(i, N_PROMPT_TILES - 1), 0)),
            pl.BlockSpec((ROW_TILE, D_MODEL), full),
            pl.BlockSpec((1, D_MODEL), full),
            pl.BlockSpec((1, D_MODEL), full),
            pl.BlockSpec((None, LANE, D_MODEL), lambda i: (0, GATE_COL0 // LANE, 0)),
            pl.BlockSpec((LANE, GLA_DK), full),
            pl.BlockSpec((1, GLA_DK), full),
        ],
        out_specs=[
            pl.BlockSpec((ROW_TILE, D_MODEL), row),
            pl.BlockSpec((ROW_TILE, D_MODEL), row),
            pl.BlockSpec((ROW_TILE, GLA_DK), row),
        ],
        out_shape=[
            jax.ShapeDtypeStruct((ROWS, D_MODEL), jnp.float32),
            jax.ShapeDtypeStruct((ROWS, D_MODEL), jnp.bfloat16),
            jax.ShapeDtypeStruct((ROWS, GLA_DK), jnp.float32),
        ],
        compiler_params=pltpu.CompilerParams(
            dimension_semantics=("arbitrary",), vmem_limit_bytes=VMEM_LIMIT),
        name="ln_gate",
    )(x_prompt, x_tail, ln_g, ln_b, w_in_t, w_gu, b_gate)


W_SLAB = 256
BIG_ROW_TILE = 1088
N_BIG_TILES = ROWS // BIG_ROW_TILE


PROMPT_CHUNK = 256
HEAD_ROWS = 2 * HEAD_K + 2 * HEAD_V


def _cast_head_weights(w_refs, wbf):
    off = 0
    for w_ref in w_refs:
        for r in range(0, w_ref.shape[0], W_SLAB):
            wbf[off + r:off + r + W_SLAB, :] = w_ref[r:r + W_SLAB, :].astype(jnp.bfloat16)
        off += w_ref.shape[0]


def _project_head(x, wbf):
    qk = lax.dot_general(x, wbf[0:2 * HEAD_K, :], _NT, preferred_element_type=jnp.float32)
    vr = lax.dot_general(x, wbf[2 * HEAD_K:, :], _NT, preferred_element_type=jnp.float32)
    return qk[:, :HEAD_K], qk[:, HEAD_K:], vr[:, :HEAD_V], vr[:, HEAD_V:]


def _head_weight_specs(**kw):
    kq = GLA_DK // HEAD_K
    return [
        pl.BlockSpec((None, HEAD_K, D_MODEL), lambda h, *_: (0, h, 0), **kw),
        pl.BlockSpec((None, HEAD_K, D_MODEL), lambda h, *_: (0, kq + h, 0), **kw),
        pl.BlockSpec((None, HEAD_V, D_MODEL), lambda h, *_: (0, kq + h, 0), **kw),
        pl.BlockSpec((None, HEAD_V, D_MODEL), lambda h, *_: (0, 2 * kq + h, 0), **kw),
    ]


def _row_of_block(b, block, r):
    n, w = b.shape
    b3 = b.reshape(n // block, block, w)
    return jnp.broadcast_to(b3[:, r:r + 1, :], b3.shape).reshape(n, w)


def _half_sizes(chunk):
    return tuple(chunk >> (j + 1) for j in range(chunk.bit_length() - 1))


def _gla_prepare(q, k, g, chunk):
    n, w = q.shape
    row = lax.broadcasted_iota(jnp.int32, (n, w), 0)
    pos = row & (chunk - 1)
    b = g
    for sh in reversed(_half_sizes(chunk)):
        b = b + jnp.where(pos >= sh, pltpu.roll(b, sh, 0), 0.0)
    qs = q * (HEAD_K ** -0.5)
    b_last = _row_of_block(b, chunk, chunk - 1)
    q_inter = (qs * jnp.exp(b)).astype(jnp.bfloat16)
    k_dec = (k * jnp.exp(b_last - b)).astype(jnp.bfloat16)
    levels = []
    for s in _half_sizes(chunk):
        upper = (row & (2 * s - 1)) >= s
        if s >= 4:
            b_mid = _row_of_block(b, 2 * s, s - 1)
            d = jnp.where(upper, b - b_mid, b_mid - b)
        elif s == 2:
            p4 = row & 3
            g_prev = pltpu.roll(g, 1, 0)
            g_next = pltpu.roll(g, n - 1, 0)
            d = jnp.where(p4 == 0, g_next, jnp.where(p4 == 1, 0.0, jnp.where(p4 == 2, g, g + g_prev)))
        else:
            d = jnp.where(upper, g, 0.0)
        x = (jnp.where(upper, qs, k) * jnp.exp(d)).astype(jnp.bfloat16)
        levels.append((x, x))
    levels.append((qs.astype(jnp.bfloat16), k.astype(jnp.bfloat16)))
    return q_inter, k_dec, b, levels


def _score_masks(chunk):
    ii = lax.broadcasted_iota(jnp.int32, (chunk, chunk), 0)
    jj = lax.broadcasted_iota(jnp.int32, (chunk, chunk), 1)
    masks = []
    for s in _half_sizes(chunk):
        blk = 2 * s
        same = (ii & ~(blk - 1)) == (jj & ~(blk - 1))
        masks.append(same & ((ii & (blk - 1)) >= s) & ((jj & (blk - 1)) < s))
    masks.append(ii == jj)
    return masks


def _gla_chunk(c, chunk, q_inter, k_dec, b, levels, v_bf, masks, s_val):
    r0 = c * chunk
    sl = slice(r0, r0 + chunk)
    scores = jnp.zeros((chunk, chunk), jnp.float32)
    for (qe, ke), m in zip(levels, masks):
        p = lax.dot_general(qe[sl], ke[sl], _NT, preferred_element_type=jnp.float32)
        scores = jnp.where(m, p, scores)
    vc = v_bf[sl]
    o = jnp.dot(scores.astype(jnp.bfloat16), vc, preferred_element_type=jnp.float32)
    o = o + jnp.dot(q_inter[sl], s_val.astype(jnp.bfloat16), preferred_element_type=jnp.float32)
    eb = jnp.exp(b[r0 + chunk - 1:r0 + chunk, :])
    eb_t = jnp.transpose(jnp.broadcast_to(eb, (LANE, HEAD_K)))
    decay = jnp.concatenate([eb_t] * (HEAD_V // LANE), axis=1)
    ds = lax.dot_general(k_dec[sl], vc, _TN, preferred_element_type=jnp.float32)
    return o, decay * s_val + ds


def _gla_finish(o, r, gn):
    o = o * lax.rsqrt(jnp.mean(o * o, axis=-1, keepdims=True) + RMS_EPS)
    return o * gn * (r * _sigmoid(r))


def _gla_tail_kernel(x_ref, wq_ref, wk_ref, wv_ref, wr_ref, g_ref, gn_ref, sin_ref,
                     y_ref, sm_ref, ss_ref, wbf):
    _cast_head_weights((wq_ref, wk_ref, wv_ref, wr_ref), wbf)
    q, k, v, r = _project_head(x_ref[...], wbf)
    masks = _score_masks(CHUNK)

    def padded(x):
        parts = [x[0:CHUNK]]
        zeros = jnp.zeros((META_ROW0, x.shape[1]), x.dtype)
        for s in range(DEC_BATCH):
            parts += [zeros, x[SAMPLE_ROW0 + s * DEC_SEQ:SAMPLE_ROW0 + (s + 1) * DEC_SEQ]]
        return jnp.concatenate(parts, axis=0)

    n = (1 + DEC_BATCH) * CHUNK
    q = padded(q)
    k = padded(k)
    g = padded(g_ref[...])
    v = padded(v)
    live = (lax.broadcasted_iota(jnp.int32, (n, HEAD_K), 0) & (CHUNK - 1)) >= META_ROW0
    k = jnp.where(live, k, 0.0)
    g = jnp.where(live, g, 0.0)
    q_inter, k_dec, b, levels = _gla_prepare(q, k, g, CHUNK)
    v_bf = v.astype(jnp.bfloat16)
    outs = [jnp.zeros((META_ROW0, HEAD_V), jnp.float32)]
    o, s_meta = _gla_chunk(0, CHUNK, q_inter, k_dec, b, levels, v_bf, masks,
                           jnp.zeros((HEAD_K, HEAD_V), jnp.float32))
    sm_ref[...] = s_meta
    outs.append(o[META_ROW0:])
    for s in range(DEC_BATCH):
        o, s_new = _gla_chunk(1 + s, CHUNK, q_inter, k_dec, b, levels, v_bf, masks, sin_ref[s])
        ss_ref[s] = s_new
        outs.append(o[META_ROW0:])
    outs.append(jnp.zeros((ROW_TILE - SAMPLE_ROW0 - SAMPLE_ROWS, HEAD_V), jnp.float32))
    y_ref[...] = _gla_finish(jnp.concatenate(outs, axis=0), r, gn_ref[...])


def _gla_tail(xb, w_in_t, g, gla_norm, state_gla):
    once = dict(pipeline_mode=pl.Buffered(1))
    return pl.pallas_call(
        _gla_tail_kernel,
        grid=(GLA_HEADS,),
        in_specs=[
            pl.BlockSpec((ROW_TILE, D_MODEL), lambda h: (N_PROMPT_TILES, 0)),
            *_head_weight_specs(**once),
            pl.BlockSpec((ROW_TILE, HEAD_K), lambda h: (N_PROMPT_TILES, h)),
            pl.BlockSpec((1, HEAD_V), lambda h: (0, h)),
            pl.BlockSpec((DEC_BATCH, None, HEAD_K, HEAD_V), lambda h: (0, h, 0, 0), **once),
        ],
        out_specs=[
            pl.BlockSpec((ROW_TILE, HEAD_V), lambda h: (0, h)),
            pl.BlockSpec((None, HEAD_K, HEAD_V), lambda h: (h, 0, 0)),
            pl.BlockSpec((DEC_BATCH, None, HEAD_K, HEAD_V), lambda h: (0, h, 0, 0)),
        ],
        out_shape=[
            jax.ShapeDtypeStruct((ROW_TILE, D_MODEL), jnp.float32),
            jax.ShapeDtypeStruct((GLA_HEADS, HEAD_K, HEAD_V), jnp.float32),
            jax.ShapeDtypeStruct((DEC_BATCH, GLA_HEADS, HEAD_K, HEAD_V), jnp.float32),
        ],
        scratch_shapes=[pltpu.VMEM((HEAD_ROWS, D_MODEL), jnp.bfloat16)],
        compiler_params=pltpu.CompilerParams(
            dimension_semantics=("arbitrary",), vmem_limit_bytes=VMEM_LIMIT),
        name="gla_tail",
    )(xb, *([w_in_t] * 4), g, gla_norm, state_gla)


def _gla_prompt_kernel(x_ref, wq_ref, wk_ref, wv_ref, wr_ref, g_ref, gn_ref, s0_ref,
                       y_ref, sp_ref, wbf, s_scr):
    i = pl.program_id(1)

    @pl.when(i == 0)
    def _start_head():
        _cast_head_weights((wq_ref, wk_ref, wv_ref, wr_ref), wbf)
        s_scr[...] = s0_ref[...]

    q, k, v, r = _project_head(x_ref[...], wbf)
    masks = _score_masks(PROMPT_CHUNK)
    q_inter, k_dec, b, levels = _gla_prepare(q, k, g_ref[...], PROMPT_CHUNK)
    v_bf = v.astype(jnp.bfloat16)
    outs = []
    for c in range(ROW_TILE // PROMPT_CHUNK):
        o, s_new = _gla_chunk(c, PROMPT_CHUNK, q_inter, k_dec, b, levels, v_bf, masks, s_scr[...])
        s_scr[...] = s_new
        outs.append(o)
    y_ref[...] = _gla_finish(jnp.concatenate(outs, axis=0), r, gn_ref[...])

    @pl.when(i == N_PROMPT_TILES - 1)
    def _final_state():
        sp_ref[...] = s_scr[...]


def _gla_prompt(xb, w_in_t, g, gla_norm, s_meta):
    return pl.pallas_call(
        _gla_prompt_kernel,
        grid=(GLA_HEADS, N_PROMPT_TILES),
        in_specs=[
            pl.BlockSpec((ROW_TILE, D_MODEL), lambda h, i: (i, 0)),
            *_head_weight_specs(),
            pl.BlockSpec((ROW_TILE, HEAD_K), lambda h, i: (i, h)),
            pl.BlockSpec((1, HEAD_V), lambda h, i: (0, h)),
            pl.BlockSpec((None, HEAD_K, HEAD_V), lambda h, i: (h, 0, 0)),
        ],
        out_specs=[
            pl.BlockSpec((ROW_TILE, HEAD_V), lambda h, i: (i, h)),
            pl.BlockSpec((None, HEAD_K, HEAD_V), lambda h, i: (h, 0, 0)),
        ],
        out_shape=[
            jax.ShapeDtypeStruct((SEQ, D_MODEL), jnp.float32),
            jax.ShapeDtypeStruct((GLA_HEADS, HEAD_K, HEAD_V), jnp.float32),
        ],
        scratch_shapes=[pltpu.VMEM((HEAD_ROWS, D_MODEL), jnp.bfloat16),
                        pltpu.VMEM((HEAD_K, HEAD_V), jnp.float32)],
        compiler_params=pltpu.CompilerParams(
            dimension_semantics=("arbitrary", "arbitrary"), vmem_limit_bytes=VMEM_LIMIT),
        name="gla_prompt",
    )(xb, *([w_in_t] * 4), g, gla_norm, s_meta)


MIX_TN = 256
N_MIX_GROUPS = 5


def _mix_kernel(x_ref, yap_ref, yat_ref, wcb_ref, wcc_ref, wcx_ref, wga_ref, wgb_ref, cw_ref, cache_ref,
                m_ref, ctail_ref, clast_ref, wbf, carry):
    i = pl.program_id(1)

    @pl.when(i == 0)
    def _cast():
        for grp, w_ref in enumerate((wcb_ref, wcc_ref, wcx_ref, wga_ref, wgb_ref)):
            wbf[grp] = w_ref[0].astype(jnp.bfloat16)

    x = x_ref[...]
    cb, cc, cx, ga, gb = [
        lax.dot_general(x, wbf[grp], _NT, preferred_element_type=jnp.float32)
        for grp in range(N_MIX_GROUPS)]
    c = cc * cx
    row = lax.broadcasted_iota(jnp.int32, (ROW_TILE, MIX_TN), 0)
    prev1 = pltpu.roll(c, 1, 0)
    prev2 = pltpu.roll(c, 2, 0)
    w0 = cw_ref[0:1, :]
    w1 = cw_ref[1:2, :]
    w2 = cw_ref[2:3, :]

    def finish(p1, p2, ya_ref):
        conv = w0 * p2 + w1 * p1 + w2 * c
        m = _sigmoid(ga) * ya_ref[...] + _sigmoid(gb) * (cb * conv)
        m_ref[...] = m.astype(jnp.bfloat16)

    @pl.when(i == 0)
    def _tail():
        def stream_rows(j):
            parts = [jnp.zeros((SAMPLE_ROW0, MIX_TN), jnp.float32)]
            for s in range(DEC_BATCH):
                parts.append(jnp.broadcast_to(cache_ref[2 * s + j:2 * s + j + 1, :], (DEC_SEQ, MIX_TN)))
            parts.append(jnp.zeros((ROW_TILE - SAMPLE_ROW0 - SAMPLE_ROWS, MIX_TN), jnp.float32))
            return jnp.concatenate(parts, axis=0)

        old = stream_rows(0)
        new = stream_rows(1)
        in_sample = (row >= SAMPLE_ROW0) & (row < SAMPLE_ROW0 + SAMPLE_ROWS)
        p16 = (row - SAMPLE_ROW0) & (DEC_SEQ - 1)
        first = in_sample & (p16 == 0)
        second = in_sample & (p16 == 1)
        p1 = jnp.where(row == META_ROW0, 0.0, jnp.where(first, new, prev1))
        p2 = jnp.where((row == META_ROW0) | (row == META_ROW0 + 1), 0.0,
                       jnp.where(first, old, jnp.where(second, new, prev2)))
        finish(p1, p2, yat_ref)
        ctail_ref[...] = c
        carry[...] = c[CHUNK - 8:CHUNK]

    @pl.when(i > 0)
    def _prompt():
        hist = carry[...]
        p1 = jnp.where(row == 0, jnp.broadcast_to(hist[7:8], c.shape), prev1)
        p2 = jnp.where(row == 0, jnp.broadcast_to(hist[6:7], c.shape),
                       jnp.where(row == 1, jnp.broadcast_to(hist[7:8], c.shape), prev2))
        finish(p1, p2, yap_ref)
        carry[...] = c[ROW_TILE - 8:]

    @pl.when(i == N_TILES - 1)
    def _last():
        clast_ref[...] = c[ROW_TILE - 8:]


def _mix(xb, ya_prompt, ya_tail, w_in_t, conv_w, cache_rows):
    rb = _tile_first_tail
    tile = lambda cj, i: (rb(i), cj)
    top = lambda cj, i: (0, cj)
    w_spec = lambda grp: pl.BlockSpec(
        (pl.Element(1), pl.Element(MIX_TN), pl.Element(D_MODEL)),
        lambda cj, i: (0, pl.multiple_of(GATE_COL0 + GATE_RANK + grp * D_MODEL + cj * MIX_TN, GATE_RANK), 0))
    return pl.pallas_call(
        _mix_kernel,
        grid=(D_MODEL // MIX_TN, N_TILES),
        in_specs=[
            pl.BlockSpec((ROW_TILE, D_MODEL), lambda cj, i: (rb(i), 0)),
            pl.BlockSpec((ROW_TILE, MIX_TN), lambda cj, i: (jnp.maximum(i - 1, 0), cj)),
            pl.BlockSpec((ROW_TILE, MIX_TN), top),
            *[w_spec(grp) for grp in range(N_MIX_GROUPS)],
            pl.BlockSpec((3, MIX_TN), top),
            pl.BlockSpec((2 * DEC_BATCH, MIX_TN), top),
        ],
        out_specs=[
            pl.BlockSpec((ROW_TILE, MIX_TN), tile),
            pl.BlockSpec((ROW_TILE, MIX_TN), top),
            pl.BlockSpec((8, MIX_TN), top),
        ],
        out_shape=[
            jax.ShapeDtypeStruct((ROWS, D_MODEL), jnp.bfloat16),
            jax.ShapeDtypeStruct((ROW_TILE, D_MODEL), jnp.float32),
            jax.ShapeDtypeStruct((8, D_MODEL), jnp.float32),
        ],
        scratch_shapes=[pltpu.VMEM((N_MIX_GROUPS, MIX_TN, D_MODEL), jnp.bfloat16),
                        pltpu.VMEM((8, MIX_TN), jnp.float32)],
        compiler_params=pltpu.CompilerParams(
            dimension_semantics=("arbitrary", "arbitrary"), vmem_limit_bytes=VMEM_LIMIT),
        name="proj_conv_mix",
    )(xb, ya_prompt, ya_tail, *([w_in_t] * N_MIX_GROUPS), conv_w, cache_rows)


SUB_ROWS = 128


def _out_proj_kernel(m_ref, xn_ref, w_ref, lg_ref, lb_ref, h_ref, hb_ref):
    for r in range(0, ROW_TILE, SUB_ROWS):
        rows = slice(r, r + SUB_ROWS)
        mix = jnp.dot(m_ref[rows, :], w_ref[...], preferred_element_type=jnp.float32)
        h = _layer_norm(ALPHA * xn_ref[rows, :] + mix, lg_ref[...], lb_ref[...])
        h_ref[rows, :] = h
        hb_ref[rows, :] = h.astype(jnp.bfloat16)


def _out_proj(m, xn, w_out, ln_g, ln_b):
    row = lambda i: (i, 0)
    full = lambda i: (0, 0)
    return pl.pallas_call(
        _out_proj_kernel,
        grid=(N_TILES,),
        in_specs=[
            pl.BlockSpec((ROW_TILE, D_MODEL), row),
            pl.BlockSpec((ROW_TILE, D_MODEL), row),
            pl.BlockSpec((D_MODEL, D_MODEL), full),
            pl.BlockSpec((1, D_MODEL), full),
            pl.BlockSpec((1, D_MODEL), full),
        ],
        out_specs=[pl.BlockSpec((ROW_TILE, D_MODEL), row), pl.BlockSpec((ROW_TILE, D_MODEL), row)],
        out_shape=[
            jax.ShapeDtypeStruct((ROWS, D_MODEL), jnp.float32),
            jax.ShapeDtypeStruct((ROWS, D_MODEL), jnp.bfloat16),
        ],
        compiler_params=pltpu.CompilerParams(
            dimension_semantics=("arbitrary",), vmem_limit_bytes=VMEM_LIMIT),
        name="out_proj_ln",
    )(m, xn, w_out, ln_g, ln_b)


FF_TILE = 512
N_FF_TILES = D_FF // FF_TILE
FFN_UP_ROWS = BIG_ROW_TILE


def _ffn_up_kernel(hb_ref, wg_ref, wu_ref, act_ref, wg_bf, wu_bf):
    @pl.when(pl.program_id(1) == 0)
    def _cast():
        for r in range(0, D_MODEL, W_SLAB):
            wg_bf[r:r + W_SLAB, :] = wg_ref[r:r + W_SLAB, :].astype(jnp.bfloat16)
            wu_bf[r:r + W_SLAB, :] = wu_ref[r:r + W_SLAB, :].astype(jnp.bfloat16)

    hb = hb_ref[...]
    gate = jnp.dot(hb, wg_bf[...], preferred_element_type=jnp.float32)
    up = jnp.dot(hb, wu_bf[...], preferred_element_type=jnp.float32)
    act_ref[...] = (gate * _sigmoid(gate) * up).astype(jnp.bfloat16)


def _ffn_up(hb, w_ffn_in):
    return pl.pallas_call(
        _ffn_up_kernel,
        grid=(N_FF_TILES, ROWS // FFN_UP_ROWS),
        in_specs=[
            pl.BlockSpec((FFN_UP_ROWS, D_MODEL), lambda j, i: (i, 0)),
            pl.BlockSpec((None, D_MODEL, FF_TILE), lambda j, i: (0, 0, j)),
            pl.BlockSpec((None, D_MODEL, FF_TILE), lambda j, i: (0, 0, N_FF_TILES + j)),
        ],
        out_specs=pl.BlockSpec((FFN_UP_ROWS, FF_TILE), lambda j, i: (i, j)),
        out_shape=jax.ShapeDtypeStruct((ROWS, D_FF), jnp.bfloat16),
        scratch_shapes=[pltpu.VMEM((D_MODEL, FF_TILE), jnp.bfloat16),
                        pltpu.VMEM((D_MODEL, FF_TILE), jnp.bfloat16)],
        compiler_params=pltpu.CompilerParams(
            dimension_semantics=("arbitrary", "arbitrary"), vmem_limit_bytes=VMEM_LIMIT),
        name="ffn_up",
    )(hb, w_ffn_in, w_ffn_in)


DOWN_TILE = 256
N_DOWN_PROMPT = SEQ // DOWN_TILE
N_DOWN_TILES = ROWS // DOWN_TILE


def _ffn_down_kernel(act_ref, h_ref, wd_ref, lg_ref, lb_ref, yp_ref, yt_ref):
    i = pl.program_id(0)

    def rows_out(o_ref):
        for r in range(0, DOWN_TILE, SUB_ROWS):
            rows = slice(r, r + SUB_ROWS)
            f = jnp.dot(act_ref[rows, :], wd_ref[...], preferred_element_type=jnp.float32)
            o_ref[rows, :] = _layer_norm(ALPHA * h_ref[rows, :] + f, lg_ref[...], lb_ref[...])

    @pl.when(i < N_DOWN_PROMPT)
    def _prompt():
        rows_out(yp_ref)

    @pl.when(i >= N_DOWN_PROMPT)
    def _tail():
        rows_out(yt_ref)


def _ffn_down(act, h, w_down, ln_g, ln_b):
    row = lambda i: (i, 0)
    full = lambda i: (0, 0)
    return pl.pallas_call(
        _ffn_down_kernel,
        grid=(N_DOWN_TILES,),
        in_specs=[
            pl.BlockSpec((DOWN_TILE, D_FF), row),
            pl.BlockSpec((DOWN_TILE, D_MODEL), row),
            pl.BlockSpec((D_FF, D_MODEL), full, pipeline_mode=pl.Buffered(1)),
            pl.BlockSpec((1, D_MODEL), full),
            pl.BlockSpec((1, D_MODEL), full),
        ],
        out_specs=[
            pl.BlockSpec((None, DOWN_TILE, D_MODEL), lambda i: (0, jnp.minimum(i, N_DOWN_PROMPT - 1), 0)),
            pl.BlockSpec((DOWN_TILE, D_MODEL), lambda i: (jnp.maximum(i - N_DOWN_PROMPT, 0), 0)),
        ],
        out_shape=[
            jax.ShapeDtypeStruct((1, SEQ, D_MODEL), jnp.float32),
            jax.ShapeDtypeStruct((ROW_TILE, D_MODEL), jnp.float32),
        ],
        compiler_params=pltpu.CompilerParams(
            dimension_semantics=("arbitrary",), vmem_limit_bytes=VMEM_LIMIT),
        name="ffn_down_ln",
    )(act, h, w_down, ln_g, ln_b)


def kernel(x_prompt, x_sample, state_gla, cache_conv, meta_tokens, ln_in_g, ln_in_b, w_in, w_gate_up, b_gate, gla_norm, conv_w, w_out, ln1_g, ln1_b, w_ffn_in, w_ffn_out, ln2_g, ln2_b):
    f32, bf16 = jnp.float32, jnp.bfloat16
    tail = jnp.concatenate([
        jnp.zeros((META_ROW0, D_MODEL), f32),
        meta_tokens.astype(f32),
        x_sample.reshape(SAMPLE_ROWS, D_MODEL),
        jnp.zeros((ROW_TILE - SAMPLE_ROW0 - SAMPLE_ROWS, D_MODEL), f32),
    ], axis=0)
    w_gu = jnp.pad(w_gate_up[0], ((0, LANE - GATE_RANK), (0, 0))).astype(bf16)
    vec = lambda p: p.reshape(1, -1).astype(f32)

    w_in_t = jnp.swapaxes(w_in, 1, 2)
    xn, xb, g = _ln_gate(x_prompt, tail, vec(ln_in_g), vec(ln_in_b), w_in_t, w_gu, vec(b_gate[0]))
    ya_tail, s_meta, s_sample = _gla_tail(xb, w_in_t, g, vec(gla_norm[0]), state_gla[0])
    ya_prompt, s_prompt = _gla_prompt(xb, w_in_t, g, vec(gla_norm[0]), s_meta)
    m, c_tail, c_last = _mix(xb, ya_prompt, ya_tail, w_in_t, conv_w[0],
                             cache_conv[0].reshape(2 * DEC_BATCH, D_MODEL))
    h, hb = _out_proj(m, xn, w_out[0].astype(bf16), vec(ln1_g[0]), vec(ln1_b[0]))
    act = _ffn_up(hb, w_ffn_in)
    y_prompt, y_tail = _ffn_down(act, h, w_ffn_out[0].astype(bf16), vec(ln2_g[0]), vec(ln2_b[0]))

    y_sample = y_tail[SAMPLE_ROW0:SAMPLE_ROW0 + SAMPLE_ROWS].reshape(DEC_BATCH, DEC_SEQ, D_MODEL)
    cache_prompt = c_last[6:8][None, None]
    cache_sample = c_tail[SAMPLE_ROW0:SAMPLE_ROW0 + SAMPLE_ROWS].reshape(
        DEC_BATCH, DEC_SEQ, D_MODEL)[:, DEC_SEQ - 2:][None]
    return (y_prompt, y_sample, s_prompt[None, None], cache_prompt, s_sample[None], cache_sample)
```

```python
import functools

import jax
import jax.numpy as jnp
from jax import lax
from jax.experimental import pallas as pl
from jax.experimental.pallas import tpu as pltpu

D_MODEL = 2048
SEQ = 8192
DEC_BATCH = 16
DEC_SEQ = 16
CHUNK = 64
N_META = 16
GLA_HEADS = 4
HEAD_K = 256
HEAD_V = 512
GLA_DK = GLA_HEADS * HEAD_K
GATE_RANK = 16
GATE_TEMP = 16.0
D_FF = 5632
ALPHA = 2.0 ** 0.25
LN_EPS = 1e-5
RMS_EPS = 1e-6

ROW_TILE = 512
N_PROMPT_TILES = SEQ // ROW_TILE
N_TILES = N_PROMPT_TILES + 1
ROWS = N_TILES * ROW_TILE
META_ROW0 = CHUNK - N_META
SAMPLE_ROW0 = CHUNK
SAMPLE_ROWS = DEC_BATCH * DEC_SEQ
LANE = 128
GATE_COL0 = 2 * GLA_DK + 2 * D_MODEL
VMEM_LIMIT = 56 * 1024 * 1024

_NT = (((1,), (1,)), ((), ()))
_TN = (((0,), (0,)), ((), ()))


def _tile_first_tail(i):
    return (i + N_PROMPT_TILES) % N_TILES


def _layer_norm(x, g, b):
    mu = jnp.mean(x, axis=-1, keepdims=True)
    xc = x - mu
    var = jnp.mean(xc * xc, axis=-1, keepdims=True)
    return xc * lax.rsqrt(var + LN_EPS) * g + b


def _log_sigmoid(x):
    return jnp.minimum(x, 0.0) - jnp.log1p(jnp.exp(-jnp.abs(x)))


def _sigmoid(x):
    return 1.0 / (1.0 + jnp.exp(-x))


def _ln_gate_kernel(xp_ref, xt_ref, lg_ref, lb_ref, wa_ref, wgu_ref, bg_ref, xn_ref, xb_ref, g_ref):
    x = jnp.where(pl.program_id(0) < N_PROMPT_TILES, xp_ref[...], xt_ref[...])
    xn = _layer_norm(x, lg_ref[...], lb_ref[...])
    xn_ref[...] = xn
    xb = xn.astype(jnp.bfloat16)
    xb_ref[...] = xb
    a = lax.dot_general(xb, wa_ref[...].astype(jnp.bfloat16), _NT, preferred_element_type=jnp.float32)
    z = jnp.dot(a.astype(jnp.bfloat16), wgu_ref[...], preferred_element_type=jnp.float32)
    g_ref[...] = _log_sigmoid(z + bg_ref[...]) * (1.0 / GATE_TEMP)


def _ln_gate(x_prompt, x_tail, ln_g, ln_b, w_in_t, w_gu, b_gate):
    row = lambda i: (i, 0)
    full = lambda i: (0, 0)
    return pl.pallas_call(
        _ln_gate_kernel,
        grid=(N_TILES,),
        in_specs=[
            pl.BlockSpec((None, ROW_TILE, D_MODEL), lambda i: (0, jnp.minimum(i, N_PROMPT_TILES - 1), 0)),
            pl.BlockSpec((ROW_TILE, D_MODEL), full),
            pl.BlockSpec((1, D_MODEL), full),
            pl.BlockSpec((1, D_MODEL), full),
            pl.BlockSpec((None, LANE, D_MODEL), lambda i: (0, GATE_COL0 // LANE, 0)),
            pl.BlockSpec((LANE, GLA_DK), full),
            pl.BlockSpec((1, GLA_DK), full),
        ],
        out_specs=[
            pl.BlockSpec((ROW_TILE, D_MODEL), row),
            pl.BlockSpec((ROW_TILE, D_MODEL), row),
            pl.BlockSpec((ROW_TILE, GLA_DK), row),
        ],
        out_shape=[
            jax.ShapeDtypeStruct((ROWS, D_MODEL), jnp.float32),
            jax.ShapeDtypeStruct((ROWS, D_MODEL), jnp.bfloat16),
            jax.ShapeDtypeStruct((ROWS, GLA_DK), jnp.float32),
        ],
        compiler_params=pltpu.CompilerParams(
            dimension_semantics=("arbitrary",), vmem_limit_bytes=VMEM_LIMIT),
        name="ln_gate",
    )(x_prompt, x_tail, ln_g, ln_b, w_in_t, w_gu, b_gate)


W_SLAB = 256
BIG_ROW_TILE = 1088
N_BIG_TILES = ROWS // BIG_ROW_TILE


PROMPT_CHUNK = 256
HEAD_ROWS = 2 * HEAD_K + 2 * HEAD_V


def _cast_head_weights(w_refs, wbf):
    off = 0
    for w_ref in w_refs:
        for r in range(0, w_ref.shape[0], W_SLAB):
            wbf[off + r:off + r + W_SLAB, :] = w_ref[r:r + W_SLAB, :].astype(jnp.bfloat16)
        off += w_ref.shape[0]


def _project_head(x, wbf):
    qk = lax.dot_general(x, wbf[0:2 * HEAD_K, :], _NT, preferred_element_type=jnp.float32)
    vr = lax.dot_general(x, wbf[2 * HEAD_K:, :], _NT, preferred_element_type=jnp.float32)
    return qk[:, :HEAD_K], qk[:, HEAD_K:], vr[:, :HEAD_V], vr[:, HEAD_V:]


def _head_weight_specs(**kw):
    kq = GLA_DK // HEAD_K
    return [
        pl.BlockSpec((None, HEAD_K, D_MODEL), lambda h, *_: (0, h, 0), **kw),
        pl.BlockSpec((None, HEAD_K, D_MODEL), lambda h, *_: (0, kq + h, 0), **kw),
        pl.BlockSpec((None, HEAD_V, D_MODEL), lambda h, *_: (0, kq + h, 0), **kw),
        pl.BlockSpec((None, HEAD_V, D_MODEL), lambda h, *_: (0, 2 * kq + h, 0), **kw),
    ]


def _row_of_block(b, block, r):
    n, w = b.shape
    b3 = b.reshape(n // block, block, w)
    return jnp.broadcast_to(b3[:, r:r + 1, :], b3.shape).reshape(n, w)


def _half_sizes(chunk):
    return tuple(chunk >> (j + 1) for j in range(chunk.bit_length() - 1))


def _gla_prepare(q, k, g, chunk):
    n, w = q.shape
    row = lax.broadcasted_iota(jnp.int32, (n, w), 0)
    pos = row & (chunk - 1)
    b = g
    for sh in reversed(_half_sizes(chunk)):
        b = b + jnp.where(pos >= sh, pltpu.roll(b, sh, 0), 0.0)
    qs = q * (HEAD_K ** -0.5)
    b_last = _row_of_block(b, chunk, chunk - 1)
    q_inter = (qs * jnp.exp(b)).astype(jnp.bfloat16)
    k_dec = (k * jnp.exp(b_last - b)).astype(jnp.bfloat16)
    levels = []
    for s in _half_sizes(chunk):
        upper = (row & (2 * s - 1)) >= s
        if s >= 4:
            b_mid = _row_of_block(b, 2 * s, s - 1)
            d = jnp.where(upper, b - b_mid, b_mid - b)
        elif s == 2:
            p4 = row & 3
            g_prev = pltpu.roll(g, 1, 0)
            g_next = pltpu.roll(g, n - 1, 0)
            d = jnp.where(p4 == 0, g_next, jnp.where(p4 == 1, 0.0, jnp.where(p4 == 2, g, g + g_prev)))
        else:
            d = jnp.where(upper, g, 0.0)
        x = (jnp.where(upper, qs, k) * jnp.exp(d)).astype(jnp.bfloat16)
        levels.append((x, x))
    levels.append((qs.astype(jnp.bfloat16), k.astype(jnp.bfloat16)))
    return q_inter, k_dec, b, levels


def _score_masks(chunk):
    ii = lax.broadcasted_iota(jnp.int32, (chunk, chunk), 0)
    jj = lax.broadcasted_iota(jnp.int32, (chunk, chunk), 1)
    masks = []
    for s in _half_sizes(chunk):
        blk = 2 * s
        same = (ii & ~(blk - 1)) == (jj & ~(blk - 1))
        masks.append(same & ((ii & (blk - 1)) >= s) & ((jj & (blk - 1)) < s))
    masks.append(ii == jj)
    return masks


def _gla_chunk(c, chunk, q_inter, k_dec, b, levels, v_bf, masks, s_val):
    r0 = c * chunk
    sl = slice(r0, r0 + chunk)
    scores = jnp.zeros((chunk, chunk), jnp.float32)
    for (qe, ke), m in zip(levels, masks):
        p = lax.dot_general(qe[sl], ke[sl], _NT, preferred_element_type=jnp.float32)
        scores = jnp.where(m, p, scores)
    vc = v_bf[sl]
    o = jnp.dot(scores.astype(jnp.bfloat16), vc, preferred_element_type=jnp.float32)
    o = o + jnp.dot(q_inter[sl], s_val.astype(jnp.bfloat16), preferred_element_type=jnp.float32)
    eb = jnp.exp(b[r0 + chunk - 1:r0 + chunk, :])
    eb_t = jnp.transpose(jnp.broadcast_to(eb, (LANE, HEAD_K)))
    decay = jnp.concatenate([eb_t] * (HEAD_V // LANE), axis=1)
    ds = lax.dot_general(k_dec[sl], vc, _TN, preferred_element_type=jnp.float32)
    return o, decay * s_val + ds


def _gla_finish(o, r, gn):
    o = o * lax.rsqrt(jnp.mean(o * o, axis=-1, keepdims=True) + RMS_EPS)
    return o * gn * (r * _sigmoid(r))


def _gla_tail_kernel(x_ref, wq_ref, wk_ref, wv_ref, wr_ref, g_ref, gn_ref, sin_ref,
                     y_ref, sm_ref, ss_ref, wbf):
    _cast_head_weights((wq_ref, wk_ref, wv_ref, wr_ref), wbf)
    q, k, v, r = _project_head(x_ref[...], wbf)
    masks = _score_masks(CHUNK)

    def padded(x):
        parts = [x[0:CHUNK]]
        zeros = jnp.zeros((META_ROW0, x.shape[1]), x.dtype)
        for s in range(DEC_BATCH):
            parts += [zeros, x[SAMPLE_ROW0 + s * DEC_SEQ:SAMPLE_ROW0 + (s + 1) * DEC_SEQ]]
        return jnp.concatenate(parts, axis=0)

    n = (1 + DEC_BATCH) * CHUNK
    q = padded(q)
    k = padded(k)
    g = padded(g_ref[...])
    v = padded(v)
    live = (lax.broadcasted_iota(jnp.int32, (n, HEAD_K), 0) & (CHUNK - 1)) >= META_ROW0
    k = jnp.where(live, k, 0.0)
    g = jnp.where(live, g, 0.0)
    q_inter, k_dec, b, levels = _gla_prepare(q, k, g, CHUNK)
    v_bf = v.astype(jnp.bfloat16)
    outs = [jnp.zeros((META_ROW0, HEAD_V), jnp.float32)]
    o, s_meta = _gla_chunk(0, CHUNK, q_inter, k_dec, b, levels, v_bf, masks,
                           jnp.zeros((HEAD_K, HEAD_V), jnp.float32))
    sm_ref[...] = s_meta
    outs.append(o[META_ROW0:])
    for s in range(DEC_BATCH):
        o, s_new = _gla_chunk(1 + s, CHUNK, q_inter, k_dec, b, levels, v_bf, masks, sin_ref[s])
        ss_ref[s] = s_new
        outs.append(o[META_ROW0:])
    outs.append(jnp.zeros((ROW_TILE - SAMPLE_ROW0 - SAMPLE_ROWS, HEAD_V), jnp.float32))
    y_ref[...] = _gla_finish(jnp.concatenate(outs, axis=0), r, gn_ref[...])


def _gla_tail(xb, w_in_t, g, gla_norm, state_gla):
    once = dict(pipeline_mode=pl.Buffered(1))
    return pl.pallas_call(
        _gla_tail_kernel,
        grid=(GLA_HEADS,),
        in_specs=[
            pl.BlockSpec((ROW_TILE, D_MODEL), lambda h: (N_PROMPT_TILES, 0)),
            *_head_weight_specs(**once),
            pl.BlockSpec((ROW_TILE, HEAD_K), lambda h: (N_PROMPT_TILES, h)),
            pl.BlockSpec((1, HEAD_V), lambda h: (0, h)),
            pl.BlockSpec((DEC_BATCH, None, HEAD_K, HEAD_V), lambda h: (0, h, 0, 0), **once),
        ],
        out_specs=[
            pl.BlockSpec((ROW_TILE, HEAD_V), lambda h: (0, h)),
            pl.BlockSpec((None, HEAD_K, HEAD_V), lambda h: (h, 0, 0)),
            pl.BlockSpec((DEC_BATCH, None, HEAD_K, HEAD_V), lambda h: (0, h, 0, 0)),
        ],
        out_shape=[
            jax.ShapeDtypeStruct((ROW_TILE, D_MODEL), jnp.float32),
            jax.ShapeDtypeStruct((GLA_HEADS, HEAD_K, HEAD_V), jnp.float32),
            jax.ShapeDtypeStruct((DEC_BATCH, GLA_HEADS, HEAD_K, HEAD_V), jnp.float32),
        ],
        scratch_shapes=[pltpu.VMEM((HEAD_ROWS, D_MODEL), jnp.bfloat16)],
        compiler_params=pltpu.CompilerParams(
            dimension_semantics=("arbitrary",), vmem_limit_bytes=VMEM_LIMIT),
        name="gla_tail",
    )(xb, *([w_in_t] * 4), g, gla_norm, state_gla)


N_GLA_STEPS = GLA_HEADS * N_PROMPT_TILES
MIX_W_ROWS = 5 * D_MODEL
MIX_W_SLAB = MIX_W_ROWS // N_GLA_STEPS
OUT_W_SLAB = D_MODEL // N_GLA_STEPS


def _gla_prompt_kernel(x_ref, wq_ref, wk_ref, wv_ref, wr_ref, g_ref, gn_ref, s0_ref, wmix_ref, wout_ref,
                       y_ref, sp_ref, wmix_bf_ref, wout_bf_ref, wbf, s_scr):
    i = pl.program_id(1)

    wmix_bf_ref[...] = wmix_ref[0].astype(jnp.bfloat16)
    wout_bf_ref[...] = wout_ref[...].astype(jnp.bfloat16)

    @pl.when(i == 0)
    def _start_head():
        _cast_head_weights((wq_ref, wk_ref, wv_ref, wr_ref), wbf)
        s_scr[...] = s0_ref[...]

    q, k, v, r = _project_head(x_ref[...], wbf)
    masks = _score_masks(PROMPT_CHUNK)
    q_inter, k_dec, b, levels = _gla_prepare(q, k, g_ref[...], PROMPT_CHUNK)
    v_bf = v.astype(jnp.bfloat16)
    outs = []
    for c in range(ROW_TILE // PROMPT_CHUNK):
        o, s_new = _gla_chunk(c, PROMPT_CHUNK, q_inter, k_dec, b, levels, v_bf, masks, s_scr[...])
        s_scr[...] = s_new
        outs.append(o)
    y_ref[...] = _gla_finish(jnp.concatenate(outs, axis=0), r, gn_ref[...])

    @pl.when(i == N_PROMPT_TILES - 1)
    def _final_state():
        sp_ref[...] = s_scr[...]


def _gla_prompt(xb, w_in_t, g, gla_norm, s_meta, w_out):
    step = lambda h, i: h * N_PROMPT_TILES + i
    return pl.pallas_call(
        _gla_prompt_kernel,
        grid=(GLA_HEADS, N_PROMPT_TILES),
        in_specs=[
            pl.BlockSpec((ROW_TILE, D_MODEL), lambda h, i: (i, 0)),
            *_head_weight_specs(),
            pl.BlockSpec((ROW_TILE, HEAD_K), lambda h, i: (i, h)),
            pl.BlockSpec((1, HEAD_V), lambda h, i: (0, h)),
            pl.BlockSpec((None, HEAD_K, HEAD_V), lambda h, i: (h, 0, 0)),
            pl.BlockSpec((pl.Element(1), pl.Element(MIX_W_SLAB), pl.Element(D_MODEL)),
                         lambda h, i: (0, pl.multiple_of(GATE_COL0 + GATE_RANK + step(h, i) * MIX_W_SLAB,
                                                         GATE_RANK), 0)),
            pl.BlockSpec((None, OUT_W_SLAB, D_MODEL), lambda h, i: (0, step(h, i), 0)),
        ],
        out_specs=[
            pl.BlockSpec((ROW_TILE, HEAD_V), lambda h, i: (i, h)),
            pl.BlockSpec((None, HEAD_K, HEAD_V), lambda h, i: (h, 0, 0)),
            pl.BlockSpec((MIX_W_SLAB, D_MODEL), lambda h, i: (step(h, i), 0)),
            pl.BlockSpec((OUT_W_SLAB, D_MODEL), lambda h, i: (step(h, i), 0)),
        ],
        out_shape=[
            jax.ShapeDtypeStruct((SEQ, D_MODEL), jnp.float32),
            jax.ShapeDtypeStruct((GLA_HEADS, HEAD_K, HEAD_V), jnp.float32),
            jax.ShapeDtypeStruct((MIX_W_ROWS, D_MODEL), jnp.bfloat16),
            jax.ShapeDtypeStruct((D_MODEL, D_MODEL), jnp.bfloat16),
        ],
        scratch_shapes=[pltpu.VMEM((HEAD_ROWS, D_MODEL), jnp.bfloat16),
                        pltpu.VMEM((HEAD_K, HEAD_V), jnp.float32)],
        compiler_params=pltpu.CompilerParams(
            dimension_semantics=("arbitrary", "arbitrary"), vmem_limit_bytes=VMEM_LIMIT),
        name="gla_prompt",
    )(xb, *([w_in_t] * 4), g, gla_norm, s_meta, w_in_t, w_out)


MIX_TN = 512
N_MIX_GROUPS = 5


def _mix_kernel(x_ref, yap_ref, yat_ref, wcb_ref, wcc_ref, wcx_ref, wga_ref, wgb_ref, cw_ref, cache_ref,
                m_ref, ctail_ref, clast_ref, carry):
    i = pl.program_id(1)
    x = x_ref[...]
    cb, cc, cx, ga, gb = [
        lax.dot_general(x, w_ref[...], _NT, preferred_element_type=jnp.float32)
        for w_ref in (wcb_ref, wcc_ref, wcx_ref, wga_ref, wgb_ref)]
    c = cc * cx
    row = lax.broadcasted_iota(jnp.int32, (ROW_TILE, MIX_TN), 0)
    prev1 = pltpu.roll(c, 1, 0)
    prev2 = pltpu.roll(c, 2, 0)
    w0 = cw_ref[0:1, :]
    w1 = cw_ref[1:2, :]
    w2 = cw_ref[2:3, :]

    def finish(p1, p2, ya_ref):
        conv = w0 * p2 + w1 * p1 + w2 * c
        m = _sigmoid(ga) * ya_ref[...] + _sigmoid(gb) * (cb * conv)
        m_ref[...] = m.astype(jnp.bfloat16)

    @pl.when(i == 0)
    def _tail():
        def stream_rows(j):
            parts = [jnp.zeros((SAMPLE_ROW0, MIX_TN), jnp.float32)]
            for s in range(DEC_BATCH):
                parts.append(jnp.broadcast_to(cache_ref[2 * s + j:2 * s + j + 1, :], (DEC_SEQ, MIX_TN)))
            parts.append(jnp.zeros((ROW_TILE - SAMPLE_ROW0 - SAMPLE_ROWS, MIX_TN), jnp.float32))
            return jnp.concatenate(parts, axis=0)

        old = stream_rows(0)
        new = stream_rows(1)
        in_sample = (row >= SAMPLE_ROW0) & (row < SAMPLE_ROW0 + SAMPLE_ROWS)
        p16 = (row - SAMPLE_ROW0) & (DEC_SEQ - 1)
        first = in_sample & (p16 == 0)
        second = in_sample & (p16 == 1)
        p1 = jnp.where(row == META_ROW0, 0.0, jnp.where(first, new, prev1))
        p2 = jnp.where((row == META_ROW0) | (row == META_ROW0 + 1), 0.0,
                       jnp.where(first, old, jnp.where(second, new, prev2)))
        finish(p1, p2, yat_ref)
        ctail_ref[...] = c
        carry[...] = c[CHUNK - 8:CHUNK]

    @pl.when(i > 0)
    def _prompt():
        hist = carry[...]
        p1 = jnp.where(row == 0, jnp.broadcast_to(hist[7:8], c.shape), prev1)
        p2 = jnp.where(row == 0, jnp.broadcast_to(hist[6:7], c.shape),
                       jnp.where(row == 1, jnp.broadcast_to(hist[7:8], c.shape), prev2))
        finish(p1, p2, yap_ref)
        carry[...] = c[ROW_TILE - 8:]

    @pl.when(i == N_TILES - 1)
    def _last():
        clast_ref[...] = c[ROW_TILE - 8:]


def _mix(xb, ya_prompt, ya_tail, w_mix_bf, conv_w, cache_rows):
    rb = _tile_first_tail
    tile = lambda cj, i: (rb(i), cj)
    top = lambda cj, i: (0, cj)
    nc = D_MODEL // MIX_TN
    w_spec = lambda grp: pl.BlockSpec((MIX_TN, D_MODEL), lambda cj, i: (grp * nc + cj, 0))
    return pl.pallas_call(
        _mix_kernel,
        grid=(D_MODEL // MIX_TN, N_TILES),
        in_specs=[
            pl.BlockSpec((ROW_TILE, D_MODEL), lambda cj, i: (rb(i), 0)),
            pl.BlockSpec((ROW_TILE, MIX_TN), lambda cj, i: (jnp.maximum(i - 1, 0), cj)),
            pl.BlockSpec((ROW_TILE, MIX_TN), top),
            *[w_spec(grp) for grp in range(N_MIX_GROUPS)],
            pl.BlockSpec((3, MIX_TN), top),
            pl.BlockSpec((2 * DEC_BATCH, MIX_TN), top),
        ],
        out_specs=[
            pl.BlockSpec((ROW_TILE, MIX_TN), tile),
            pl.BlockSpec((ROW_TILE, MIX_TN), top),
            pl.BlockSpec((8, MIX_TN), top),
        ],
        out_shape=[
            jax.ShapeDtypeStruct((ROWS, D_MODEL), jnp.bfloat16),
            jax.ShapeDtypeStruct((ROW_TILE, D_MODEL), jnp.float32),
            jax.ShapeDtypeStruct((8, D_MODEL), jnp.float32),
        ],
        scratch_shapes=[pltpu.VMEM((8, MIX_TN), jnp.float32)],
        compiler_params=pltpu.CompilerParams(
            dimension_semantics=("arbitrary", "arbitrary"), vmem_limit_bytes=VMEM_LIMIT),
        name="proj_conv_mix",
    )(xb, ya_prompt, ya_tail, *([w_mix_bf] * N_MIX_GROUPS), conv_w, cache_rows)


SUB_ROWS = 128


def _out_proj_kernel(m_ref, xn_ref, w_ref, lg_ref, lb_ref, h_ref, hb_ref):
    for r in range(0, ROW_TILE, SUB_ROWS):
        rows = slice(r, r + SUB_ROWS)
        mix = jnp.dot(m_ref[rows, :], w_ref[...], preferred_element_type=jnp.float32)
        h = _layer_norm(ALPHA * xn_ref[rows, :] + mix, lg_ref[...], lb_ref[...])
        h_ref[rows, :] = h
        hb_ref[rows, :] = h.astype(jnp.bfloat16)


def _out_proj(m, xn, w_out, ln_g, ln_b):
    row = lambda i: (i, 0)
    full = lambda i: (0, 0)
    return pl.pallas_call(
        _out_proj_kernel,
        grid=(N_TILES,),
        in_specs=[
            pl.BlockSpec((ROW_TILE, D_MODEL), row),
            pl.BlockSpec((ROW_TILE, D_MODEL), row),
            pl.BlockSpec((D_MODEL, D_MODEL), full),
            pl.BlockSpec((1, D_MODEL), full),
            pl.BlockSpec((1, D_MODEL), full),
        ],
        out_specs=[pl.BlockSpec((ROW_TILE, D_MODEL), row), pl.BlockSpec((ROW_TILE, D_MODEL), row)],
        out_shape=[
            jax.ShapeDtypeStruct((ROWS, D_MODEL), jnp.float32),
            jax.ShapeDtypeStruct((ROWS, D_MODEL), jnp.bfloat16),
        ],
        compiler_params=pltpu.CompilerParams(
            dimension_semantics=("arbitrary",), vmem_limit_bytes=VMEM_LIMIT),
        name="out_proj_ln",
    )(m, xn, w_out, ln_g, ln_b)


FF_TILE = 512
N_FF_TILES = D_FF // FF_TILE
FFN_UP_ROWS = BIG_ROW_TILE


N_FFN_UP_STEPS = N_FF_TILES * (ROWS // FFN_UP_ROWS)
DOWN_W_SLAB = D_FF // N_FFN_UP_STEPS


def _ffn_up_kernel(hb_ref, wg_ref, wu_ref, wd_ref, act_ref, wd_bf_ref, wg_bf, wu_bf):
    wd_bf_ref[...] = wd_ref[...].astype(jnp.bfloat16)

    @pl.when(pl.program_id(1) == 0)
    def _cast():
        for r in range(0, D_MODEL, W_SLAB):
            wg_bf[r:r + W_SLAB, :] = wg_ref[r:r + W_SLAB, :].astype(jnp.bfloat16)
            wu_bf[r:r + W_SLAB, :] = wu_ref[r:r + W_SLAB, :].astype(jnp.bfloat16)

    hb = hb_ref[...]
    gate = jnp.dot(hb, wg_bf[...], preferred_element_type=jnp.float32)
    up = jnp.dot(hb, wu_bf[...], preferred_element_type=jnp.float32)
    act_ref[...] = (gate * _sigmoid(gate) * up).astype(jnp.bfloat16)


def _ffn_up(hb, w_ffn_in, w_ffn_out):
    step = lambda j, i: j * (ROWS // FFN_UP_ROWS) + i
    return pl.pallas_call(
        _ffn_up_kernel,
        grid=(N_FF_TILES, ROWS // FFN_UP_ROWS),
        in_specs=[
            pl.BlockSpec((FFN_UP_ROWS, D_MODEL), lambda j, i: (i, 0)),
            pl.BlockSpec((None, D_MODEL, FF_TILE), lambda j, i: (0, 0, j)),
            pl.BlockSpec((None, D_MODEL, FF_TILE), lambda j, i: (0, 0, N_FF_TILES + j)),
            pl.BlockSpec((None, DOWN_W_SLAB, D_MODEL), lambda j, i: (0, step(j, i), 0)),
        ],
        out_specs=[
            pl.BlockSpec((FFN_UP_ROWS, FF_TILE), lambda j, i: (i, j)),
            pl.BlockSpec((DOWN_W_SLAB, D_MODEL), lambda j, i: (step(j, i), 0)),
        ],
        out_shape=[
            jax.ShapeDtypeStruct((ROWS, D_FF), jnp.bfloat16),
            jax.ShapeDtypeStruct((D_FF, D_MODEL), jnp.bfloat16),
        ],
        scratch_shapes=[pltpu.VMEM((D_MODEL, FF_TILE), jnp.bfloat16),
                        pltpu.VMEM((D_MODEL, FF_TILE), jnp.bfloat16)],
        compiler_params=pltpu.CompilerParams(
            dimension_semantics=("arbitrary", "arbitrary"), vmem_limit_bytes=VMEM_LIMIT),
        name="ffn_up",
    )(hb, w_ffn_in, w_ffn_in, w_ffn_out)


DOWN_TILE = 256
N_DOWN_PROMPT = SEQ // DOWN_TILE
N_DOWN_TILES = ROWS // DOWN_TILE


def _ffn_down_kernel(act_ref, h_ref, wd_ref, lg_ref, lb_ref, yp_ref, yt_ref):
    i = pl.program_id(0)

    def rows_out(o_ref):
        for r in range(0, DOWN_TILE, SUB_ROWS):
            rows = slice(r, r + SUB_ROWS)
            f = jnp.dot(act_ref[rows, :], wd_ref[...], preferred_element_type=jnp.float32)
            o_ref[rows, :] = _layer_norm(ALPHA * h_ref[rows, :] + f, lg_ref[...], lb_ref[...])

    @pl.when(i < N_DOWN_PROMPT)
    def _prompt():
        rows_out(yp_ref)

    @pl.when(i >= N_DOWN_PROMPT)
    def _tail():
        rows_out(yt_ref)


def _ffn_down(act, h, w_down, ln_g, ln_b):
    row = lambda i: (i, 0)
    full = lambda i: (0, 0)
    return pl.pallas_call(
        _ffn_down_kernel,
        grid=(N_DOWN_TILES,),
        in_specs=[
            pl.BlockSpec((DOWN_TILE, D_FF), row),
            pl.BlockSpec((DOWN_TILE, D_MODEL), row),
            pl.BlockSpec((D_FF, D_MODEL), full, pipeline_mode=pl.Buffered(1)),
            pl.BlockSpec((1, D_MODEL), full),
            pl.BlockSpec((1, D_MODEL), full),
        ],
        out_specs=[
            pl.BlockSpec((None, DOWN_TILE, D_MODEL), lambda i: (0, jnp.minimum(i, N_DOWN_PROMPT - 1), 0)),
            pl.BlockSpec((DOWN_TILE, D_MODEL), lambda i: (jnp.maximum(i - N_DOWN_PROMPT, 0), 0)),
        ],
        out_shape=[
            jax.ShapeDtypeStruct((1, SEQ, D_MODEL), jnp.float32),
            jax.ShapeDtypeStruct((ROW_TILE, D_MODEL), jnp.float32),
        ],
        compiler_params=pltpu.CompilerParams(
            dimension_semantics=("arbitrary",), vmem_limit_bytes=VMEM_LIMIT),
        name="ffn_down_ln",
    )(act, h, w_down, ln_g, ln_b)


def kernel(x_prompt, x_sample, state_gla, cache_conv, meta_tokens, ln_in_g, ln_in_b, w_in, w_gate_up, b_gate, gla_norm, conv_w, w_out, ln1_g, ln1_b, w_ffn_in, w_ffn_out, ln2_g, ln2_b):
    f32, bf16 = jnp.float32, jnp.bfloat16
    tail = jnp.concatenate([
        jnp.zeros((META_ROW0, D_MODEL), f32),
        meta_tokens.astype(f32),
        x_sample.reshape(SAMPLE_ROWS, D_MODEL),
        jnp.zeros((ROW_TILE - SAMPLE_ROW0 - SAMPLE_ROWS, D_MODEL), f32),
    ], axis=0)
    w_gu = jnp.pad(w_gate_up[0], ((0, LANE - GATE_RANK), (0, 0))).astype(bf16)
    vec = lambda p: p.reshape(1, -1).astype(f32)

    w_in_t = jnp.swapaxes(w_in, 1, 2)
    xn, xb, g = _ln_gate(x_prompt, tail, vec(ln_in_g), vec(ln_in_b), w_in_t, w_gu, vec(b_gate[0]))
    ya_tail, s_meta, s_sample = _gla_tail(xb, w_in_t, g, vec(gla_norm[0]), state_gla[0])
    ya_prompt, s_prompt, w_mix_bf, w_out_bf = _gla_prompt(xb, w_in_t, g, vec(gla_norm[0]), s_meta, w_out)
    m, c_tail, c_last = _mix(xb, ya_prompt, ya_tail, w_mix_bf, conv_w[0],
                             cache_conv[0].reshape(2 * DEC_BATCH, D_MODEL))
    h, hb = _out_proj(m, xn, w_out_bf, vec(ln1_g[0]), vec(ln1_b[0]))
    act, w_down_bf = _ffn_up(hb, w_ffn_in, w_ffn_out)
    y_prompt, y_tail = _ffn_down(act, h, w_down_bf, vec(ln2_g[0]), vec(ln2_b[0]))

    y_sample = y_tail[SAMPLE_ROW0:SAMPLE_ROW0 + SAMPLE_ROWS].reshape(DEC_BATCH, DEC_SEQ, D_MODEL)
    cache_prompt = c_last[6:8][None, None]
    cache_sample = c_tail[SAMPLE_ROW0:SAMPLE_ROW0 + SAMPLE_ROWS].reshape(
        DEC_BATCH, DEC_SEQ, D_MODEL)[:, DEC_SEQ - 2:][None]
    return (y_prompt, y_sample, s_prompt[None, None], cache_prompt, s_sample[None], cache_sample)
```

```python
import functools

import jax
import jax.numpy as jnp
from jax import lax
from jax.experimental import pallas as pl
from jax.experimental.pallas import tpu as pltpu

D_MODEL = 2048
SEQ = 8192
DEC_BATCH = 16
DEC_SEQ = 16
CHUNK = 64
N_META = 16
GLA_HEADS = 4
HEAD_K = 256
HEAD_V = 512
GLA_DK = GLA_HEADS * HEAD_K
GATE_RANK = 16
GATE_TEMP = 16.0
D_FF = 5632
ALPHA = 2.0 ** 0.25
LN_EPS = 1e-5
RMS_EPS = 1e-6

ROW_TILE = 512
N_PROMPT_TILES = SEQ // ROW_TILE
N_TILES = N_PROMPT_TILES + 1
ROWS = N_TILES * ROW_TILE
META_ROW0 = CHUNK - N_META
SAMPLE_ROW0 = CHUNK
SAMPLE_ROWS = DEC_BATCH * DEC_SEQ
LANE = 128
GATE_COL0 = 2 * GLA_DK + 2 * D_MODEL
VMEM_LIMIT = 56 * 1024 * 1024

_NT = (((1,), (1,)), ((), ()))
_TN = (((0,), (0,)), ((), ()))


def _tile_first_tail(i):
    return (i + N_PROMPT_TILES) % N_TILES


def _layer_norm(x, g, b):
    mu = jnp.mean(x, axis=-1, keepdims=True)
    xc = x - mu
    var = jnp.mean(xc * xc, axis=-1, keepdims=True)
    return xc * lax.rsqrt(var + LN_EPS) * g + b


def _log_sigmoid(x):
    return jnp.minimum(x, 0.0) - jnp.log1p(jnp.exp(-jnp.abs(x)))


def _sigmoid(x):
    return 1.0 / (1.0 + jnp.exp(-x))


def _ln_gate_kernel(xp_ref, xt_ref, lg_ref, lb_ref, wa_ref, wgu_ref, bg_ref, xn_ref, xb_ref, g_ref):
    x = jnp.where(pl.program_id(0) < N_PROMPT_TILES, xp_ref[...], xt_ref[...])
    xn = _layer_norm(x, lg_ref[...], lb_ref[...])
    xn_ref[...] = xn
    xb = xn.astype(jnp.bfloat16)
    xb_ref[...] = xb
    a = lax.dot_general(xb, wa_ref[...].astype(jnp.bfloat16), _NT, preferred_element_type=jnp.float32)
    z = jnp.dot(a.astype(jnp.bfloat16), wgu_ref[...], preferred_element_type=jnp.float32)
    g_ref[...] = _log_sigmoid(z + bg_ref[...]) * (1.0 / GATE_TEMP)


def _ln_gate(x_prompt, x_tail, ln_g, ln_b, w_in_t, w_gu, b_gate):
    row = lambda i: (i, 0)
    full = lambda i: (0, 0)
    return pl.pallas_call(
        _ln_gate_kernel,
        grid=(N_TILES,),
        in_specs=[
            pl.BlockSpec((None, ROW_TILE, D_MODEL), lambda i: (0, jnp.minimum(i, N_PROMPT_TILES - 1), 0)),
            pl.BlockSpec((ROW_TILE, D_MODEL), full),
            pl.BlockSpec((1, D_MODEL), full),
            pl.BlockSpec((1, D_MODEL), full),
            pl.BlockSpec((None, LANE, D_MODEL), lambda i: (0, GATE_COL0 // LANE, 0)),
            pl.BlockSpec((LANE, GLA_DK), full),
            pl.BlockSpec((1, GLA_DK), full),
        ],
        out_specs=[
            pl.BlockSpec((ROW_TILE, D_MODEL), row),
            pl.BlockSpec((ROW_TILE, D_MODEL), row),
            pl.BlockSpec((ROW_TILE, GLA_DK), row),
        ],
        out_shape=[
            jax.ShapeDtypeStruct((ROWS, D_MODEL), jnp.float32),
            jax.ShapeDtypeStruct((ROWS, D_MODEL), jnp.bfloat16),
            jax.ShapeDtypeStruct((ROWS, GLA_DK), jnp.float32),
        ],
        compiler_params=pltpu.CompilerParams(
            dimension_semantics=("arbitrary",), vmem_limit_bytes=VMEM_LIMIT),
        name="ln_gate",
    )(x_prompt, x_tail, ln_g, ln_b, w_in_t, w_gu, b_gate)


W_SLAB = 256
BIG_ROW_TILE = 1088
N_BIG_TILES = ROWS // BIG_ROW_TILE


PROMPT_CHUNK = 256
HEAD_ROWS = 2 * HEAD_K + 2 * HEAD_V


def _cast_head_weights(w_refs, wbf):
    off = 0
    for w_ref in w_refs:
        for r in range(0, w_ref.shape[0], W_SLAB):
            wbf[off + r:off + r + W_SLAB, :] = w_ref[r:r + W_SLAB, :].astype(jnp.bfloat16)
        off += w_ref.shape[0]


def _project_head(x, wbf):
    qk = lax.dot_general(x, wbf[0:2 * HEAD_K, :], _NT, preferred_element_type=jnp.float32)
    vr = lax.dot_general(x, wbf[2 * HEAD_K:, :], _NT, preferred_element_type=jnp.float32)
    return qk[:, :HEAD_K], qk[:, HEAD_K:], vr[:, :HEAD_V], vr[:, HEAD_V:]


def _head_weight_specs(**kw):
    kq = GLA_DK // HEAD_K
    return [
        pl.BlockSpec((None, HEAD_K, D_MODEL), lambda h, *_: (0, h, 0), **kw),
        pl.BlockSpec((None, HEAD_K, D_MODEL), lambda h, *_: (0, kq + h, 0), **kw),
        pl.BlockSpec((None, HEAD_V, D_MODEL), lambda h, *_: (0, kq + h, 0), **kw),
        pl.BlockSpec((None, HEAD_V, D_MODEL), lambda h, *_: (0, 2 * kq + h, 0), **kw),
    ]


def _row_of_block(b, block, r):
    n, w = b.shape
    b3 = b.reshape(n // block, block, w)
    return jnp.broadcast_to(b3[:, r:r + 1, :], b3.shape).reshape(n, w)


def _half_sizes(chunk):
    return tuple(chunk >> (j + 1) for j in range(chunk.bit_length() - 1))


def _gla_prepare(q, k, g, chunk):
    n, w = q.shape
    row = lax.broadcasted_iota(jnp.int32, (n, w), 0)
    pos = row & (chunk - 1)
    b = g
    for sh in reversed(_half_sizes(chunk)):
        b = b + jnp.where(pos >= sh, pltpu.roll(b, sh, 0), 0.0)
    qs = q * (HEAD_K ** -0.5)
    b_last = _row_of_block(b, chunk, chunk - 1)
    q_inter = (qs * jnp.exp(b)).astype(jnp.bfloat16)
    k_dec = (k * jnp.exp(b_last - b)).astype(jnp.bfloat16)
    levels = []
    for s in _half_sizes(chunk):
        upper = (row & (2 * s - 1)) >= s
        if s >= 4:
            b_mid = _row_of_block(b, 2 * s, s - 1)
            d = jnp.where(upper, b - b_mid, b_mid - b)
        elif s == 2:
            p4 = row & 3
            g_prev = pltpu.roll(g, 1, 0)
            g_next = pltpu.roll(g, n - 1, 0)
            d = jnp.where(p4 == 0, g_next, jnp.where(p4 == 1, 0.0, jnp.where(p4 == 2, g, g + g_prev)))
        else:
            d = jnp.where(upper, g, 0.0)
        x = (jnp.where(upper, qs, k) * jnp.exp(d)).astype(jnp.bfloat16)
        levels.append((x, x))
    levels.append((qs.astype(jnp.bfloat16), k.astype(jnp.bfloat16)))
    return q_inter, k_dec, b, levels


def _score_masks(chunk):
    ii = lax.broadcasted_iota(jnp.int32, (chunk, chunk), 0)
    jj = lax.broadcasted_iota(jnp.int32, (chunk, chunk), 1)
    masks = []
    for s in _half_sizes(chunk):
        blk = 2 * s
        same = (ii & ~(blk - 1)) == (jj & ~(blk - 1))
        masks.append(same & ((ii & (blk - 1)) >= s) & ((jj & (blk - 1)) < s))
    masks.append(ii == jj)
    return masks


def _gla_chunk(c, chunk, q_inter, k_dec, b, levels, v_bf, masks, s_val):
    r0 = c * chunk
    sl = slice(r0, r0 + chunk)
    scores = jnp.zeros((chunk, chunk), jnp.float32)
    for (qe, ke), m in zip(levels, masks):
        p = lax.dot_general(qe[sl], ke[sl], _NT, preferred_element_type=jnp.float32)
        scores = jnp.where(m, p, scores)
    vc = v_bf[sl]
    o = jnp.dot(scores.astype(jnp.bfloat16), vc, preferred_element_type=jnp.float32)
    o = o + jnp.dot(q_inter[sl], s_val.astype(jnp.bfloat16), preferred_element_type=jnp.float32)
    eb = jnp.exp(b[r0 + chunk - 1:r0 + chunk, :])
    eb_t = jnp.transpose(jnp.broadcast_to(eb, (LANE, HEAD_K)))
    decay = jnp.concatenate([eb_t] * (HEAD_V // LANE), axis=1)
    ds = lax.dot_general(k_dec[sl], vc, _TN, preferred_element_type=jnp.float32)
    return o, decay * s_val + ds


def _gla_finish(o, r, gn):
    o = o * lax.rsqrt(jnp.mean(o * o, axis=-1, keepdims=True) + RMS_EPS)
    return o * gn * (r * _sigmoid(r))


def _gla_tail_kernel(x_ref, wq_ref, wk_ref, wv_ref, wr_ref, g_ref, gn_ref, sin_ref,
                     y_ref, sm_ref, ss_ref, wbf):
    _cast_head_weights((wq_ref, wk_ref, wv_ref, wr_ref), wbf)
    q, k, v, r = _project_head(x_ref[...], wbf)
    masks = _score_masks(CHUNK)

    def padded(x):
        parts = [x[0:CHUNK]]
        zeros = jnp.zeros((META_ROW0, x.shape[1]), x.dtype)
        for s in range(DEC_BATCH):
            parts += [zeros, x[SAMPLE_ROW0 + s * DEC_SEQ:SAMPLE_ROW0 + (s + 1) * DEC_SEQ]]
        return jnp.concatenate(parts, axis=0)

    n = (1 + DEC_BATCH) * CHUNK
    q = padded(q)
    k = padded(k)
    g = padded(g_ref[...])
    v = padded(v)
    live = (lax.broadcasted_iota(jnp.int32, (n, HEAD_K), 0) & (CHUNK - 1)) >= META_ROW0
    k = jnp.where(live, k, 0.0)
    g = jnp.where(live, g, 0.0)
    q_inter, k_dec, b, levels = _gla_prepare(q, k, g, CHUNK)
    v_bf = v.astype(jnp.bfloat16)
    outs = [jnp.zeros((META_ROW0, HEAD_V), jnp.float32)]
    o, s_meta = _gla_chunk(0, CHUNK, q_inter, k_dec, b, levels, v_bf, masks,
                           jnp.zeros((HEAD_K, HEAD_V), jnp.float32))
    sm_ref[...] = s_meta
    outs.append(o[META_ROW0:])
    for s in range(DEC_BATCH):
        o, s_new = _gla_chunk(1 + s, CHUNK, q_inter, k_dec, b, levels, v_bf, masks, sin_ref[s])
        ss_ref[s] = s_new
        outs.append(o[META_ROW0:])
    outs.append(jnp.zeros((ROW_TILE - SAMPLE_ROW0 - SAMPLE_ROWS, HEAD_V), jnp.float32))
    y_ref[...] = _gla_finish(jnp.concatenate(outs, axis=0), r, gn_ref[...])


def _gla_tail(xb, w_in_t, g, gla_norm, state_gla):
    once = dict(pipeline_mode=pl.Buffered(1))
    return pl.pallas_call(
        _gla_tail_kernel,
        grid=(GLA_HEADS,),
        in_specs=[
            pl.BlockSpec((ROW_TILE, D_MODEL), lambda h: (N_PROMPT_TILES, 0)),
            *_head_weight_specs(**once),
            pl.BlockSpec((ROW_TILE, HEAD_K), lambda h: (N_PROMPT_TILES, h)),
            pl.BlockSpec((1, HEAD_V), lambda h: (0, h)),
            pl.BlockSpec((DEC_BATCH, None, HEAD_K, HEAD_V), lambda h: (0, h, 0, 0), **once),
        ],
        out_specs=[
            pl.BlockSpec((ROW_TILE, HEAD_V), lambda h: (0, h)),
            pl.BlockSpec((None, HEAD_K, HEAD_V), lambda h: (h, 0, 0)),
            pl.BlockSpec((DEC_BATCH, None, HEAD_K, HEAD_V), lambda h: (0, h, 0, 0)),
        ],
        out_shape=[
            jax.ShapeDtypeStruct((ROW_TILE, D_MODEL), jnp.float32),
            jax.ShapeDtypeStruct((GLA_HEADS, HEAD_K, HEAD_V), jnp.float32),
            jax.ShapeDtypeStruct((DEC_BATCH, GLA_HEADS, HEAD_K, HEAD_V), jnp.float32),
        ],
        scratch_shapes=[pltpu.VMEM((HEAD_ROWS, D_MODEL), jnp.bfloat16)],
        compiler_params=pltpu.CompilerParams(
            dimension_semantics=("arbitrary",), vmem_limit_bytes=VMEM_LIMIT),
        name="gla_tail",
    )(xb, *([w_in_t] * 4), g, gla_norm, state_gla)


GLA_STEP_ROWS = ROW_TILE
N_GLA_ROW_STEPS = SEQ // GLA_STEP_ROWS
N_GLA_STEPS = GLA_HEADS * N_GLA_ROW_STEPS
MIX_W_ROWS = 5 * D_MODEL
MIX_W_SLAB = MIX_W_ROWS // N_GLA_STEPS
OUT_W_SLAB = D_MODEL // N_GLA_STEPS


def _gla_prompt_kernel(x_ref, wq_ref, wk_ref, wv_ref, wr_ref, g_ref, gn_ref, s0_ref, wmix_ref, wout_ref,
                       y_ref, sp_ref, wmix_bf_ref, wout_bf_ref, wbf, s_scr):
    i = pl.program_id(1)

    wmix_bf_ref[...] = wmix_ref[0].astype(jnp.bfloat16)
    wout_bf_ref[...] = wout_ref[...].astype(jnp.bfloat16)

    @pl.when(i == 0)
    def _start_head():
        _cast_head_weights((wq_ref, wk_ref, wv_ref, wr_ref), wbf)
        s_scr[...] = s0_ref[...]

    masks = _score_masks(PROMPT_CHUNK)
    for r0 in range(0, GLA_STEP_ROWS, ROW_TILE):
        rows = slice(r0, r0 + ROW_TILE)
        q, k, v, r = _project_head(x_ref[rows, :], wbf)
        q_inter, k_dec, b, levels = _gla_prepare(q, k, g_ref[rows, :], PROMPT_CHUNK)
        v_bf = v.astype(jnp.bfloat16)
        outs = []
        for c in range(ROW_TILE // PROMPT_CHUNK):
            o, s_new = _gla_chunk(c, PROMPT_CHUNK, q_inter, k_dec, b, levels, v_bf, masks, s_scr[...])
            s_scr[...] = s_new
            outs.append(o)
        y_ref[rows, :] = _gla_finish(jnp.concatenate(outs, axis=0), r, gn_ref[...])

    @pl.when(i == N_GLA_ROW_STEPS - 1)
    def _final_state():
        sp_ref[...] = s_scr[...]


def _gla_prompt(xb, w_in_t, g, gla_norm, s_meta, w_out):
    step = lambda h, i: h * N_GLA_ROW_STEPS + i
    return pl.pallas_call(
        _gla_prompt_kernel,
        grid=(GLA_HEADS, N_GLA_ROW_STEPS),
        in_specs=[
            pl.BlockSpec((GLA_STEP_ROWS, D_MODEL), lambda h, i: (i, 0)),
            *_head_weight_specs(),
            pl.BlockSpec((GLA_STEP_ROWS, HEAD_K), lambda h, i: (i, h)),
            pl.BlockSpec((1, HEAD_V), lambda h, i: (0, h)),
            pl.BlockSpec((None, HEAD_K, HEAD_V), lambda h, i: (h, 0, 0)),
            pl.BlockSpec((pl.Element(1), pl.Element(MIX_W_SLAB), pl.Element(D_MODEL)),
                         lambda h, i: (0, pl.multiple_of(GATE_COL0 + GATE_RANK + step(h, i) * MIX_W_SLAB,
                                                         GATE_RANK), 0)),
            pl.BlockSpec((None, OUT_W_SLAB, D_MODEL), lambda h, i: (0, step(h, i), 0)),
        ],
        out_specs=[
            pl.BlockSpec((GLA_STEP_ROWS, HEAD_V), lambda h, i: (i, h)),
            pl.BlockSpec((None, HEAD_K, HEAD_V), lambda h, i: (h, 0, 0)),
            pl.BlockSpec((MIX_W_SLAB, D_MODEL), lambda h, i: (step(h, i), 0)),
            pl.BlockSpec((OUT_W_SLAB, D_MODEL), lambda h, i: (step(h, i), 0)),
        ],
        out_shape=[
            jax.ShapeDtypeStruct((SEQ, D_MODEL), jnp.float32),
            jax.ShapeDtypeStruct((GLA_HEADS, HEAD_K, HEAD_V), jnp.float32),
            jax.ShapeDtypeStruct((MIX_W_ROWS, D_MODEL), jnp.bfloat16),
            jax.ShapeDtypeStruct((D_MODEL, D_MODEL), jnp.bfloat16),
        ],
        scratch_shapes=[pltpu.VMEM((HEAD_ROWS, D_MODEL), jnp.bfloat16),
                        pltpu.VMEM((HEAD_K, HEAD_V), jnp.float32)],
        compiler_params=pltpu.CompilerParams(
            dimension_semantics=("arbitrary", "arbitrary"), vmem_limit_bytes=VMEM_LIMIT),
        name="gla_prompt",
    )(xb, *([w_in_t] * 4), g, gla_norm, s_meta, w_in_t, w_out)


MIX_TN = 512
N_MIX_GROUPS = 5


MIX_STEP_ROWS = 2 * ROW_TILE
N_MIX_ROW_STEPS = SEQ // MIX_STEP_ROWS


def _mix_products(x, w_refs):
    return [lax.dot_general(x, w_ref[...], _NT, preferred_element_type=jnp.float32) for w_ref in w_refs]


def _mix_finish(c, p1, p2, cb, ga, gb, ya, cw_ref):
    conv = cw_ref[0:1, :] * p2 + cw_ref[1:2, :] * p1 + cw_ref[2:3, :] * c
    return (_sigmoid(ga) * ya + _sigmoid(gb) * (cb * conv)).astype(jnp.bfloat16)


def _mix_tail_kernel(x_ref, ya_ref, wcb_ref, wcc_ref, wcx_ref, wga_ref, wgb_ref, cw_ref, cache_ref,
                     m_ref, ctail_ref):
    cb, cc, cx, ga, gb = _mix_products(x_ref[...], (wcb_ref, wcc_ref, wcx_ref, wga_ref, wgb_ref))
    c = cc * cx
    row = lax.broadcasted_iota(jnp.int32, (ROW_TILE, MIX_TN), 0)
    prev1 = pltpu.roll(c, 1, 0)
    prev2 = pltpu.roll(c, 2, 0)

    def stream_rows(j):
        parts = [jnp.zeros((SAMPLE_ROW0, MIX_TN), jnp.float32)]
        for s in range(DEC_BATCH):
            parts.append(jnp.broadcast_to(cache_ref[2 * s + j:2 * s + j + 1, :], (DEC_SEQ, MIX_TN)))
        parts.append(jnp.zeros((ROW_TILE - SAMPLE_ROW0 - SAMPLE_ROWS, MIX_TN), jnp.float32))
        return jnp.concatenate(parts, axis=0)

    old = stream_rows(0)
    new = stream_rows(1)
    in_sample = (row >= SAMPLE_ROW0) & (row < SAMPLE_ROW0 + SAMPLE_ROWS)
    p16 = (row - SAMPLE_ROW0) & (DEC_SEQ - 1)
    first = in_sample & (p16 == 0)
    second = in_sample & (p16 == 1)
    p1 = jnp.where(row == META_ROW0, 0.0, jnp.where(first, new, prev1))
    p2 = jnp.where((row == META_ROW0) | (row == META_ROW0 + 1), 0.0,
                   jnp.where(first, old, jnp.where(second, new, prev2)))
    m_ref[...] = _mix_finish(c, p1, p2, cb, ga, gb, ya_ref[...], cw_ref)
    ctail_ref[...] = c


def _mix_prompt_kernel(x_ref, ya_ref, wcb_ref, wcc_ref, wcx_ref, wga_ref, wgb_ref, cw_ref, c0_ref,
                       m_ref, clast_ref, carry):
    i = pl.program_id(1)

    @pl.when(i == 0)
    def _start():
        carry[...] = c0_ref[...]

    hist = carry[...]
    row = lax.broadcasted_iota(jnp.int32, (ROW_TILE, MIX_TN), 0)
    for r0 in range(0, MIX_STEP_ROWS, ROW_TILE):
        rows = slice(r0, r0 + ROW_TILE)
        cb, cc, cx, ga, gb = _mix_products(x_ref[rows, :], (wcb_ref, wcc_ref, wcx_ref, wga_ref, wgb_ref))
        c = cc * cx
        last = jnp.broadcast_to(hist[7:8], c.shape)
        p1 = jnp.where(row == 0, last, pltpu.roll(c, 1, 0))
        p2 = jnp.where(row == 0, jnp.broadcast_to(hist[6:7], c.shape),
                       jnp.where(row == 1, last, pltpu.roll(c, 2, 0)))
        m_ref[rows, :] = _mix_finish(c, p1, p2, cb, ga, gb, ya_ref[rows, :], cw_ref)
        hist = c[ROW_TILE - 8:]
    carry[...] = hist

    @pl.when(i == N_MIX_ROW_STEPS - 1)
    def _last():
        clast_ref[...] = hist


def _mix_weight_specs(index_args):
    nc = D_MODEL // MIX_TN
    return [pl.BlockSpec((MIX_TN, D_MODEL), functools.partial(index_args, grp * nc))
            for grp in range(N_MIX_GROUPS)]


def _mix_tail(xb, ya_tail, w_mix_bf, conv_w, cache_rows):
    top = lambda cj: (0, cj)
    return pl.pallas_call(
        _mix_tail_kernel,
        grid=(D_MODEL // MIX_TN,),
        in_specs=[
            pl.BlockSpec((ROW_TILE, D_MODEL), lambda cj: (N_PROMPT_TILES, 0)),
            pl.BlockSpec((ROW_TILE, MIX_TN), top),
            *_mix_weight_specs(lambda base, cj: (base + cj, 0)),
            pl.BlockSpec((3, MIX_TN), top),
            pl.BlockSpec((2 * DEC_BATCH, MIX_TN), top),
        ],
        out_specs=[pl.BlockSpec((ROW_TILE, MIX_TN), top), pl.BlockSpec((ROW_TILE, MIX_TN), top)],
        out_shape=[
            jax.ShapeDtypeStruct((ROW_TILE, D_MODEL), jnp.bfloat16),
            jax.ShapeDtypeStruct((ROW_TILE, D_MODEL), jnp.float32),
        ],
        compiler_params=pltpu.CompilerParams(
            dimension_semantics=("arbitrary",), vmem_limit_bytes=VMEM_LIMIT),
        name="proj_conv_mix_tail",
    )(xb, ya_tail, *([w_mix_bf] * N_MIX_GROUPS), conv_w, cache_rows)


def _mix_prompt(xb, ya_prompt, w_mix_bf, conv_w, c_tail):
    top = lambda cj, i: (0, cj)
    return pl.pallas_call(
        _mix_prompt_kernel,
        grid=(D_MODEL // MIX_TN, N_MIX_ROW_STEPS),
        in_specs=[
            pl.BlockSpec((MIX_STEP_ROWS, D_MODEL), lambda cj, i: (i, 0)),
            pl.BlockSpec((MIX_STEP_ROWS, MIX_TN), lambda cj, i: (i, cj)),
            *_mix_weight_specs(lambda base, cj, i: (base + cj, 0)),
            pl.BlockSpec((3, MIX_TN), top),
            pl.BlockSpec((8, MIX_TN), lambda cj, i: (CHUNK // 8 - 1, cj)),
        ],
        out_specs=[pl.BlockSpec((MIX_STEP_ROWS, MIX_TN), lambda cj, i: (i, cj)), pl.BlockSpec((8, MIX_TN), top)],
        out_shape=[
            jax.ShapeDtypeStruct((SEQ, D_MODEL), jnp.bfloat16),
            jax.ShapeDtypeStruct((8, D_MODEL), jnp.float32),
        ],
        scratch_shapes=[pltpu.VMEM((8, MIX_TN), jnp.float32)],
        compiler_params=pltpu.CompilerParams(
            dimension_semantics=("arbitrary", "arbitrary"), vmem_limit_bytes=VMEM_LIMIT),
        name="proj_conv_mix",
    )(xb, ya_prompt, *([w_mix_bf] * N_MIX_GROUPS), conv_w, c_tail)


SUB_ROWS = 128


def _out_proj_kernel(mp_ref, mt_ref, xn_ref, w_ref, lg_ref, lb_ref, h_ref, hb_ref):
    is_prompt = pl.program_id(0) < N_PROMPT_TILES
    for r in range(0, ROW_TILE, SUB_ROWS):
        rows = slice(r, r + SUB_ROWS)
        m = jnp.where(is_prompt, mp_ref[rows, :], mt_ref[rows, :])
        mix = jnp.dot(m, w_ref[...], preferred_element_type=jnp.float32)
        h = _layer_norm(ALPHA * xn_ref[rows, :] + mix, lg_ref[...], lb_ref[...])
        h_ref[rows, :] = h
        hb_ref[rows, :] = h.astype(jnp.bfloat16)


def _out_proj(m_prompt, m_tail, xn, w_out, ln_g, ln_b):
    row = lambda i: (i, 0)
    full = lambda i: (0, 0)
    return pl.pallas_call(
        _out_proj_kernel,
        grid=(N_TILES,),
        in_specs=[
            pl.BlockSpec((ROW_TILE, D_MODEL), lambda i: (jnp.minimum(i, N_PROMPT_TILES - 1), 0)),
            pl.BlockSpec((ROW_TILE, D_MODEL), full),
            pl.BlockSpec((ROW_TILE, D_MODEL), row),
            pl.BlockSpec((D_MODEL, D_MODEL), full),
            pl.BlockSpec((1, D_MODEL), full),
            pl.BlockSpec((1, D_MODEL), full),
        ],
        out_specs=[pl.BlockSpec((ROW_TILE, D_MODEL), row), pl.BlockSpec((ROW_TILE, D_MODEL), row)],
        out_shape=[
            jax.ShapeDtypeStruct((ROWS, D_MODEL), jnp.float32),
            jax.ShapeDtypeStruct((ROWS, D_MODEL), jnp.bfloat16),
        ],
        compiler_params=pltpu.CompilerParams(
            dimension_semantics=("arbitrary",), vmem_limit_bytes=VMEM_LIMIT),
        name="out_proj_ln",
    )(m_prompt, m_tail, xn, w_out, ln_g, ln_b)


FF_TILE = 512
N_FF_TILES = D_FF // FF_TILE
FFN_UP_ROWS = 2 * BIG_ROW_TILE


N_FFN_UP_STEPS = N_FF_TILES * (ROWS // FFN_UP_ROWS)
DOWN_W_SLAB = D_FF // N_FFN_UP_STEPS


def _ffn_up_kernel(hb_ref, wg_ref, wu_ref, wd_ref, act_ref, wd_bf_ref, wg_bf, wu_bf):
    wd_bf_ref[...] = wd_ref[...].astype(jnp.bfloat16)

    @pl.when(pl.program_id(1) == 0)
    def _cast():
        for r in range(0, D_MODEL, W_SLAB):
            wg_bf[r:r + W_SLAB, :] = wg_ref[r:r + W_SLAB, :].astype(jnp.bfloat16)
            wu_bf[r:r + W_SLAB, :] = wu_ref[r:r + W_SLAB, :].astype(jnp.bfloat16)

    for r in range(0, FFN_UP_ROWS, BIG_ROW_TILE):
        rows = slice(r, r + BIG_ROW_TILE)
        hb = hb_ref[rows, :]
        gate = jnp.dot(hb, wg_bf[...], preferred_element_type=jnp.float32)
        up = jnp.dot(hb, wu_bf[...], preferred_element_type=jnp.float32)
        act_ref[rows, :] = (gate * _sigmoid(gate) * up).astype(jnp.bfloat16)


def _ffn_up(hb, w_ffn_in, w_ffn_out):
    step = lambda j, i: j * (ROWS // FFN_UP_ROWS) + i
    return pl.pallas_call(
        _ffn_up_kernel,
        grid=(N_FF_TILES, ROWS // FFN_UP_ROWS),
        in_specs=[
            pl.BlockSpec((FFN_UP_ROWS, D_MODEL), lambda j, i: (i, 0)),
            pl.BlockSpec((None, D_MODEL, FF_TILE), lambda j, i: (0, 0, j)),
            pl.BlockSpec((None, D_MODEL, FF_TILE), lambda j, i: (0, 0, N_FF_TILES + j)),
            pl.BlockSpec((None, DOWN_W_SLAB, D_MODEL), lambda j, i: (0, step(j, i), 0)),
        ],
        out_specs=[
            pl.BlockSpec((FFN_UP_ROWS, FF_TILE), lambda j, i: (i, j)),
            pl.BlockSpec((DOWN_W_SLAB, D_MODEL), lambda j, i: (step(j, i), 0)),
        ],
        out_shape=[
            jax.ShapeDtypeStruct((ROWS, D_FF), jnp.bfloat16),
            jax.ShapeDtypeStruct((D_FF, D_MODEL), jnp.bfloat16),
        ],
        scratch_shapes=[pltpu.VMEM((D_MODEL, FF_TILE), jnp.bfloat16),
                        pltpu.VMEM((D_MODEL, FF_TILE), jnp.bfloat16)],
        compiler_params=pltpu.CompilerParams(
            dimension_semantics=("arbitrary", "arbitrary"), vmem_limit_bytes=VMEM_LIMIT),
        name="ffn_up",
    )(hb, w_ffn_in, w_ffn_in, w_ffn_out)


DOWN_TILE = 256
N_DOWN_PROMPT = SEQ // DOWN_TILE
N_DOWN_TILES = ROWS // DOWN_TILE


def _ffn_down_kernel(act_ref, h_ref, wd_ref, lg_ref, lb_ref, yp_ref, yt_ref):
    i = pl.program_id(0)

    def rows_out(o_ref):
        for r in range(0, DOWN_TILE, SUB_ROWS):
            rows = slice(r, r + SUB_ROWS)
            f = jnp.dot(act_ref[rows, :], wd_ref[...], preferred_element_type=jnp.float32)
            o_ref[rows, :] = _layer_norm(ALPHA * h_ref[rows, :] + f, lg_ref[...], lb_ref[...])

    @pl.when(i < N_DOWN_PROMPT)
    def _prompt():
        rows_out(yp_ref)

    @pl.when(i >= N_DOWN_PROMPT)
    def _tail():
        rows_out(yt_ref)


def _ffn_down(act, h, w_down, ln_g, ln_b):
    row = lambda i: (i, 0)
    full = lambda i: (0, 0)
    return pl.pallas_call(
        _ffn_down_kernel,
        grid=(N_DOWN_TILES,),
        in_specs=[
            pl.BlockSpec((DOWN_TILE, D_FF), row),
            pl.BlockSpec((DOWN_TILE, D_MODEL), row),
            pl.BlockSpec((D_FF, D_MODEL), full, pipeline_mode=pl.Buffered(1)),
            pl.BlockSpec((1, D_MODEL), full),
            pl.BlockSpec((1, D_MODEL), full),
        ],
        out_specs=[
            pl.BlockSpec((None, DOWN_TILE, D_MODEL), lambda i: (0, jnp.minimum(i, N_DOWN_PROMPT - 1), 0)),
            pl.BlockSpec((DOWN_TILE, D_MODEL), lambda i: (jnp.maximum(i - N_DOWN_PROMPT, 0), 0)),
        ],
        out_shape=[
            jax.ShapeDtypeStruct((1, SEQ, D_MODEL), jnp.float32),
            jax.ShapeDtypeStruct((ROW_TILE, D_MODEL), jnp.float32),
        ],
        compiler_params=pltpu.CompilerParams(
            dimension_semantics=("arbitrary",), vmem_limit_bytes=VMEM_LIMIT),
        name="ffn_down_ln",
    )(act, h, w_down, ln_g, ln_b)


def kernel(x_prompt, x_sample, state_gla, cache_conv, meta_tokens, ln_in_g, ln_in_b, w_in, w_gate_up, b_gate, gla_norm, conv_w, w_out, ln1_g, ln1_b, w_ffn_in, w_ffn_out, ln2_g, ln2_b):
    f32, bf16 = jnp.float32, jnp.bfloat16
    tail = jnp.concatenate([
        jnp.zeros((META_ROW0, D_MODEL), f32),
        meta_tokens.astype(f32),
        x_sample.reshape(SAMPLE_ROWS, D_MODEL),
        jnp.zeros((ROW_TILE - SAMPLE_ROW0 - SAMPLE_ROWS, D_MODEL), f32),
    ], axis=0)
    w_gu = jnp.pad(w_gate_up[0], ((0, LANE - GATE_RANK), (0, 0))).astype(bf16)
    vec = lambda p: p.reshape(1, -1).astype(f32)

    w_in_t = jnp.swapaxes(w_in, 1, 2)
    xn, xb, g = _ln_gate(x_prompt, tail, vec(ln_in_g), vec(ln_in_b), w_in_t, w_gu, vec(b_gate[0]))
    ya_tail, s_meta, s_sample = _gla_tail(xb, w_in_t, g, vec(gla_norm[0]), state_gla[0])
    ya_prompt, s_prompt, w_mix_bf, w_out_bf = _gla_prompt(xb, w_in_t, g, vec(gla_norm[0]), s_meta, w_out)
    m_tail, c_tail = _mix_tail(xb, ya_tail, w_mix_bf, conv_w[0], cache_conv[0].reshape(2 * DEC_BATCH, D_MODEL))
    m_prompt, c_last = _mix_prompt(xb, ya_prompt, w_mix_bf, conv_w[0], c_tail)
    h, hb = _out_proj(m_prompt, m_tail, xn, w_out_bf, vec(ln1_g[0]), vec(ln1_b[0]))
    act, w_down_bf = _ffn_up(hb, w_ffn_in, w_ffn_out)
    y_prompt, y_tail = _ffn_down(act, h, w_down_bf, vec(ln2_g[0]), vec(ln2_b[0]))

    y_sample = y_tail[SAMPLE_ROW0:SAMPLE_ROW0 + SAMPLE_ROWS].reshape(DEC_BATCH, DEC_SEQ, D_MODEL)
    cache_prompt = c_last[6:8][None, None]
    cache_sample = c_tail[SAMPLE_ROW0:SAMPLE_ROW0 + SAMPLE_ROWS].reshape(
        DEC_BATCH, DEC_SEQ, D_MODEL)[:, DEC_SEQ - 2:][None]
    return (y_prompt, y_sample, s_prompt[None, None], cache_prompt, s_sample[None], cache_sample)
```

```python
import functools

import jax
import jax.numpy as jnp
from jax import lax
from jax.experimental import pallas as pl
from jax.experimental.pallas import tpu as pltpu

D_MODEL = 2048
SEQ = 8192
DEC_BATCH = 16
DEC_SEQ = 16
CHUNK = 64
N_META = 16
GLA_HEADS = 4
HEAD_K = 256
HEAD_V = 512
GLA_DK = GLA_HEADS * HEAD_K
GATE_RANK = 16
GATE_TEMP = 16.0
D_FF = 5632
ALPHA = 2.0 ** 0.25
LN_EPS = 1e-5
RMS_EPS = 1e-6

ROW_TILE = 512
N_PROMPT_TILES = SEQ // ROW_TILE
N_TILES = N_PROMPT_TILES + 1
ROWS = N_TILES * ROW_TILE
META_ROW0 = CHUNK - N_META
SAMPLE_ROW0 = CHUNK
SAMPLE_ROWS = DEC_BATCH * DEC_SEQ
LANE = 128
GATE_COL0 = 2 * GLA_DK + 2 * D_MODEL
VMEM_LIMIT = 56 * 1024 * 1024

_NT = (((1,), (1,)), ((), ()))
_TN = (((0,), (0,)), ((), ()))


def _tile_first_tail(i):
    return (i + N_PROMPT_TILES) % N_TILES


def _layer_norm(x, g, b):
    mu = jnp.mean(x, axis=-1, keepdims=True)
    xc = x - mu
    var = jnp.mean(xc * xc, axis=-1, keepdims=True)
    return xc * lax.rsqrt(var + LN_EPS) * g + b


def _log_sigmoid(x):
    return jnp.minimum(x, 0.0) - jnp.log1p(jnp.exp(-jnp.abs(x)))


def _sigmoid(x):
    return 1.0 / (1.0 + jnp.exp(-x))


def _ln_gate_kernel(xp_ref, xt_ref, lg_ref, lb_ref, wa_ref, wgu_ref, bg_ref, xn_ref, xb_ref, g_ref):
    x = jnp.where(pl.program_id(0) < N_PROMPT_TILES, xp_ref[...], xt_ref[...])
    xn = _layer_norm(x, lg_ref[...], lb_ref[...])
    xn_ref[...] = xn
    xb = xn.astype(jnp.bfloat16)
    xb_ref[...] = xb
    a = lax.dot_general(xb, wa_ref[...].astype(jnp.bfloat16), _NT, preferred_element_type=jnp.float32)
    z = jnp.dot(a.astype(jnp.bfloat16), wgu_ref[...], preferred_element_type=jnp.float32)
    g_ref[...] = _log_sigmoid(z + bg_ref[...]) * (1.0 / GATE_TEMP)


def _ln_gate(x_prompt, x_tail, ln_g, ln_b, w_in_t, w_gu, b_gate):
    row = lambda i: (i, 0)
    full = lambda i: (0, 0)
    return pl.pallas_call(
        _ln_gate_kernel,
        grid=(N_TILES,),
        in_specs=[
            pl.BlockSpec((None, ROW_TILE, D_MODEL), lambda i: (0, jnp.minimum(i, N_PROMPT_TILES - 1), 0)),
            pl.BlockSpec((ROW_TILE, D_MODEL), full),
            pl.BlockSpec((1, D_MODEL), full),
            pl.BlockSpec((1, D_MODEL), full),
            pl.BlockSpec((None, LANE, D_MODEL), lambda i: (0, GATE_COL0 // LANE, 0)),
            pl.BlockSpec((LANE, GLA_DK), full),
            pl.BlockSpec((1, GLA_DK), full),
        ],
        out_specs=[
            pl.BlockSpec((ROW_TILE, D_MODEL), row),
            pl.BlockSpec((ROW_TILE, D_MODEL), row),
            pl.BlockSpec((ROW_TILE, GLA_DK), row),
        ],
        out_shape=[
            jax.ShapeDtypeStruct((ROWS, D_MODEL), jnp.float32),
            jax.ShapeDtypeStruct((ROWS, D_MODEL), jnp.bfloat16),
            jax.ShapeDtypeStruct((ROWS, GLA_DK), jnp.float32),
        ],
        compiler_params=pltpu.CompilerParams(
            dimension_semantics=("arbitrary",), vmem_limit_bytes=VMEM_LIMIT),
        name="ln_gate",
    )(x_prompt, x_tail, ln_g, ln_b, w_in_t, w_gu, b_gate)


W_SLAB = 256
BIG_ROW_TILE = 1088
N_BIG_TILES = ROWS // BIG_ROW_TILE


PROMPT_CHUNK = 256
HEAD_ROWS = 2 * HEAD_K + 2 * HEAD_V


def _cast_head_weights(w_refs, wbf):
    off = 0
    for w_ref in w_refs:
        for r in range(0, w_ref.shape[0], W_SLAB):
            wbf[off + r:off + r + W_SLAB, :] = w_ref[r:r + W_SLAB, :].astype(jnp.bfloat16)
        off += w_ref.shape[0]


def _project_head(x, wbf):
    qk = lax.dot_general(x, wbf[0:2 * HEAD_K, :], _NT, preferred_element_type=jnp.float32)
    vr = lax.dot_general(x, wbf[2 * HEAD_K:, :], _NT, preferred_element_type=jnp.float32)
    return qk[:, :HEAD_K], qk[:, HEAD_K:], vr[:, :HEAD_V], vr[:, HEAD_V:]


def _head_weight_specs(**kw):
    kq = GLA_DK // HEAD_K
    return [
        pl.BlockSpec((None, HEAD_K, D_MODEL), lambda h, *_: (0, h, 0), **kw),
        pl.BlockSpec((None, HEAD_K, D_MODEL), lambda h, *_: (0, kq + h, 0), **kw),
        pl.BlockSpec((None, HEAD_V, D_MODEL), lambda h, *_: (0, kq + h, 0), **kw),
        pl.BlockSpec((None, HEAD_V, D_MODEL), lambda h, *_: (0, 2 * kq + h, 0), **kw),
    ]


def _row_of_block(b, block, r):
    n, w = b.shape
    b3 = b.reshape(n // block, block, w)
    return jnp.broadcast_to(b3[:, r:r + 1, :], b3.shape).reshape(n, w)


def _half_sizes(chunk):
    return tuple(chunk >> (j + 1) for j in range(chunk.bit_length() - 1))


def _gla_prepare(q, k, g, chunk):
    n, w = q.shape
    row = lax.broadcasted_iota(jnp.int32, (n, w), 0)
    pos = row & (chunk - 1)
    b = g
    for sh in reversed(_half_sizes(chunk)):
        b = b + jnp.where(pos >= sh, pltpu.roll(b, sh, 0), 0.0)
    qs = q * (HEAD_K ** -0.5)
    b_last = _row_of_block(b, chunk, chunk - 1)
    q_inter = (qs * jnp.exp(b)).astype(jnp.bfloat16)
    k_dec = (k * jnp.exp(b_last - b)).astype(jnp.bfloat16)
    levels = []
    for s in _half_sizes(chunk):
        upper = (row & (2 * s - 1)) >= s
        if s >= 4:
            b_mid = _row_of_block(b, 2 * s, s - 1)
            d = jnp.where(upper, b - b_mid, b_mid - b)
        elif s == 2:
            p4 = row & 3
            g_prev = pltpu.roll(g, 1, 0)
            g_next = pltpu.roll(g, n - 1, 0)
            d = jnp.where(p4 == 0, g_next, jnp.where(p4 == 1, 0.0, jnp.where(p4 == 2, g, g + g_prev)))
        else:
            d = jnp.where(upper, g, 0.0)
        x = (jnp.where(upper, qs, k) * jnp.exp(d)).astype(jnp.bfloat16)
        levels.append((x, x))
    levels.append((qs.astype(jnp.bfloat16), k.astype(jnp.bfloat16)))
    return q_inter, k_dec, b, levels


def _score_masks(chunk):
    ii = lax.broadcasted_iota(jnp.int32, (chunk, chunk), 0)
    jj = lax.broadcasted_iota(jnp.int32, (chunk, chunk), 1)
    masks = []
    for s in _half_sizes(chunk):
        blk = 2 * s
        same = (ii & ~(blk - 1)) == (jj & ~(blk - 1))
        masks.append(same & ((ii & (blk - 1)) >= s) & ((jj & (blk - 1)) < s))
    masks.append(ii == jj)
    return masks


def _gla_chunk(c, chunk, q_inter, k_dec, b, levels, v_bf, masks, s_val):
    r0 = c * chunk
    sl = slice(r0, r0 + chunk)
    scores = jnp.zeros((chunk, chunk), jnp.float32)
    for (qe, ke), m in zip(levels, masks):
        p = lax.dot_general(qe[sl], ke[sl], _NT, preferred_element_type=jnp.float32)
        scores = jnp.where(m, p, scores)
    vc = v_bf[sl]
    o = jnp.dot(scores.astype(jnp.bfloat16), vc, preferred_element_type=jnp.float32)
    o = o + jnp.dot(q_inter[sl], s_val.astype(jnp.bfloat16), preferred_element_type=jnp.float32)
    eb = jnp.exp(b[r0 + chunk - 1:r0 + chunk, :])
    eb_t = jnp.transpose(jnp.broadcast_to(eb, (LANE, HEAD_K)))
    decay = jnp.concatenate([eb_t] * (HEAD_V // LANE), axis=1)
    ds = lax.dot_general(k_dec[sl], vc, _TN, preferred_element_type=jnp.float32)
    return o, decay * s_val + ds


def _gla_finish(o, r, gn):
    o = o * lax.rsqrt(jnp.mean(o * o, axis=-1, keepdims=True) + RMS_EPS)
    return o * gn * (r * _sigmoid(r))


TAIL_HALF = DEC_BATCH // 2
TAIL_SPLIT_ROW = SAMPLE_ROW0 + TAIL_HALF * DEC_SEQ


def _gla_tail_kernel(x_ref, wq_ref, wk_ref, wv_ref, wr_ref, g_ref, gn_ref, sin_ref,
                     y_ref, sm_ref, ss_ref, wbf, u_scr):
    half = pl.program_id(1)
    masks = _score_masks(CHUNK)
    kv0, r0 = 2 * HEAD_K, 2 * HEAD_K + HEAD_V

    @pl.when(half == 0)
    def _project():
        _cast_head_weights((wq_ref, wk_ref, wv_ref, wr_ref), wbf)
        q, k, v, r = _project_head(x_ref[...], wbf)
        u_scr[...] = jnp.concatenate([q, k, v, r], axis=1)

    def run(first_stream, with_meta, row_lo, row_hi):
        def padded(c0, width):
            parts = [u_scr[0:CHUNK, c0:c0 + width]] if with_meta else []
            zeros = jnp.zeros((META_ROW0, width), jnp.float32)
            for s in range(first_stream, first_stream + TAIL_HALF):
                lo = SAMPLE_ROW0 + s * DEC_SEQ
                parts += [zeros, u_scr[lo:lo + DEC_SEQ, c0:c0 + width]]
            return jnp.concatenate(parts, axis=0)

        def padded_g():
            parts = [g_ref[0:CHUNK, :]] if with_meta else []
            zeros = jnp.zeros((META_ROW0, HEAD_K), jnp.float32)
            for s in range(first_stream, first_stream + TAIL_HALF):
                lo = SAMPLE_ROW0 + s * DEC_SEQ
                parts += [zeros, g_ref[lo:lo + DEC_SEQ, :]]
            return jnp.concatenate(parts, axis=0)

        q = padded(0, HEAD_K)
        k = padded(HEAD_K, HEAD_K)
        v = padded(kv0, HEAD_V)
        g = padded_g()
        live = (lax.broadcasted_iota(jnp.int32, q.shape, 0) & (CHUNK - 1)) >= META_ROW0
        k = jnp.where(live, k, 0.0)
        g = jnp.where(live, g, 0.0)
        q_inter, k_dec, b, levels = _gla_prepare(q, k, g, CHUNK)
        v_bf = v.astype(jnp.bfloat16)
        outs, c = [], 0
        if with_meta:
            outs.append(jnp.zeros((META_ROW0, HEAD_V), jnp.float32))
            o, s_meta = _gla_chunk(0, CHUNK, q_inter, k_dec, b, levels, v_bf, masks,
                                   jnp.zeros((HEAD_K, HEAD_V), jnp.float32))
            sm_ref[...] = s_meta
            outs.append(o[META_ROW0:])
            c = 1
        for s in range(TAIL_HALF):
            o, s_new = _gla_chunk(c + s, CHUNK, q_inter, k_dec, b, levels, v_bf, masks, sin_ref[s])
            ss_ref[s] = s_new
            outs.append(o[META_ROW0:])
        if not with_meta:
            outs.append(jnp.zeros((ROW_TILE - SAMPLE_ROW0 - SAMPLE_ROWS, HEAD_V), jnp.float32))
        y_ref[row_lo:row_hi, :] = _gla_finish(
            jnp.concatenate(outs, axis=0), u_scr[row_lo:row_hi, r0:r0 + HEAD_V], gn_ref[...])

    @pl.when(half == 0)
    def _first():
        run(0, True, 0, TAIL_SPLIT_ROW)

    @pl.when(half == 1)
    def _second():
        run(TAIL_HALF, False, TAIL_SPLIT_ROW, ROW_TILE)


def _gla_tail(xb, w_in_t, g, gla_norm, state_gla):
    states = pl.BlockSpec((TAIL_HALF, None, HEAD_K, HEAD_V), lambda h, half: (half, h, 0, 0))
    return pl.pallas_call(
        _gla_tail_kernel,
        grid=(GLA_HEADS, 2),
        in_specs=[
            pl.BlockSpec((ROW_TILE, D_MODEL), lambda h, half: (N_PROMPT_TILES, 0)),
            *_head_weight_specs(),
            pl.BlockSpec((ROW_TILE, HEAD_K), lambda h, half: (N_PROMPT_TILES, h)),
            pl.BlockSpec((1, HEAD_V), lambda h, half: (0, h)),
            states,
        ],
        out_specs=[
            pl.BlockSpec((ROW_TILE, HEAD_V), lambda h, half: (0, h)),
            pl.BlockSpec((None, HEAD_K, HEAD_V), lambda h, half: (h, 0, 0)),
            states,
        ],
        out_shape=[
            jax.ShapeDtypeStruct((ROW_TILE, D_MODEL), jnp.float32),
            jax.ShapeDtypeStruct((GLA_HEADS, HEAD_K, HEAD_V), jnp.float32),
            jax.ShapeDtypeStruct((DEC_BATCH, GLA_HEADS, HEAD_K, HEAD_V), jnp.float32),
        ],
        scratch_shapes=[pltpu.VMEM((HEAD_ROWS, D_MODEL), jnp.bfloat16),
                        pltpu.VMEM((ROW_TILE, HEAD_ROWS), jnp.float32)],
        compiler_params=pltpu.CompilerParams(
            dimension_semantics=("arbitrary", "arbitrary"), vmem_limit_bytes=VMEM_LIMIT),
        name="gla_tail",
    )(xb, *([w_in_t] * 4), g, gla_norm, state_gla)


GLA_STEP_ROWS = ROW_TILE
N_GLA_ROW_STEPS = SEQ // GLA_STEP_ROWS
N_GLA_STEPS = GLA_HEADS * N_GLA_ROW_STEPS
MIX_W_ROWS = 5 * D_MODEL
MIX_W_SLAB = MIX_W_ROWS // N_GLA_STEPS
OUT_W_SLAB = D_MODEL // N_GLA_STEPS


def _gla_prompt_kernel(x_ref, wq_ref, wk_ref, wv_ref, wr_ref, g_ref, gn_ref, s0_ref, wmix_ref, wout_ref,
                       y_ref, sp_ref, wmix_bf_ref, wout_bf_ref, wbf, s_scr):
    i = pl.program_id(1)

    @pl.when(i == 0)
    def _start_head():
        _cast_head_weights((wq_ref, wk_ref, wv_ref, wr_ref), wbf)
        s_scr[...] = s0_ref[...]

    masks = _score_masks(PROMPT_CHUNK)
    for r0 in range(0, GLA_STEP_ROWS, ROW_TILE):
        rows = slice(r0, r0 + ROW_TILE)
        q, k, v, r = _project_head(x_ref[rows, :], wbf)
        if r0 == 0:
            wmix_bf_ref[...] = wmix_ref[0].astype(jnp.bfloat16)
            wout_bf_ref[...] = wout_ref[...].astype(jnp.bfloat16)
        q_inter, k_dec, b, levels = _gla_prepare(q, k, g_ref[rows, :], PROMPT_CHUNK)
        v_bf = v.astype(jnp.bfloat16)
        outs = []
        for c in range(ROW_TILE // PROMPT_CHUNK):
            o, s_new = _gla_chunk(c, PROMPT_CHUNK, q_inter, k_dec, b, levels, v_bf, masks, s_scr[...])
            s_scr[...] = s_new
            outs.append(o)
        y_ref[rows, :] = _gla_finish(jnp.concatenate(outs, axis=0), r, gn_ref[...])

    @pl.when(i == N_GLA_ROW_STEPS - 1)
    def _final_state():
        sp_ref[...] = s_scr[...]


def _gla_prompt(xb, w_in_t, g, gla_norm, s_meta, w_out):
    step = lambda h, i: h * N_GLA_ROW_STEPS + i
    return pl.pallas_call(
        _gla_prompt_kernel,
        grid=(GLA_HEADS, N_GLA_ROW_STEPS),
        in_specs=[
            pl.BlockSpec((GLA_STEP_ROWS, D_MODEL), lambda h, i: (i, 0)),
            *_head_weight_specs(),
            pl.BlockSpec((GLA_STEP_ROWS, HEAD_K), lambda h, i: (i, h)),
            pl.BlockSpec((1, HEAD_V), lambda h, i: (0, h)),
            pl.BlockSpec((None, HEAD_K, HEAD_V), lambda h, i: (h, 0, 0)),
            pl.BlockSpec((pl.Element(1), pl.Element(MIX_W_SLAB), pl.Element(D_MODEL)),
                         lambda h, i: (0, pl.multiple_of(GATE_COL0 + GATE_RANK + step(h, i) * MIX_W_SLAB,
                                                         GATE_RANK), 0)),
            pl.BlockSpec((None, OUT_W_SLAB, D_MODEL), lambda h, i: (0, step(h, i), 0)),
        ],
        out_specs=[
            pl.BlockSpec((GLA_STEP_ROWS, HEAD_V), lambda h, i: (i, h)),
            pl.BlockSpec((None, HEAD_K, HEAD_V), lambda h, i: (h, 0, 0)),
            pl.BlockSpec((MIX_W_SLAB, D_MODEL), lambda h, i: (step(h, i), 0)),
            pl.BlockSpec((OUT_W_SLAB, D_MODEL), lambda h, i: (step(h, i), 0)),
        ],
        out_shape=[
            jax.ShapeDtypeStruct((SEQ, D_MODEL), jnp.float32),
            jax.ShapeDtypeStruct((GLA_HEADS, HEAD_K, HEAD_V), jnp.float32),
            jax.ShapeDtypeStruct((MIX_W_ROWS, D_MODEL), jnp.bfloat16),
            jax.ShapeDtypeStruct((D_MODEL, D_MODEL), jnp.bfloat16),
        ],
        scratch_shapes=[pltpu.VMEM((HEAD_ROWS, D_MODEL), jnp.bfloat16),
                        pltpu.VMEM((HEAD_K, HEAD_V), jnp.float32)],
        compiler_params=pltpu.CompilerParams(
            dimension_semantics=("arbitrary", "arbitrary"), vmem_limit_bytes=VMEM_LIMIT),
        name="gla_prompt",
    )(xb, *([w_in_t] * 4), g, gla_norm, s_meta, w_in_t, w_out)


MIX_TN = 512
N_MIX_GROUPS = 5


MIX_STEP_ROWS = 2 * ROW_TILE
N_MIX_ROW_STEPS = SEQ // MIX_STEP_ROWS


def _mix_products(x, w_refs):
    return [lax.dot_general(x, w_ref[...], _NT, preferred_element_type=jnp.float32) for w_ref in w_refs]


def _mix_finish(c, p1, p2, cb, ga, gb, ya, cw_ref):
    conv = cw_ref[0:1, :] * p2 + cw_ref[1:2, :] * p1 + cw_ref[2:3, :] * c
    return (_sigmoid(ga) * ya + _sigmoid(gb) * (cb * conv)).astype(jnp.bfloat16)


def _mix_tail_kernel(x_ref, ya_ref, wcb_ref, wcc_ref, wcx_ref, wga_ref, wgb_ref, cw_ref, cache_ref,
                     m_ref, ctail_ref):
    cb, cc, cx, ga, gb = _mix_products(x_ref[...], (wcb_ref, wcc_ref, wcx_ref, wga_ref, wgb_ref))
    c = cc * cx
    row = lax.broadcasted_iota(jnp.int32, (ROW_TILE, MIX_TN), 0)
    prev1 = pltpu.roll(c, 1, 0)
    prev2 = pltpu.roll(c, 2, 0)

    def stream_rows(j):
        parts = [jnp.zeros((SAMPLE_ROW0, MIX_TN), jnp.float32)]
        for s in range(DEC_BATCH):
            parts.append(jnp.broadcast_to(cache_ref[2 * s + j:2 * s + j + 1, :], (DEC_SEQ, MIX_TN)))
        parts.append(jnp.zeros((ROW_TILE - SAMPLE_ROW0 - SAMPLE_ROWS, MIX_TN), jnp.float32))
        return jnp.concatenate(parts, axis=0)

    old = stream_rows(0)
    new = stream_rows(1)
    in_sample = (row >= SAMPLE_ROW0) & (row < SAMPLE_ROW0 + SAMPLE_ROWS)
    p16 = (row - SAMPLE_ROW0) & (DEC_SEQ - 1)
    first = in_sample & (p16 == 0)
    second = in_sample & (p16 == 1)
    p1 = jnp.where(row == META_ROW0, 0.0, jnp.where(first, new, prev1))
    p2 = jnp.where((row == META_ROW0) | (row == META_ROW0 + 1), 0.0,
                   jnp.where(first, old, jnp.where(second, new, prev2)))
    m_ref[...] = _mix_finish(c, p1, p2, cb, ga, gb, ya_ref[...], cw_ref)
    ctail_ref[...] = c


def _mix_prompt_kernel(x_ref, ya_ref, wcb_ref, wcc_ref, wcx_ref, wga_ref, wgb_ref, cw_ref, c0_ref,
                       m_ref, clast_ref, carry):
    i = pl.program_id(1)

    @pl.when(i == 0)
    def _start():
        carry[...] = c0_ref[...]

    hist = carry[...]
    row = lax.broadcasted_iota(jnp.int32, (ROW_TILE, MIX_TN), 0)
    for r0 in range(0, MIX_STEP_ROWS, ROW_TILE):
        rows = slice(r0, r0 + ROW_TILE)
        cb, cc, cx, ga, gb = _mix_products(x_ref[rows, :], (wcb_ref, wcc_ref, wcx_ref, wga_ref, wgb_ref))
        c = cc * cx
        last = jnp.broadcast_to(hist[7:8], c.shape)
        p1 = jnp.where(row == 0, last, pltpu.roll(c, 1, 0))
        p2 = jnp.where(row == 0, jnp.broadcast_to(hist[6:7], c.shape),
                       jnp.where(row == 1, last, pltpu.roll(c, 2, 0)))
        m_ref[rows, :] = _mix_finish(c, p1, p2, cb, ga, gb, ya_ref[rows, :], cw_ref)
        hist = c[ROW_TILE - 8:]
    carry[...] = hist

    @pl.when(i == N_MIX_ROW_STEPS - 1)
    def _last():
        clast_ref[...] = hist


def _mix_weight_specs(index_args):
    nc = D_MODEL // MIX_TN
    return [pl.BlockSpec((MIX_TN, D_MODEL), functools.partial(index_args, grp * nc))
            for grp in range(N_MIX_GROUPS)]


def _mix_tail(xb, ya_tail, w_mix_bf, conv_w, cache_rows):
    top = lambda cj: (0, cj)
    return pl.pallas_call(
        _mix_tail_kernel,
        grid=(D_MODEL // MIX_TN,),
        in_specs=[
            pl.BlockSpec((ROW_TILE, D_MODEL), lambda cj: (N_PROMPT_TILES, 0)),
            pl.BlockSpec((ROW_TILE, MIX_TN), top),
            *_mix_weight_specs(lambda base, cj: (base + cj, 0)),
            pl.BlockSpec((3, MIX_TN), top),
            pl.BlockSpec((2 * DEC_BATCH, MIX_TN), top),
        ],
        out_specs=[pl.BlockSpec((ROW_TILE, MIX_TN), top), pl.BlockSpec((ROW_TILE, MIX_TN), top)],
        out_shape=[
            jax.ShapeDtypeStruct((ROW_TILE, D_MODEL), jnp.bfloat16),
            jax.ShapeDtypeStruct((ROW_TILE, D_MODEL), jnp.float32),
        ],
        compiler_params=pltpu.CompilerParams(
            dimension_semantics=("arbitrary",), vmem_limit_bytes=VMEM_LIMIT),
        name="proj_conv_mix_tail",
    )(xb, ya_tail, *([w_mix_bf] * N_MIX_GROUPS), conv_w, cache_rows)


def _mix_prompt(xb, ya_prompt, w_mix_bf, conv_w, c_tail):
    top = lambda cj, i: (0, cj)
    return pl.pallas_call(
        _mix_prompt_kernel,
        grid=(D_MODEL // MIX_TN, N_MIX_ROW_STEPS),
        in_specs=[
            pl.BlockSpec((MIX_STEP_ROWS, D_MODEL), lambda cj, i: (i, 0)),
            pl.BlockSpec((MIX_STEP_ROWS, MIX_TN), lambda cj, i: (i, cj)),
            *_mix_weight_specs(lambda base, cj, i: (base + cj, 0)),
            pl.BlockSpec((3, MIX_TN), top),
            pl.BlockSpec((8, MIX_TN), lambda cj, i: (CHUNK // 8 - 1, cj)),
        ],
        out_specs=[pl.BlockSpec((MIX_STEP_ROWS, MIX_TN), lambda cj, i: (i, cj)), pl.BlockSpec((8, MIX_TN), top)],
        out_shape=[
            jax.ShapeDtypeStruct((SEQ, D_MODEL), jnp.bfloat16),
            jax.ShapeDtypeStruct((8, D_MODEL), jnp.float32),
        ],
        scratch_shapes=[pltpu.VMEM((8, MIX_TN), jnp.float32)],
        compiler_params=pltpu.CompilerParams(
            dimension_semantics=("arbitrary", "arbitrary"), vmem_limit_bytes=VMEM_LIMIT),
        name="proj_conv_mix",
    )(xb, ya_prompt, *([w_mix_bf] * N_MIX_GROUPS), conv_w, c_tail)


SUB_ROWS = 128


def _out_proj_kernel(mp_ref, mt_ref, xn_ref, w_ref, lg_ref, lb_ref, h_ref, hb_ref):
    is_prompt = pl.program_id(0) < N_PROMPT_TILES
    for r in range(0, ROW_TILE, SUB_ROWS):
        rows = slice(r, r + SUB_ROWS)
        m = jnp.where(is_prompt, mp_ref[rows, :], mt_ref[rows, :])
        mix = jnp.dot(m, w_ref[...], preferred_element_type=jnp.float32)
        h = _layer_norm(ALPHA * xn_ref[rows, :] + mix, lg_ref[...], lb_ref[...])
        h_ref[rows, :] = h
        hb_ref[rows, :] = h.astype(jnp.bfloat16)


def _out_proj(m_prompt, m_tail, xn, w_out, ln_g, ln_b):
    row = lambda i: (i, 0)
    full = lambda i: (0, 0)
    return pl.pallas_call(
        _out_proj_kernel,
        grid=(N_TILES,),
        in_specs=[
            pl.BlockSpec((ROW_TILE, D_MODEL), lambda i: (jnp.minimum(i, N_PROMPT_TILES - 1), 0)),
            pl.BlockSpec((ROW_TILE, D_MODEL), full),
            pl.BlockSpec((ROW_TILE, D_MODEL), row),
            pl.BlockSpec((D_MODEL, D_MODEL), full),
            pl.BlockSpec((1, D_MODEL), full),
            pl.BlockSpec((1, D_MODEL), full),
        ],
        out_specs=[pl.BlockSpec((ROW_TILE, D_MODEL), row), pl.BlockSpec((ROW_TILE, D_MODEL), row)],
        out_shape=[
            jax.ShapeDtypeStruct((ROWS, D_MODEL), jnp.float32),
            jax.ShapeDtypeStruct((ROWS, D_MODEL), jnp.bfloat16),
        ],
        compiler_params=pltpu.CompilerParams(
            dimension_semantics=("arbitrary",), vmem_limit_bytes=VMEM_LIMIT),
        name="out_proj_ln",
    )(m_prompt, m_tail, xn, w_out, ln_g, ln_b)


FF_TILE = 512
N_FF_TILES = D_FF // FF_TILE
FFN_UP_ROWS = 2 * BIG_ROW_TILE


N_FFN_UP_STEPS = N_FF_TILES * (ROWS // FFN_UP_ROWS)
DOWN_W_SLAB = D_FF // N_FFN_UP_STEPS


def _ffn_up_kernel(hb_ref, wg_ref, wu_ref, wd_ref, act_ref, wd_bf_ref, wg_bf, wu_bf):
    @pl.when(pl.program_id(1) == 0)
    def _cast():
        for r in range(0, D_MODEL, W_SLAB):
            wg_bf[r:r + W_SLAB, :] = wg_ref[r:r + W_SLAB, :].astype(jnp.bfloat16)
            wu_bf[r:r + W_SLAB, :] = wu_ref[r:r + W_SLAB, :].astype(jnp.bfloat16)

    for r in range(0, FFN_UP_ROWS, BIG_ROW_TILE):
        rows = slice(r, r + BIG_ROW_TILE)
        hb = hb_ref[rows, :]
        gate = jnp.dot(hb, wg_bf[...], preferred_element_type=jnp.float32)
        up = jnp.dot(hb, wu_bf[...], preferred_element_type=jnp.float32)
        if r == 0:
            wd_bf_ref[...] = wd_ref[...].astype(jnp.bfloat16)
        act_ref[rows, :] = (gate * _sigmoid(gate) * up).astype(jnp.bfloat16)


def _ffn_up(hb, w_ffn_in, w_ffn_out):
    step = lambda j, i: j * (ROWS // FFN_UP_ROWS) + i
    return pl.pallas_call(
        _ffn_up_kernel,
        grid=(N_FF_TILES, ROWS // FFN_UP_ROWS),
        in_specs=[
            pl.BlockSpec((FFN_UP_ROWS, D_MODEL), lambda j, i: (i, 0)),
            pl.BlockSpec((None, D_MODEL, FF_TILE), lambda j, i: (0, 0, j)),
            pl.BlockSpec((None, D_MODEL, FF_TILE), lambda j, i: (0, 0, N_FF_TILES + j)),
            pl.BlockSpec((None, DOWN_W_SLAB, D_MODEL), lambda j, i: (0, step(j, i), 0)),
        ],
        out_specs=[
            pl.BlockSpec((FFN_UP_ROWS, FF_TILE), lambda j, i: (i, j)),
            pl.BlockSpec((DOWN_W_SLAB, D_MODEL), lambda j, i: (step(j, i), 0)),
        ],
        out_shape=[
            jax.ShapeDtypeStruct((ROWS, D_FF), jnp.bfloat16),
            jax.ShapeDtypeStruct((D_FF, D_MODEL), jnp.bfloat16),
        ],
        scratch_shapes=[pltpu.VMEM((D_MODEL, FF_TILE), jnp.bfloat16),
                        pltpu.VMEM((D_MODEL, FF_TILE), jnp.bfloat16)],
        compiler_params=pltpu.CompilerParams(
            dimension_semantics=("arbitrary", "arbitrary"), vmem_limit_bytes=VMEM_LIMIT),
        name="ffn_up",
    )(hb, w_ffn_in, w_ffn_in, w_ffn_out)


DOWN_TILE = 256
N_DOWN_PROMPT = SEQ // DOWN_TILE
N_DOWN_TILES = ROWS // DOWN_TILE


def _ffn_down_kernel(act_ref, h_ref, wd_ref, lg_ref, lb_ref, yp_ref, yt_ref):
    i = pl.program_id(0)

    def rows_out(o_ref):
        for r in range(0, DOWN_TILE, SUB_ROWS):
            rows = slice(r, r + SUB_ROWS)
            f = jnp.dot(act_ref[rows, :], wd_ref[...], preferred_element_type=jnp.float32)
            o_ref[rows, :] = _layer_norm(ALPHA * h_ref[rows, :] + f, lg_ref[...], lb_ref[...])

    @pl.when(i < N_DOWN_PROMPT)
    def _prompt():
        rows_out(yp_ref)

    @pl.when(i >= N_DOWN_PROMPT)
    def _tail():
        rows_out(yt_ref)


def _ffn_down(act, h, w_down, ln_g, ln_b):
    row = lambda i: (i, 0)
    full = lambda i: (0, 0)
    return pl.pallas_call(
        _ffn_down_kernel,
        grid=(N_DOWN_TILES,),
        in_specs=[
            pl.BlockSpec((DOWN_TILE, D_FF), row),
            pl.BlockSpec((DOWN_TILE, D_MODEL), row),
            pl.BlockSpec((D_FF, D_MODEL), full, pipeline_mode=pl.Buffered(1)),
            pl.BlockSpec((1, D_MODEL), full),
            pl.BlockSpec((1, D_MODEL), full),
        ],
        out_specs=[
            pl.BlockSpec((None, DOWN_TILE, D_MODEL), lambda i: (0, jnp.minimum(i, N_DOWN_PROMPT - 1), 0)),
            pl.BlockSpec((DOWN_TILE, D_MODEL), lambda i: (jnp.maximum(i - N_DOWN_PROMPT, 0), 0)),
        ],
        out_shape=[
            jax.ShapeDtypeStruct((1, SEQ, D_MODEL), jnp.float32),
            jax.ShapeDtypeStruct((ROW_TILE, D_MODEL), jnp.float32),
        ],
        compiler_params=pltpu.CompilerParams(
            dimension_semantics=("arbitrary",), vmem_limit_bytes=VMEM_LIMIT),
        name="ffn_down_ln",
    )(act, h, w_down, ln_g, ln_b)


def kernel(x_prompt, x_sample, state_gla, cache_conv, meta_tokens, ln_in_g, ln_in_b, w_in, w_gate_up, b_gate, gla_norm, conv_w, w_out, ln1_g, ln1_b, w_ffn_in, w_ffn_out, ln2_g, ln2_b):
    f32, bf16 = jnp.float32, jnp.bfloat16
    tail = jnp.concatenate([
        jnp.zeros((META_ROW0, D_MODEL), f32),
        meta_tokens.astype(f32),
        x_sample.reshape(SAMPLE_ROWS, D_MODEL),
        jnp.zeros((ROW_TILE - SAMPLE_ROW0 - SAMPLE_ROWS, D_MODEL), f32),
    ], axis=0)
    w_gu = jnp.pad(w_gate_up[0], ((0, LANE - GATE_RANK), (0, 0))).astype(bf16)
    vec = lambda p: p.reshape(1, -1).astype(f32)

    w_in_t = jnp.swapaxes(w_in, 1, 2)
    xn, xb, g = _ln_gate(x_prompt, tail, vec(ln_in_g), vec(ln_in_b), w_in_t, w_gu, vec(b_gate[0]))
    ya_tail, s_meta, s_sample = _gla_tail(xb, w_in_t, g, vec(gla_norm[0]), state_gla[0])
    ya_prompt, s_prompt, w_mix_bf, w_out_bf = _gla_prompt(xb, w_in_t, g, vec(gla_norm[0]), s_meta, w_out)
    m_tail, c_tail = _mix_tail(xb, ya_tail, w_mix_bf, conv_w[0], cache_conv[0].reshape(2 * DEC_BATCH, D_MODEL))
    m_prompt, c_last = _mix_prompt(xb, ya_prompt, w_mix_bf, conv_w[0], c_tail)
    h, hb = _out_proj(m_prompt, m_tail, xn, w_out_bf, vec(ln1_g[0]), vec(ln1_b[0]))
    act, w_down_bf = _ffn_up(hb, w_ffn_in, w_ffn_out)
    y_prompt, y_tail = _ffn_down(act, h, w_down_bf, vec(ln2_g[0]), vec(ln2_b[0]))

    y_sample = y_tail[SAMPLE_ROW0:SAMPLE_ROW0 + SAMPLE_ROWS].reshape(DEC_BATCH, DEC_SEQ, D_MODEL)
    cache_prompt = c_last[6:8][None, None]
    cache_sample = c_tail[SAMPLE_ROW0:SAMPLE_ROW0 + SAMPLE_ROWS].reshape(
        DEC_BATCH, DEC_SEQ, D_MODEL)[:, DEC_SEQ - 2:][None]
    return (y_prompt, y_sample, s_prompt[None, None], cache_prompt, s_sample[None], cache_sample)
```

```python
import functools

import jax
import jax.numpy as jnp
from jax import lax
from jax.experimental import pallas as pl
from jax.experimental.pallas import tpu as pltpu

D_MODEL = 2048
SEQ = 8192
DEC_BATCH = 16
DEC_SEQ = 16
CHUNK = 64
N_META = 16
GLA_HEADS = 4
HEAD_K = 256
HEAD_V = 512
GLA_DK = GLA_HEADS * HEAD_K
GATE_RANK = 16
GATE_TEMP = 16.0
D_FF = 5632
ALPHA = 2.0 ** 0.25
LN_EPS = 1e-5
RMS_EPS = 1e-6

ROW_TILE = 512
N_PROMPT_TILES = SEQ // ROW_TILE
N_TILES = N_PROMPT_TILES + 1
ROWS = N_TILES * ROW_TILE
META_ROW0 = CHUNK - N_META
SAMPLE_ROW0 = CHUNK
SAMPLE_ROWS = DEC_BATCH * DEC_SEQ
LANE = 128
GATE_COL0 = 2 * GLA_DK + 2 * D_MODEL
VMEM_LIMIT = 58 * 1024 * 1024

_NT = (((1,), (1,)), ((), ()))
_TN = (((0,), (0,)), ((), ()))


def _tile_first_tail(i):
    return (i + N_PROMPT_TILES) % N_TILES


def _layer_norm(x, g, b):
    mu = jnp.mean(x, axis=-1, keepdims=True)
    xc = x - mu
    var = jnp.mean(xc * xc, axis=-1, keepdims=True)
    return xc * lax.rsqrt(var + LN_EPS) * g + b


def _log_sigmoid(x):
    return jnp.minimum(x, 0.0) - jnp.log1p(jnp.exp(-jnp.abs(x)))


def _sigmoid(x):
    return 1.0 / (1.0 + jnp.exp(-x))


def _ln_gate_kernel(xp_ref, xt_ref, lg_ref, lb_ref, wa_ref, wgu_ref, bg_ref, xn_ref, xb_ref, g_ref):
    x = jnp.where(pl.program_id(0) < N_PROMPT_TILES, xp_ref[...], xt_ref[...])
    xn = _layer_norm(x, lg_ref[...], lb_ref[...])
    xn_ref[...] = xn
    xb = xn.astype(jnp.bfloat16)
    xb_ref[...] = xb
    a = lax.dot_general(xb, wa_ref[...].astype(jnp.bfloat16), _NT, preferred_element_type=jnp.float32)
    z = jnp.dot(a.astype(jnp.bfloat16), wgu_ref[...], preferred_element_type=jnp.float32)
    g_ref[...] = _log_sigmoid(z + bg_ref[...]) * (1.0 / GATE_TEMP)


def _ln_gate(x_prompt, x_tail, ln_g, ln_b, w_in_t, w_gu, b_gate):
    row = lambda i: (i, 0)
    full = lambda i: (0, 0)
    return pl.pallas_call(
        _ln_gate_kernel,
        grid=(N_TILES,),
        in_specs=[
            pl.BlockSpec((None, ROW_TILE, D_MODEL), lambda i: (0, jnp.minimum(i, N_PROMPT_TILES - 1), 0)),
            pl.BlockSpec((ROW_TILE, D_MODEL), full),
            pl.BlockSpec((1, D_MODEL), full),
            pl.BlockSpec((1, D_MODEL), full),
            pl.BlockSpec((None, LANE, D_MODEL), lambda i: (0, GATE_COL0 // LANE, 0)),
            pl.BlockSpec((LANE, GLA_DK), full),
            pl.BlockSpec((1, GLA_DK), full),
        ],
        out_specs=[
            pl.BlockSpec((ROW_TILE, D_MODEL), row),
            pl.BlockSpec((ROW_TILE, D_MODEL), row),
            pl.BlockSpec((ROW_TILE, GLA_DK), row),
        ],
        out_shape=[
            jax.ShapeDtypeStruct((ROWS, D_MODEL), jnp.float32),
            jax.ShapeDtypeStruct((ROWS, D_MODEL), jnp.bfloat16),
            jax.ShapeDtypeStruct((ROWS, GLA_DK), jnp.float32),
        ],
        compiler_params=pltpu.CompilerParams(
            dimension_semantics=("arbitrary",), vmem_limit_bytes=VMEM_LIMIT),
        name="ln_gate",
    )(x_prompt, x_tail, ln_g, ln_b, w_in_t, w_gu, b_gate)


W_SLAB = 256
BIG_ROW_TILE = 1088
N_BIG_TILES = ROWS // BIG_ROW_TILE


PROMPT_CHUNK = 256
HEAD_ROWS = 2 * HEAD_K + 2 * HEAD_V


def _cast_head_weights(w_refs, wbf):
    off = 0
    for w_ref in w_refs:
        for r in range(0, w_ref.shape[0], W_SLAB):
            wbf[off + r:off + r + W_SLAB, :] = w_ref[r:r + W_SLAB, :].astype(jnp.bfloat16)
        off += w_ref.shape[0]


def _project_head(x, wbf):
    qk = lax.dot_general(x, wbf[0:2 * HEAD_K, :], _NT, preferred_element_type=jnp.float32)
    vr = lax.dot_general(x, wbf[2 * HEAD_K:, :], _NT, preferred_element_type=jnp.float32)
    return qk[:, :HEAD_K], qk[:, HEAD_K:], vr[:, :HEAD_V], vr[:, HEAD_V:]


def _head_weight_specs(**vr_kw):
    kq = GLA_DK // HEAD_K
    return [
        pl.BlockSpec((None, HEAD_K, D_MODEL), lambda h, *_: (0, h, 0)),
        pl.BlockSpec((None, HEAD_K, D_MODEL), lambda h, *_: (0, kq + h, 0)),
        pl.BlockSpec((None, HEAD_V, D_MODEL), lambda h, *_: (0, kq + h, 0), **vr_kw),
        pl.BlockSpec((None, HEAD_V, D_MODEL), lambda h, *_: (0, 2 * kq + h, 0), **vr_kw),
    ]


def _row_of_block(b, block, r):
    n, w = b.shape
    b3 = b.reshape(n // block, block, w)
    return jnp.broadcast_to(b3[:, r:r + 1, :], b3.shape).reshape(n, w)


def _half_sizes(chunk):
    return tuple(chunk >> (j + 1) for j in range(chunk.bit_length() - 1))


def _gla_prepare(q, k, g, chunk):
    n, w = q.shape
    row = lax.broadcasted_iota(jnp.int32, (n, w), 0)
    pos = row & (chunk - 1)
    b = g
    for sh in reversed(_half_sizes(chunk)):
        b = b + jnp.where(pos >= sh, pltpu.roll(b, sh, 0), 0.0)
    qs = q * (HEAD_K ** -0.5)
    b_last = _row_of_block(b, chunk, chunk - 1)
    q_inter = (qs * jnp.exp(b)).astype(jnp.bfloat16)
    k_dec = (k * jnp.exp(b_last - b)).astype(jnp.bfloat16)
    levels = []
    for s in _half_sizes(chunk):
        upper = (row & (2 * s - 1)) >= s
        if s >= 4:
            b_mid = _row_of_block(b, 2 * s, s - 1)
            d = jnp.where(upper, b - b_mid, b_mid - b)
        elif s == 2:
            p4 = row & 3
            g_prev = pltpu.roll(g, 1, 0)
            g_next = pltpu.roll(g, n - 1, 0)
            d = jnp.where(p4 == 0, g_next, jnp.where(p4 == 1, 0.0, jnp.where(p4 == 2, g, g + g_prev)))
        else:
            d = jnp.where(upper, g, 0.0)
        x = (jnp.where(upper, qs, k) * jnp.exp(d)).astype(jnp.bfloat16)
        levels.append((x, x))
    levels.append((qs.astype(jnp.bfloat16), k.astype(jnp.bfloat16)))
    return q_inter, k_dec, b, levels


def _score_masks(chunk):
    ii = lax.broadcasted_iota(jnp.int32, (chunk, chunk), 0)
    jj = lax.broadcasted_iota(jnp.int32, (chunk, chunk), 1)
    masks = []
    for s in _half_sizes(chunk):
        blk = 2 * s
        same = (ii & ~(blk - 1)) == (jj & ~(blk - 1))
        masks.append(same & ((ii & (blk - 1)) >= s) & ((jj & (blk - 1)) < s))
    masks.append(ii == jj)
    return masks


def _gla_chunk(c, chunk, q_inter, k_dec, b, levels, v_bf, masks, s_val):
    r0 = c * chunk
    sl = slice(r0, r0 + chunk)
    scores = jnp.zeros((chunk, chunk), jnp.float32)
    for (qe, ke), m in zip(levels, masks):
        p = lax.dot_general(qe[sl], ke[sl], _NT, preferred_element_type=jnp.float32)
        scores = jnp.where(m, p, scores)
    vc = v_bf[sl]
    o = jnp.dot(scores.astype(jnp.bfloat16), vc, preferred_element_type=jnp.float32)
    o = o + jnp.dot(q_inter[sl], s_val.astype(jnp.bfloat16), preferred_element_type=jnp.float32)
    eb = jnp.exp(b[r0 + chunk - 1:r0 + chunk, :])
    eb_t = jnp.transpose(jnp.broadcast_to(eb, (LANE, HEAD_K)))
    decay = jnp.concatenate([eb_t] * (HEAD_V // LANE), axis=1)
    ds = lax.dot_general(k_dec[sl], vc, _TN, preferred_element_type=jnp.float32)
    return o, decay * s_val + ds


def _gla_finish(o, r, gn):
    o = o * lax.rsqrt(jnp.mean(o * o, axis=-1, keepdims=True) + RMS_EPS)
    return o * gn * (r * _sigmoid(r))


TAIL_HALF = DEC_BATCH // 2
TAIL_SPLIT_ROW = SAMPLE_ROW0 + TAIL_HALF * DEC_SEQ


def _gla_tail_kernel(x_ref, wq_ref, wk_ref, wv_ref, wr_ref, g_ref, gn_ref, sin_ref,
                     y_ref, sm_ref, ss_ref, wbf, u_scr):
    half = pl.program_id(1)
    masks = _score_masks(CHUNK)
    kv0, r0 = 2 * HEAD_K, 2 * HEAD_K + HEAD_V

    @pl.when(half == 0)
    def _project():
        _cast_head_weights((wq_ref, wk_ref, wv_ref, wr_ref), wbf)
        q, k, v, r = _project_head(x_ref[...], wbf)
        u_scr[...] = jnp.concatenate([q, k, v, r], axis=1)

    def run(first_stream, with_meta, row_lo, row_hi):
        def padded(c0, width):
            parts = [u_scr[0:CHUNK, c0:c0 + width]] if with_meta else []
            zeros = jnp.zeros((META_ROW0, width), jnp.float32)
            for s in range(first_stream, first_stream + TAIL_HALF):
                lo = SAMPLE_ROW0 + s * DEC_SEQ
                parts += [zeros, u_scr[lo:lo + DEC_SEQ, c0:c0 + width]]
            return jnp.concatenate(parts, axis=0)

        def padded_g():
            parts = [g_ref[0:CHUNK, :]] if with_meta else []
            zeros = jnp.zeros((META_ROW0, HEAD_K), jnp.float32)
            for s in range(first_stream, first_stream + TAIL_HALF):
                lo = SAMPLE_ROW0 + s * DEC_SEQ
                parts += [zeros, g_ref[lo:lo + DEC_SEQ, :]]
            return jnp.concatenate(parts, axis=0)

        q = padded(0, HEAD_K)
        k = padded(HEAD_K, HEAD_K)
        v = padded(kv0, HEAD_V)
        g = padded_g()
        live = (lax.broadcasted_iota(jnp.int32, q.shape, 0) & (CHUNK - 1)) >= META_ROW0
        k = jnp.where(live, k, 0.0)
        g = jnp.where(live, g, 0.0)
        q_inter, k_dec, b, levels = _gla_prepare(q, k, g, CHUNK)
        v_bf = v.astype(jnp.bfloat16)
        outs, c = [], 0
        if with_meta:
            outs.append(jnp.zeros((META_ROW0, HEAD_V), jnp.float32))
            o, s_meta = _gla_chunk(0, CHUNK, q_inter, k_dec, b, levels, v_bf, masks,
                                   jnp.zeros((HEAD_K, HEAD_V), jnp.float32))
            sm_ref[...] = s_meta
            outs.append(o[META_ROW0:])
            c = 1
        for s in range(TAIL_HALF):
            o, s_new = _gla_chunk(c + s, CHUNK, q_inter, k_dec, b, levels, v_bf, masks, sin_ref[s])
            ss_ref[s] = s_new
            outs.append(o[META_ROW0:])
        if not with_meta:
            outs.append(jnp.zeros((ROW_TILE - SAMPLE_ROW0 - SAMPLE_ROWS, HEAD_V), jnp.float32))
        y_ref[row_lo:row_hi, :] = _gla_finish(
            jnp.concatenate(outs, axis=0), u_scr[row_lo:row_hi, r0:r0 + HEAD_V], gn_ref[...])

    @pl.when(half == 0)
    def _first():
        run(0, True, 0, TAIL_SPLIT_ROW)

    @pl.when(half == 1)
    def _second():
        run(TAIL_HALF, False, TAIL_SPLIT_ROW, ROW_TILE)


def _gla_tail(xb, w_in_t, g, gla_norm, state_gla):
    states = pl.BlockSpec((TAIL_HALF, None, HEAD_K, HEAD_V), lambda h, half: (half, h, 0, 0))
    return pl.pallas_call(
        _gla_tail_kernel,
        grid=(GLA_HEADS, 2),
        in_specs=[
            pl.BlockSpec((ROW_TILE, D_MODEL), lambda h, half: (N_PROMPT_TILES, 0)),
            *_head_weight_specs(),
            pl.BlockSpec((ROW_TILE, HEAD_K), lambda h, half: (N_PROMPT_TILES, h)),
            pl.BlockSpec((1, HEAD_V), lambda h, half: (0, h)),
            states,
        ],
        out_specs=[
            pl.BlockSpec((ROW_TILE, HEAD_V), lambda h, half: (0, h)),
            pl.BlockSpec((None, HEAD_K, HEAD_V), lambda h, half: (h, 0, 0)),
            states,
        ],
        out_shape=[
            jax.ShapeDtypeStruct((ROW_TILE, D_MODEL), jnp.float32),
            jax.ShapeDtypeStruct((GLA_HEADS, HEAD_K, HEAD_V), jnp.float32),
            jax.ShapeDtypeStruct((DEC_BATCH, GLA_HEADS, HEAD_K, HEAD_V), jnp.float32),
        ],
        scratch_shapes=[pltpu.VMEM((HEAD_ROWS, D_MODEL), jnp.bfloat16),
                        pltpu.VMEM((ROW_TILE, HEAD_ROWS), jnp.float32)],
        compiler_params=pltpu.CompilerParams(
            dimension_semantics=("arbitrary", "arbitrary"), vmem_limit_bytes=VMEM_LIMIT),
        name="gla_tail",
    )(xb, *([w_in_t] * 4), g, gla_norm, state_gla)


GLA_STEP_ROWS = 2 * ROW_TILE
N_GLA_ROW_STEPS = SEQ // GLA_STEP_ROWS
N_GLA_STEPS = GLA_HEADS * N_GLA_ROW_STEPS
MIX_W_ROWS = 5 * D_MODEL
MIX_W_SLAB = MIX_W_ROWS // N_GLA_STEPS
OUT_W_SLAB = D_MODEL // N_GLA_STEPS


def _gla_prompt_kernel(x_ref, wq_ref, wk_ref, wv_ref, wr_ref, g_ref, gn_ref, s0_ref, wmix_ref, wout_ref,
                       y_ref, sp_ref, wmix_bf_ref, wout_bf_ref, wbf, s_scr):
    i = pl.program_id(1)

    @pl.when(i == 0)
    def _start_head():
        _cast_head_weights((wq_ref, wk_ref, wv_ref, wr_ref), wbf)
        s_scr[...] = s0_ref[...]

    masks = _score_masks(PROMPT_CHUNK)
    for r0 in range(0, GLA_STEP_ROWS, ROW_TILE):
        rows = slice(r0, r0 + ROW_TILE)
        q, k, v, r = _project_head(x_ref[rows, :], wbf)
        if r0 == 0:
            wmix_bf_ref[...] = wmix_ref[0].astype(jnp.bfloat16)
            wout_bf_ref[...] = wout_ref[...].astype(jnp.bfloat16)
        q_inter, k_dec, b, levels = _gla_prepare(q, k, g_ref[rows, :], PROMPT_CHUNK)
        v_bf = v.astype(jnp.bfloat16)
        outs = []
        for c in range(ROW_TILE // PROMPT_CHUNK):
            o, s_new = _gla_chunk(c, PROMPT_CHUNK, q_inter, k_dec, b, levels, v_bf, masks, s_scr[...])
            s_scr[...] = s_new
            outs.append(o)
        y_ref[rows, :] = _gla_finish(jnp.concatenate(outs, axis=0), r, gn_ref[...])

    @pl.when(i == N_GLA_ROW_STEPS - 1)
    def _final_state():
        sp_ref[...] = s_scr[...]


def _gla_prompt(xb, w_in_t, g, gla_norm, s_meta, w_out):
    step = lambda h, i: h * N_GLA_ROW_STEPS + i
    return pl.pallas_call(
        _gla_prompt_kernel,
        grid=(GLA_HEADS, N_GLA_ROW_STEPS),
        in_specs=[
            pl.BlockSpec((GLA_STEP_ROWS, D_MODEL), lambda h, i: (i, 0)),
            *_head_weight_specs(pipeline_mode=pl.Buffered(1)),
            pl.BlockSpec((GLA_STEP_ROWS, HEAD_K), lambda h, i: (i, h)),
            pl.BlockSpec((1, HEAD_V), lambda h, i: (0, h)),
            pl.BlockSpec((None, HEAD_K, HEAD_V), lambda h, i: (h, 0, 0)),
            pl.BlockSpec((pl.Element(1), pl.Element(MIX_W_SLAB), pl.Element(D_MODEL)),
                         lambda h, i: (0, pl.multiple_of(GATE_COL0 + GATE_RANK + step(h, i) * MIX_W_SLAB,
                                                         GATE_RANK), 0)),
            pl.BlockSpec((None, OUT_W_SLAB, D_MODEL), lambda h, i: (0, step(h, i), 0)),
        ],
        out_specs=[
            pl.BlockSpec((GLA_STEP_ROWS, HEAD_V), lambda h, i: (i, h)),
            pl.BlockSpec((None, HEAD_K, HEAD_V), lambda h, i: (h, 0, 0)),
            pl.BlockSpec((MIX_W_SLAB, D_MODEL), lambda h, i: (step(h, i), 0)),
            pl.BlockSpec((OUT_W_SLAB, D_MODEL), lambda h, i: (step(h, i), 0)),
        ],
        out_shape=[
            jax.ShapeDtypeStruct((SEQ, D_MODEL), jnp.float32),
            jax.ShapeDtypeStruct((GLA_HEADS, HEAD_K, HEAD_V), jnp.float32),
            jax.ShapeDtypeStruct((MIX_W_ROWS, D_MODEL), jnp.bfloat16),
            jax.ShapeDtypeStruct((D_MODEL, D_MODEL), jnp.bfloat16),
        ],
        scratch_shapes=[pltpu.VMEM((HEAD_ROWS, D_MODEL), jnp.bfloat16),
                        pltpu.VMEM((HEAD_K, HEAD_V), jnp.float32)],
        compiler_params=pltpu.CompilerParams(
            dimension_semantics=("arbitrary", "arbitrary"), vmem_limit_bytes=VMEM_LIMIT),
        name="gla_prompt",
    )(xb, *([w_in_t] * 4), g, gla_norm, s_meta, w_in_t, w_out)


MIX_TN = 512
N_MIX_GROUPS = 5


MIX_STEP_ROWS = 2 * ROW_TILE
N_MIX_ROW_STEPS = SEQ // MIX_STEP_ROWS


def _mix_products(x, w_refs):
    return [lax.dot_general(x, w_ref[...], _NT, preferred_element_type=jnp.float32) for w_ref in w_refs]


def _mix_finish(c, p1, p2, cb, ga, gb, ya, cw_ref):
    conv = cw_ref[0:1, :] * p2 + cw_ref[1:2, :] * p1 + cw_ref[2:3, :] * c
    return (_sigmoid(ga) * ya + _sigmoid(gb) * (cb * conv)).astype(jnp.bfloat16)


def _mix_tail_kernel(x_ref, ya_ref, wcb_ref, wcc_ref, wcx_ref, wga_ref, wgb_ref, cw_ref, cache_ref,
                     m_ref, ctail_ref):
    cb, cc, cx, ga, gb = _mix_products(x_ref[...], (wcb_ref, wcc_ref, wcx_ref, wga_ref, wgb_ref))
    c = cc * cx
    row = lax.broadcasted_iota(jnp.int32, (ROW_TILE, MIX_TN), 0)
    prev1 = pltpu.roll(c, 1, 0)
    prev2 = pltpu.roll(c, 2, 0)

    def stream_rows(j):
        parts = [jnp.zeros((SAMPLE_ROW0, MIX_TN), jnp.float32)]
        for s in range(DEC_BATCH):
            parts.append(jnp.broadcast_to(cache_ref[2 * s + j:2 * s + j + 1, :], (DEC_SEQ, MIX_TN)))
        parts.append(jnp.zeros((ROW_TILE - SAMPLE_ROW0 - SAMPLE_ROWS, MIX_TN), jnp.float32))
        return jnp.concatenate(parts, axis=0)

    old = stream_rows(0)
    new = stream_rows(1)
    in_sample = (row >= SAMPLE_ROW0) & (row < SAMPLE_ROW0 + SAMPLE_ROWS)
    p16 = (row - SAMPLE_ROW0) & (DEC_SEQ - 1)
    first = in_sample & (p16 == 0)
    second = in_sample & (p16 == 1)
    p1 = jnp.where(row == META_ROW0, 0.0, jnp.where(first, new, prev1))
    p2 = jnp.where((row == META_ROW0) | (row == META_ROW0 + 1), 0.0,
                   jnp.where(first, old, jnp.where(second, new, prev2)))
    m_ref[...] = _mix_finish(c, p1, p2, cb, ga, gb, ya_ref[...], cw_ref)
    ctail_ref[...] = c


def _mix_prompt_kernel(x_ref, ya_ref, wcb_ref, wcc_ref, wcx_ref, wga_ref, wgb_ref, cw_ref, c0_ref,
                       m_ref, clast_ref, carry):
    i = pl.program_id(1)

    @pl.when(i == 0)
    def _start():
        carry[...] = c0_ref[...]

    hist = carry[...]
    row = lax.broadcasted_iota(jnp.int32, (ROW_TILE, MIX_TN), 0)
    for r0 in range(0, MIX_STEP_ROWS, ROW_TILE):
        rows = slice(r0, r0 + ROW_TILE)
        cb, cc, cx, ga, gb = _mix_products(x_ref[rows, :], (wcb_ref, wcc_ref, wcx_ref, wga_ref, wgb_ref))
        c = cc * cx
        last = jnp.broadcast_to(hist[7:8], c.shape)
        p1 = jnp.where(row == 0, last, pltpu.roll(c, 1, 0))
        p2 = jnp.where(row == 0, jnp.broadcast_to(hist[6:7], c.shape),
                       jnp.where(row == 1, last, pltpu.roll(c, 2, 0)))
        m_ref[rows, :] = _mix_finish(c, p1, p2, cb, ga, gb, ya_ref[rows, :], cw_ref)
        hist = c[ROW_TILE - 8:]
    carry[...] = hist

    @pl.when(i == N_MIX_ROW_STEPS - 1)
    def _last():
        clast_ref[...] = hist


def _mix_weight_specs(index_args):
    nc = D_MODEL // MIX_TN
    return [pl.BlockSpec((MIX_TN, D_MODEL), functools.partial(index_args, grp * nc))
            for grp in range(N_MIX_GROUPS)]


def _mix_tail(xb, ya_tail, w_mix_bf, conv_w, cache_rows):
    top = lambda cj: (0, cj)
    return pl.pallas_call(
        _mix_tail_kernel,
        grid=(D_MODEL // MIX_TN,),
        in_specs=[
            pl.BlockSpec((ROW_TILE, D_MODEL), lambda cj: (N_PROMPT_TILES, 0)),
            pl.BlockSpec((ROW_TILE, MIX_TN), top),
            *_mix_weight_specs(lambda base, cj: (base + cj, 0)),
            pl.BlockSpec((3, MIX_TN), top),
            pl.BlockSpec((2 * DEC_BATCH, MIX_TN), top),
        ],
        out_specs=[pl.BlockSpec((ROW_TILE, MIX_TN), top), pl.BlockSpec((ROW_TILE, MIX_TN), top)],
        out_shape=[
            jax.ShapeDtypeStruct((ROW_TILE, D_MODEL), jnp.bfloat16),
            jax.ShapeDtypeStruct((ROW_TILE, D_MODEL), jnp.float32),
        ],
        compiler_params=pltpu.CompilerParams(
            dimension_semantics=("arbitrary",), vmem_limit_bytes=VMEM_LIMIT),
        name="proj_conv_mix_tail",
    )(xb, ya_tail, *([w_mix_bf] * N_MIX_GROUPS), conv_w, cache_rows)


def _mix_prompt(xb, ya_prompt, w_mix_bf, conv_w, c_tail):
    top = lambda cj, i: (0, cj)
    return pl.pallas_call(
        _mix_prompt_kernel,
        grid=(D_MODEL // MIX_TN, N_MIX_ROW_STEPS),
        in_specs=[
            pl.BlockSpec((MIX_STEP_ROWS, D_MODEL), lambda cj, i: (i, 0)),
            pl.BlockSpec((MIX_STEP_ROWS, MIX_TN), lambda cj, i: (i, cj)),
            *_mix_weight_specs(lambda base, cj, i: (base + cj, 0)),
            pl.BlockSpec((3, MIX_TN), top),
            pl.BlockSpec((8, MIX_TN), lambda cj, i: (CHUNK // 8 - 1, cj)),
        ],
        out_specs=[pl.BlockSpec((MIX_STEP_ROWS, MIX_TN), lambda cj, i: (i, cj)), pl.BlockSpec((8, MIX_TN), top)],
        out_shape=[
            jax.ShapeDtypeStruct((SEQ, D_MODEL), jnp.bfloat16),
            jax.ShapeDtypeStruct((8, D_MODEL), jnp.float32),
        ],
        scratch_shapes=[pltpu.VMEM((8, MIX_TN), jnp.float32)],
        compiler_params=pltpu.CompilerParams(
            dimension_semantics=("arbitrary", "arbitrary"), vmem_limit_bytes=VMEM_LIMIT),
        name="proj_conv_mix",
    )(xb, ya_prompt, *([w_mix_bf] * N_MIX_GROUPS), conv_w, c_tail)


SUB_ROWS = 128


def _out_proj_kernel(mp_ref, mt_ref, xn_ref, w_ref, lg_ref, lb_ref, h_ref, hb_ref):
    is_prompt = pl.program_id(0) < N_PROMPT_TILES
    for r in range(0, ROW_TILE, SUB_ROWS):
        rows = slice(r, r + SUB_ROWS)
        m = jnp.where(is_prompt, mp_ref[rows, :], mt_ref[rows, :])
        mix = jnp.dot(m, w_ref[...], preferred_element_type=jnp.float32)
        h = _layer_norm(ALPHA * xn_ref[rows, :] + mix, lg_ref[...], lb_ref[...])
        h_ref[rows, :] = h
        hb_ref[rows, :] = h.astype(jnp.bfloat16)


def _out_proj(m_prompt, m_tail, xn, w_out, ln_g, ln_b):
    row = lambda i: (i, 0)
    full = lambda i: (0, 0)
    return pl.pallas_call(
        _out_proj_kernel,
        grid=(N_TILES,),
        in_specs=[
            pl.BlockSpec((ROW_TILE, D_MODEL), lambda i: (jnp.minimum(i, N_PROMPT_TILES - 1), 0)),
            pl.BlockSpec((ROW_TILE, D_MODEL), full),
            pl.BlockSpec((ROW_TILE, D_MODEL), row),
            pl.BlockSpec((D_MODEL, D_MODEL), full),
            pl.BlockSpec((1, D_MODEL), full),
            pl.BlockSpec((1, D_MODEL), full),
        ],
        out_specs=[pl.BlockSpec((ROW_TILE, D_MODEL), row), pl.BlockSpec((ROW_TILE, D_MODEL), row)],
        out_shape=[
            jax.ShapeDtypeStruct((ROWS, D_MODEL), jnp.float32),
            jax.ShapeDtypeStruct((ROWS, D_MODEL), jnp.bfloat16),
        ],
        compiler_params=pltpu.CompilerParams(
            dimension_semantics=("arbitrary",), vmem_limit_bytes=VMEM_LIMIT),
        name="out_proj_ln",
    )(m_prompt, m_tail, xn, w_out, ln_g, ln_b)


FF_TILE = 512
N_FF_TILES = D_FF // FF_TILE
FFN_UP_ROWS = 2 * BIG_ROW_TILE


N_FFN_UP_STEPS = N_FF_TILES * (ROWS // FFN_UP_ROWS)
DOWN_W_SLAB = D_FF // N_FFN_UP_STEPS


def _ffn_up_kernel(hb_ref, wg_ref, wu_ref, wd_ref, act_ref, wd_bf_ref, wg_bf, wu_bf):
    @pl.when(pl.program_id(1) == 0)
    def _cast():
        for r in range(0, D_MODEL, W_SLAB):
            wg_bf[r:r + W_SLAB, :] = wg_ref[r:r + W_SLAB, :].astype(jnp.bfloat16)
            wu_bf[r:r + W_SLAB, :] = wu_ref[r:r + W_SLAB, :].astype(jnp.bfloat16)

    for r in range(0, FFN_UP_ROWS, BIG_ROW_TILE):
        rows = slice(r, r + BIG_ROW_TILE)
        hb = hb_ref[rows, :]
        gate = jnp.dot(hb, wg_bf[...], preferred_element_type=jnp.float32)
        up = jnp.dot(hb, wu_bf[...], preferred_element_type=jnp.float32)
        if r == 0:
            wd_bf_ref[...] = wd_ref[...].astype(jnp.bfloat16)
        act_ref[rows, :] = (gate * _sigmoid(gate) * up).astype(jnp.bfloat16)


def _ffn_up(hb, w_ffn_in, w_ffn_out):
    step = lambda j, i: j * (ROWS // FFN_UP_ROWS) + i
    return pl.pallas_call(
        _ffn_up_kernel,
        grid=(N_FF_TILES, ROWS // FFN_UP_ROWS),
        in_specs=[
            pl.BlockSpec((FFN_UP_ROWS, D_MODEL), lambda j, i: (i, 0)),
            pl.BlockSpec((None, D_MODEL, FF_TILE), lambda j, i: (0, 0, j)),
            pl.BlockSpec((None, D_MODEL, FF_TILE), lambda j, i: (0, 0, N_FF_TILES + j)),
            pl.BlockSpec((None, DOWN_W_SLAB, D_MODEL), lambda j, i: (0, step(j, i), 0)),
        ],
        out_specs=[
            pl.BlockSpec((FFN_UP_ROWS, FF_TILE), lambda j, i: (i, j)),
            pl.BlockSpec((DOWN_W_SLAB, D_MODEL), lambda j, i: (step(j, i), 0)),
        ],
        out_shape=[
            jax.ShapeDtypeStruct((ROWS, D_FF), jnp.bfloat16),
            jax.ShapeDtypeStruct((D_FF, D_MODEL), jnp.bfloat16),
        ],
        scratch_shapes=[pltpu.VMEM((D_MODEL, FF_TILE), jnp.bfloat16),
                        pltpu.VMEM((D_MODEL, FF_TILE), jnp.bfloat16)],
        compiler_params=pltpu.CompilerParams(
            dimension_semantics=("arbitrary", "arbitrary"), vmem_limit_bytes=VMEM_LIMIT),
        name="ffn_up",
    )(hb, w_ffn_in, w_ffn_in, w_ffn_out)


DOWN_TILE = 256
N_DOWN_PROMPT = SEQ // DOWN_TILE
N_DOWN_TILES = ROWS // DOWN_TILE


def _ffn_down_kernel(act_ref, h_ref, wd_ref, lg_ref, lb_ref, yp_ref, yt_ref):
    i = pl.program_id(0)

    def rows_out(o_ref):
        for r in range(0, DOWN_TILE, SUB_ROWS):
            rows = slice(r, r + SUB_ROWS)
            f = jnp.dot(act_ref[rows, :], wd_ref[...], preferred_element_type=jnp.float32)
            o_ref[rows, :] = _layer_norm(ALPHA * h_ref[rows, :] + f, lg_ref[...], lb_ref[...])

    @pl.when(i < N_DOWN_PROMPT)
    def _prompt():
        rows_out(yp_ref)

    @pl.when(i >= N_DOWN_PROMPT)
    def _tail():
        rows_out(yt_ref)


def _ffn_down(act, h, w_down, ln_g, ln_b):
    row = lambda i: (i, 0)
    full = lambda i: (0, 0)
    return pl.pallas_call(
        _ffn_down_kernel,
        grid=(N_DOWN_TILES,),
        in_specs=[
            pl.BlockSpec((DOWN_TILE, D_FF), row),
            pl.BlockSpec((DOWN_TILE, D_MODEL), row),
            pl.BlockSpec((D_FF, D_MODEL), full, pipeline_mode=pl.Buffered(1)),
            pl.BlockSpec((1, D_MODEL), full),
            pl.BlockSpec((1, D_MODEL), full),
        ],
        out_specs=[
            pl.BlockSpec((None, DOWN_TILE, D_MODEL), lambda i: (0, jnp.minimum(i, N_DOWN_PROMPT - 1), 0)),
            pl.BlockSpec((DOWN_TILE, D_MODEL), lambda i: (jnp.maximum(i - N_DOWN_PROMPT, 0), 0)),
        ],
        out_shape=[
            jax.ShapeDtypeStruct((1, SEQ, D_MODEL), jnp.float32),
            jax.ShapeDtypeStruct((ROW_TILE, D_MODEL), jnp.float32),
        ],
        compiler_params=pltpu.CompilerParams(
            dimension_semantics=("arbitrary",), vmem_limit_bytes=VMEM_LIMIT),
        name="ffn_down_ln",
    )(act, h, w_down, ln_g, ln_b)


def kernel(x_prompt, x_sample, state_gla, cache_conv, meta_tokens, ln_in_g, ln_in_b, w_in, w_gate_up, b_gate, gla_norm, conv_w, w_out, ln1_g, ln1_b, w_ffn_in, w_ffn_out, ln2_g, ln2_b):
    f32, bf16 = jnp.float32, jnp.bfloat16
    tail = jnp.concatenate([
        jnp.zeros((META_ROW0, D_MODEL), f32),
        meta_tokens.astype(f32),
        x_sample.reshape(SAMPLE_ROWS, D_MODEL),
        jnp.zeros((ROW_TILE - SAMPLE_ROW0 - SAMPLE_ROWS, D_MODEL), f32),
    ], axis=0)
    w_gu = jnp.pad(w_gate_up[0], ((0, LANE - GATE_RANK), (0, 0))).astype(bf16)
    vec = lambda p: p.reshape(1, -1).astype(f32)

    w_in_t = jnp.swapaxes(w_in, 1, 2)
    xn, xb, g = _ln_gate(x_prompt, tail, vec(ln_in_g), vec(ln_in_b), w_in_t, w_gu, vec(b_gate[0]))
    ya_tail, s_meta, s_sample = _gla_tail(xb, w_in_t, g, vec(gla_norm[0]), state_gla[0])
    ya_prompt, s_prompt, w_mix_bf, w_out_bf = _gla_prompt(xb, w_in_t, g, vec(gla_norm[0]), s_meta, w_out)
    m_tail, c_tail = _mix_tail(xb, ya_tail, w_mix_bf, conv_w[0], cache_conv[0].reshape(2 * DEC_BATCH, D_MODEL))
    m_prompt, c_last = _mix_prompt(xb, ya_prompt, w_mix_bf, conv_w[0], c_tail)
    h, hb = _out_proj(m_prompt, m_tail, xn, w_out_bf, vec(ln1_g[0]), vec(ln1_b[0]))
    act, w_down_bf = _ffn_up(hb, w_ffn_in, w_ffn_out)
    y_prompt, y_tail = _ffn_down(act, h, w_down_bf, vec(ln2_g[0]), vec(ln2_b[0]))

    y_sample = y_tail[SAMPLE_ROW0:SAMPLE_ROW0 + SAMPLE_ROWS].reshape(DEC_BATCH, DEC_SEQ, D_MODEL)
    cache_prompt = c_last[6:8][None, None]
    cache_sample = c_tail[SAMPLE_ROW0:SAMPLE_ROW0 + SAMPLE_ROWS].reshape(
        DEC_BATCH, DEC_SEQ, D_MODEL)[:, DEC_SEQ - 2:][None]
    return (y_prompt, y_sample, s_prompt[None, None], cache_prompt, s_sample[None], cache_sample)
```

```python
import functools

import jax
import jax.numpy as jnp
from jax import lax
from jax.experimental import pallas as pl
from jax.experimental.pallas import tpu as pltpu

D_MODEL = 2048
SEQ = 8192
DEC_BATCH = 16
DEC_SEQ = 16
CHUNK = 64
N_META = 16
GLA_HEADS = 4
HEAD_K = 256
HEAD_V = 512
GLA_DK = GLA_HEADS * HEAD_K
GATE_RANK = 16
GATE_TEMP = 16.0
D_FF = 5632
ALPHA = 2.0 ** 0.25
LN_EPS = 1e-5
RMS_EPS = 1e-6

ROW_TILE = 512
N_PROMPT_TILES = SEQ // ROW_TILE
N_TILES = N_PROMPT_TILES + 1
ROWS = N_TILES * ROW_TILE
META_ROW0 = CHUNK - N_META
SAMPLE_ROW0 = CHUNK
SAMPLE_ROWS = DEC_BATCH * DEC_SEQ
LANE = 128
GATE_COL0 = 2 * GLA_DK + 2 * D_MODEL
VMEM_LIMIT = 58 * 1024 * 1024

_NT = (((1,), (1,)), ((), ()))
_TN = (((0,), (0,)), ((), ()))


def _tile_first_tail(i):
    return (i + N_PROMPT_TILES) % N_TILES


def _layer_norm(x, g, b):
    mu = jnp.mean(x, axis=-1, keepdims=True)
    xc = x - mu
    var = jnp.mean(xc * xc, axis=-1, keepdims=True)
    return xc * lax.rsqrt(var + LN_EPS) * g + b


def _log_sigmoid(x):
    return jnp.minimum(x, 0.0) - jnp.log1p(jnp.exp(-jnp.abs(x)))


def _sigmoid(x):
    return 1.0 / (1.0 + jnp.exp(-x))


def _ln_gate_kernel(xp_ref, xt_ref, lg_ref, lb_ref, wa_ref, wgu_ref, bg_ref, xn_ref, xb_ref, g_ref):
    x = jnp.where(pl.program_id(0) < N_PROMPT_TILES, xp_ref[...], xt_ref[...])
    xn = _layer_norm(x, lg_ref[...], lb_ref[...])
    xn_ref[...] = xn
    xb = xn.astype(jnp.bfloat16)
    xb_ref[...] = xb
    a = lax.dot_general(xb, wa_ref[...].astype(jnp.bfloat16), _NT, preferred_element_type=jnp.float32)
    z = jnp.dot(a.astype(jnp.bfloat16), wgu_ref[...], preferred_element_type=jnp.float32)
    g_ref[...] = _log_sigmoid(z + bg_ref[...]) * (1.0 / GATE_TEMP)


def _ln_gate(x_prompt, x_tail, ln_g, ln_b, w_in_t, w_gu, b_gate):
    row = lambda i: (i, 0)
    full = lambda i: (0, 0)
    return pl.pallas_call(
        _ln_gate_kernel,
        grid=(N_TILES,),
        in_specs=[
            pl.BlockSpec((None, ROW_TILE, D_MODEL), lambda i: (0, jnp.minimum(i, N_PROMPT_TILES - 1), 0)),
            pl.BlockSpec((ROW_TILE, D_MODEL), full),
            pl.BlockSpec((1, D_MODEL), full),
            pl.BlockSpec((1, D_MODEL), full),
            pl.BlockSpec((None, LANE, D_MODEL), lambda i: (0, GATE_COL0 // LANE, 0)),
            pl.BlockSpec((LANE, GLA_DK), full),
            pl.BlockSpec((1, GLA_DK), full),
        ],
        out_specs=[
            pl.BlockSpec((ROW_TILE, D_MODEL), row),
            pl.BlockSpec((ROW_TILE, D_MODEL), row),
            pl.BlockSpec((ROW_TILE, GLA_DK), row),
        ],
        out_shape=[
            jax.ShapeDtypeStruct((ROWS, D_MODEL), jnp.float32),
            jax.ShapeDtypeStruct((ROWS, D_MODEL), jnp.bfloat16),
            jax.ShapeDtypeStruct((ROWS, GLA_DK), jnp.float32),
        ],
        compiler_params=pltpu.CompilerParams(
            dimension_semantics=("arbitrary",), vmem_limit_bytes=VMEM_LIMIT),
        name="ln_gate",
    )(x_prompt, x_tail, ln_g, ln_b, w_in_t, w_gu, b_gate)


W_SLAB = 256
BIG_ROW_TILE = 1088
N_BIG_TILES = ROWS // BIG_ROW_TILE


PROMPT_CHUNK = 256
HEAD_ROWS = 2 * HEAD_K + 2 * HEAD_V


def _cast_head_weights(w_refs, wbf):
    off = 0
    for w_ref in w_refs:
        for r in range(0, w_ref.shape[0], W_SLAB):
            wbf[off + r:off + r + W_SLAB, :] = w_ref[r:r + W_SLAB, :].astype(jnp.bfloat16)
        off += w_ref.shape[0]


def _project_head(x, wbf):
    qk = lax.dot_general(x, wbf[0:2 * HEAD_K, :], _NT, preferred_element_type=jnp.float32)
    vr = lax.dot_general(x, wbf[2 * HEAD_K:, :], _NT, preferred_element_type=jnp.float32)
    return qk[:, :HEAD_K], qk[:, HEAD_K:], vr[:, :HEAD_V], vr[:, HEAD_V:]


def _head_weight_specs(head_of=lambda h, *_: h, **vr_kw):
    kq = GLA_DK // HEAD_K
    return [
        pl.BlockSpec((None, HEAD_K, D_MODEL), lambda *ix: (0, head_of(*ix), 0)),
        pl.BlockSpec((None, HEAD_K, D_MODEL), lambda *ix: (0, kq + head_of(*ix), 0)),
        pl.BlockSpec((None, HEAD_V, D_MODEL), lambda *ix: (0, kq + head_of(*ix), 0), **vr_kw),
        pl.BlockSpec((None, HEAD_V, D_MODEL), lambda *ix: (0, 2 * kq + head_of(*ix), 0), **vr_kw),
    ]


def _row_of_block(b, block, r):
    n, w = b.shape
    b3 = b.reshape(n // block, block, w)
    return jnp.broadcast_to(b3[:, r:r + 1, :], b3.shape).reshape(n, w)


def _half_sizes(chunk):
    return tuple(chunk >> (j + 1) for j in range(chunk.bit_length() - 1))


def _gla_prepare(q, k, g, chunk):
    n, w = q.shape
    row = lax.broadcasted_iota(jnp.int32, (n, w), 0)
    pos = row & (chunk - 1)
    b = g
    for sh in reversed(_half_sizes(chunk)):
        b = b + jnp.where(pos >= sh, pltpu.roll(b, sh, 0), 0.0)
    qs = q * (HEAD_K ** -0.5)
    b_last = _row_of_block(b, chunk, chunk - 1)
    q_inter = (qs * jnp.exp(b)).astype(jnp.bfloat16)
    k_dec = (k * jnp.exp(b_last - b)).astype(jnp.bfloat16)
    levels = []
    for s in _half_sizes(chunk):
        upper = (row & (2 * s - 1)) >= s
        if s >= 4:
            b_mid = _row_of_block(b, 2 * s, s - 1)
            d = jnp.where(upper, b - b_mid, b_mid - b)
        elif s == 2:
            p4 = row & 3
            g_prev = pltpu.roll(g, 1, 0)
            g_next = pltpu.roll(g, n - 1, 0)
            d = jnp.where(p4 == 0, g_next, jnp.where(p4 == 1, 0.0, jnp.where(p4 == 2, g, g + g_prev)))
        else:
            d = jnp.where(upper, g, 0.0)
        x = (jnp.where(upper, qs, k) * jnp.exp(d)).astype(jnp.bfloat16)
        levels.append((x, x))
    levels.append((qs.astype(jnp.bfloat16), k.astype(jnp.bfloat16)))
    return q_inter, k_dec, b, levels


def _score_masks(chunk):
    ii = lax.broadcasted_iota(jnp.int32, (chunk, chunk), 0)
    jj = lax.broadcasted_iota(jnp.int32, (chunk, chunk), 1)
    masks = []
    for s in _half_sizes(chunk):
        blk = 2 * s
        same = (ii & ~(blk - 1)) == (jj & ~(blk - 1))
        masks.append(same & ((ii & (blk - 1)) >= s) & ((jj & (blk - 1)) < s))
    masks.append(ii == jj)
    return masks


def _gla_chunk(c, chunk, q_inter, k_dec, b, levels, v_bf, masks, s_val):
    r0 = c * chunk
    sl = slice(r0, r0 + chunk)
    scores = jnp.zeros((chunk, chunk), jnp.float32)
    for (qe, ke), m in zip(levels, masks):
        p = lax.dot_general(qe[sl], ke[sl], _NT, preferred_element_type=jnp.float32)
        scores = jnp.where(m, p, scores)
    vc = v_bf[sl]
    o = jnp.dot(scores.astype(jnp.bfloat16), vc, preferred_element_type=jnp.float32)
    o = o + jnp.dot(q_inter[sl], s_val.astype(jnp.bfloat16), preferred_element_type=jnp.float32)
    eb = jnp.exp(b[r0 + chunk - 1:r0 + chunk, :])
    eb_t = jnp.transpose(jnp.broadcast_to(eb, (LANE, HEAD_K)))
    decay = jnp.concatenate([eb_t] * (HEAD_V // LANE), axis=1)
    ds = lax.dot_general(k_dec[sl], vc, _TN, preferred_element_type=jnp.float32)
    return o, decay * s_val + ds


def _gla_finish(o, r, gn):
    o = o * lax.rsqrt(jnp.mean(o * o, axis=-1, keepdims=True) + RMS_EPS)
    return o * gn * (r * _sigmoid(r))


TAIL_HALF = DEC_BATCH // 2
TAIL_SPLIT_ROW = SAMPLE_ROW0 + TAIL_HALF * DEC_SEQ


def _gla_tail_kernel(x_ref, wq_ref, wk_ref, wv_ref, wr_ref, g_ref, gn_ref, sin_ref,
                     y_ref, sm_ref, ss_ref, wbf, u_scr):
    half = pl.program_id(1)
    masks = _score_masks(CHUNK)
    kv0, r0 = 2 * HEAD_K, 2 * HEAD_K + HEAD_V

    @pl.when(half == 0)
    def _project():
        _cast_head_weights((wq_ref, wk_ref, wv_ref, wr_ref), wbf)
        q, k, v, r = _project_head(x_ref[...], wbf)
        u_scr[...] = jnp.concatenate([q, k, v, r], axis=1)

    def run(first_stream, with_meta, row_lo, row_hi):
        def padded(c0, width):
            parts = [u_scr[0:CHUNK, c0:c0 + width]] if with_meta else []
            zeros = jnp.zeros((META_ROW0, width), jnp.float32)
            for s in range(first_stream, first_stream + TAIL_HALF):
                lo = SAMPLE_ROW0 + s * DEC_SEQ
                parts += [zeros, u_scr[lo:lo + DEC_SEQ, c0:c0 + width]]
            return jnp.concatenate(parts, axis=0)

        def padded_g():
            parts = [g_ref[0:CHUNK, :]] if with_meta else []
            zeros = jnp.zeros((META_ROW0, HEAD_K), jnp.float32)
            for s in range(first_stream, first_stream + TAIL_HALF):
                lo = SAMPLE_ROW0 + s * DEC_SEQ
                parts += [zeros, g_ref[lo:lo + DEC_SEQ, :]]
            return jnp.concatenate(parts, axis=0)

        q = padded(0, HEAD_K)
        k = padded(HEAD_K, HEAD_K)
        v = padded(kv0, HEAD_V)
        g = padded_g()
        live = (lax.broadcasted_iota(jnp.int32, q.shape, 0) & (CHUNK - 1)) >= META_ROW0
        k = jnp.where(live, k, 0.0)
        g = jnp.where(live, g, 0.0)
        q_inter, k_dec, b, levels = _gla_prepare(q, k, g, CHUNK)
        v_bf = v.astype(jnp.bfloat16)
        outs, c = [], 0
        if with_meta:
            outs.append(jnp.zeros((META_ROW0, HEAD_V), jnp.float32))
            o, s_meta = _gla_chunk(0, CHUNK, q_inter, k_dec, b, levels, v_bf, masks,
                                   jnp.zeros((HEAD_K, HEAD_V), jnp.float32))
            sm_ref[...] = s_meta
            outs.append(o[META_ROW0:])
            c = 1
        for s in range(TAIL_HALF):
            o, s_new = _gla_chunk(c + s, CHUNK, q_inter, k_dec, b, levels, v_bf, masks, sin_ref[s])
            ss_ref[s] = s_new
            outs.append(o[META_ROW0:])
        if not with_meta:
            outs.append(jnp.zeros((ROW_TILE - SAMPLE_ROW0 - SAMPLE_ROWS, HEAD_V), jnp.float32))
        y_ref[row_lo:row_hi, :] = _gla_finish(
            jnp.concatenate(outs, axis=0), u_scr[row_lo:row_hi, r0:r0 + HEAD_V], gn_ref[...])

    @pl.when(half == 0)
    def _first():
        run(0, True, 0, TAIL_SPLIT_ROW)

    @pl.when(half == 1)
    def _second():
        run(TAIL_HALF, False, TAIL_SPLIT_ROW, ROW_TILE)


def _gla_tail(xb, w_in_t, g, gla_norm, state_gla):
    states = pl.BlockSpec((TAIL_HALF, None, HEAD_K, HEAD_V), lambda h, half: (half, h, 0, 0))
    next_head_early = lambda h, half: jnp.minimum(h + half, GLA_HEADS - 1)
    return pl.pallas_call(
        _gla_tail_kernel,
        grid=(GLA_HEADS, 2),
        in_specs=[
            pl.BlockSpec((ROW_TILE, D_MODEL), lambda h, half: (N_PROMPT_TILES, 0)),
            *_head_weight_specs(next_head_early),
            pl.BlockSpec((ROW_TILE, HEAD_K), lambda h, half: (N_PROMPT_TILES, h)),
            pl.BlockSpec((1, HEAD_V), lambda h, half: (0, h)),
            states,
        ],
        out_specs=[
            pl.BlockSpec((ROW_TILE, HEAD_V), lambda h, half: (0, h)),
            pl.BlockSpec((None, HEAD_K, HEAD_V), lambda h, half: (h, 0, 0)),
            states,
        ],
        out_shape=[
            jax.ShapeDtypeStruct((ROW_TILE, D_MODEL), jnp.float32),
            jax.ShapeDtypeStruct((GLA_HEADS, HEAD_K, HEAD_V), jnp.float32),
            jax.ShapeDtypeStruct((DEC_BATCH, GLA_HEADS, HEAD_K, HEAD_V), jnp.float32),
        ],
        scratch_shapes=[pltpu.VMEM((HEAD_ROWS, D_MODEL), jnp.bfloat16),
                        pltpu.VMEM((ROW_TILE, HEAD_ROWS), jnp.float32)],
        compiler_params=pltpu.CompilerParams(
            dimension_semantics=("arbitrary", "arbitrary"), vmem_limit_bytes=VMEM_LIMIT),
        name="gla_tail",
    )(xb, *([w_in_t] * 4), g, gla_norm, state_gla)


GLA_STEP_ROWS = 2 * ROW_TILE
N_GLA_ROW_STEPS = SEQ // GLA_STEP_ROWS
N_GLA_STEPS = GLA_HEADS * N_GLA_ROW_STEPS
MIX_W_ROWS = 5 * D_MODEL
MIX_W_SLAB = MIX_W_ROWS // N_GLA_STEPS
OUT_W_SLAB = D_MODEL // N_GLA_STEPS


def _gla_prompt_kernel(x_ref, wq_ref, wk_ref, wv_ref, wr_ref, g_ref, gn_ref, s0_ref, wmix_ref, wout_ref,
                       y_ref, sp_ref, wmix_bf_ref, wout_bf_ref, wbf, s_scr):
    i = pl.program_id(1)

    @pl.when(i == 0)
    def _start_head():
        _cast_head_weights((wq_ref, wk_ref, wv_ref, wr_ref), wbf)
        s_scr[...] = s0_ref[...]

    masks = _score_masks(PROMPT_CHUNK)
    for r0 in range(0, GLA_STEP_ROWS, ROW_TILE):
        rows = slice(r0, r0 + ROW_TILE)
        q, k, v, r = _project_head(x_ref[rows, :], wbf)
        if r0 == 0:
            wmix_bf_ref[...] = wmix_ref[0].astype(jnp.bfloat16)
            wout_bf_ref[...] = wout_ref[...].astype(jnp.bfloat16)
        q_inter, k_dec, b, levels = _gla_prepare(q, k, g_ref[rows, :], PROMPT_CHUNK)
        v_bf = v.astype(jnp.bfloat16)
        outs = []
        for c in range(ROW_TILE // PROMPT_CHUNK):
            o, s_new = _gla_chunk(c, PROMPT_CHUNK, q_inter, k_dec, b, levels, v_bf, masks, s_scr[...])
            s_scr[...] = s_new
            outs.append(o)
        y_ref[rows, :] = _gla_finish(jnp.concatenate(outs, axis=0), r, gn_ref[...])

    @pl.when(i == N_GLA_ROW_STEPS - 1)
    def _final_state():
        sp_ref[...] = s_scr[...]


def _gla_prompt(xb, w_in_t, g, gla_norm, s_meta, w_out):
    step = lambda h, i: h * N_GLA_ROW_STEPS + i
    return pl.pallas_call(
        _gla_prompt_kernel,
        grid=(GLA_HEADS, N_GLA_ROW_STEPS),
        in_specs=[
            pl.BlockSpec((GLA_STEP_ROWS, D_MODEL), lambda h, i: (i, 0)),
            *_head_weight_specs(pipeline_mode=pl.Buffered(1)),
            pl.BlockSpec((GLA_STEP_ROWS, HEAD_K), lambda h, i: (i, h)),
            pl.BlockSpec((1, HEAD_V), lambda h, i: (0, h)),
            pl.BlockSpec((None, HEAD_K, HEAD_V), lambda h, i: (h, 0, 0)),
            pl.BlockSpec((pl.Element(1), pl.Element(MIX_W_SLAB), pl.Element(D_MODEL)),
                         lambda h, i: (0, pl.multiple_of(GATE_COL0 + GATE_RANK + step(h, i) * MIX_W_SLAB,
                                                         GATE_RANK), 0)),
            pl.BlockSpec((None, OUT_W_SLAB, D_MODEL), lambda h, i: (0, step(h, i), 0)),
        ],
        out_specs=[
            pl.BlockSpec((GLA_STEP_ROWS, HEAD_V), lambda h, i: (i, h)),
            pl.BlockSpec((None, HEAD_K, HEAD_V), lambda h, i: (h, 0, 0)),
            pl.BlockSpec((MIX_W_SLAB, D_MODEL), lambda h, i: (step(h, i), 0)),
            pl.BlockSpec((OUT_W_SLAB, D_MODEL), lambda h, i: (step(h, i), 0)),
        ],
        out_shape=[
            jax.ShapeDtypeStruct((SEQ, D_MODEL), jnp.float32),
            jax.ShapeDtypeStruct((GLA_HEADS, HEAD_K, HEAD_V), jnp.float32),
            jax.ShapeDtypeStruct((MIX_W_ROWS, D_MODEL), jnp.bfloat16),
            jax.ShapeDtypeStruct((D_MODEL, D_MODEL), jnp.bfloat16),
        ],
        scratch_shapes=[pltpu.VMEM((HEAD_ROWS, D_MODEL), jnp.bfloat16),
                        pltpu.VMEM((HEAD_K, HEAD_V), jnp.float32)],
        compiler_params=pltpu.CompilerParams(
            dimension_semantics=("arbitrary", "arbitrary"), vmem_limit_bytes=VMEM_LIMIT),
        name="gla_prompt",
    )(xb, *([w_in_t] * 4), g, gla_norm, s_meta, w_in_t, w_out)


MIX_TN = 512
N_MIX_GROUPS = 5


MIX_STEP_ROWS = 4 * ROW_TILE
N_MIX_ROW_STEPS = SEQ // MIX_STEP_ROWS


def _mix_products(x, w_refs):
    return [lax.dot_general(x, w_ref[...], _NT, preferred_element_type=jnp.float32) for w_ref in w_refs]


def _mix_finish(c, p1, p2, cb, ga, gb, ya, cw_ref):
    conv = cw_ref[0:1, :] * p2 + cw_ref[1:2, :] * p1 + cw_ref[2:3, :] * c
    return (_sigmoid(ga) * ya + _sigmoid(gb) * (cb * conv)).astype(jnp.bfloat16)


def _mix_tail_kernel(x_ref, ya_ref, wcb_ref, wcc_ref, wcx_ref, wga_ref, wgb_ref, cw_ref, cache_ref,
                     m_ref, ctail_ref):
    cb, cc, cx, ga, gb = _mix_products(x_ref[...], (wcb_ref, wcc_ref, wcx_ref, wga_ref, wgb_ref))
    c = cc * cx
    row = lax.broadcasted_iota(jnp.int32, (ROW_TILE, MIX_TN), 0)
    prev1 = pltpu.roll(c, 1, 0)
    prev2 = pltpu.roll(c, 2, 0)

    def stream_rows(j):
        parts = [jnp.zeros((SAMPLE_ROW0, MIX_TN), jnp.float32)]
        for s in range(DEC_BATCH):
            parts.append(jnp.broadcast_to(cache_ref[2 * s + j:2 * s + j + 1, :], (DEC_SEQ, MIX_TN)))
        parts.append(jnp.zeros((ROW_TILE - SAMPLE_ROW0 - SAMPLE_ROWS, MIX_TN), jnp.float32))
        return jnp.concatenate(parts, axis=0)

    old = stream_rows(0)
    new = stream_rows(1)
    in_sample = (row >= SAMPLE_ROW0) & (row < SAMPLE_ROW0 + SAMPLE_ROWS)
    p16 = (row - SAMPLE_ROW0) & (DEC_SEQ - 1)
    first = in_sample & (p16 == 0)
    second = in_sample & (p16 == 1)
    p1 = jnp.where(row == META_ROW0, 0.0, jnp.where(first, new, prev1))
    p2 = jnp.where((row == META_ROW0) | (row == META_ROW0 + 1), 0.0,
                   jnp.where(first, old, jnp.where(second, new, prev2)))
    m_ref[...] = _mix_finish(c, p1, p2, cb, ga, gb, ya_ref[...], cw_ref)
    ctail_ref[...] = c


def _mix_prompt_kernel(x_ref, ya_ref, wcb_ref, wcc_ref, wcx_ref, wga_ref, wgb_ref, cw_ref, c0_ref,
                       m_ref, clast_ref, carry):
    i = pl.program_id(1)

    @pl.when(i == 0)
    def _start():
        carry[...] = c0_ref[...]

    hist = carry[...]
    row = lax.broadcasted_iota(jnp.int32, (ROW_TILE, MIX_TN), 0)
    for r0 in range(0, MIX_STEP_ROWS, ROW_TILE):
        rows = slice(r0, r0 + ROW_TILE)
        cb, cc, cx, ga, gb = _mix_products(x_ref[rows, :], (wcb_ref, wcc_ref, wcx_ref, wga_ref, wgb_ref))
        c = cc * cx
        last = jnp.broadcast_to(hist[7:8], c.shape)
        p1 = jnp.where(row == 0, last, pltpu.roll(c, 1, 0))
        p2 = jnp.where(row == 0, jnp.broadcast_to(hist[6:7], c.shape),
                       jnp.where(row == 1, last, pltpu.roll(c, 2, 0)))
        m_ref[rows, :] = _mix_finish(c, p1, p2, cb, ga, gb, ya_ref[rows, :], cw_ref)
        hist = c[ROW_TILE - 8:]
    carry[...] = hist

    @pl.when(i == N_MIX_ROW_STEPS - 1)
    def _last():
        clast_ref[...] = hist


def _mix_weight_specs(index_args):
    nc = D_MODEL // MIX_TN
    return [pl.BlockSpec((MIX_TN, D_MODEL), functools.partial(index_args, grp * nc))
            for grp in range(N_MIX_GROUPS)]


def _mix_tail(xb, ya_tail, w_mix_bf, conv_w, cache_rows):
    top = lambda cj: (0, cj)
    return pl.pallas_call(
        _mix_tail_kernel,
        grid=(D_MODEL // MIX_TN,),
        in_specs=[
            pl.BlockSpec((ROW_TILE, D_MODEL), lambda cj: (N_PROMPT_TILES, 0)),
            pl.BlockSpec((ROW_TILE, MIX_TN), top),
            *_mix_weight_specs(lambda base, cj: (base + cj, 0)),
            pl.BlockSpec((3, MIX_TN), top),
            pl.BlockSpec((2 * DEC_BATCH, MIX_TN), top),
        ],
        out_specs=[pl.BlockSpec((ROW_TILE, MIX_TN), top), pl.BlockSpec((ROW_TILE, MIX_TN), top)],
        out_shape=[
            jax.ShapeDtypeStruct((ROW_TILE, D_MODEL), jnp.bfloat16),
            jax.ShapeDtypeStruct((ROW_TILE, D_MODEL), jnp.float32),
        ],
        compiler_params=pltpu.CompilerParams(
            dimension_semantics=("arbitrary",), vmem_limit_bytes=VMEM_LIMIT),
        name="proj_conv_mix_tail",
    )(xb, ya_tail, *([w_mix_bf] * N_MIX_GROUPS), conv_w, cache_rows)


def _mix_prompt(xb, ya_prompt, w_mix_bf, conv_w, c_tail):
    top = lambda cj, i: (0, cj)
    return pl.pallas_call(
        _mix_prompt_kernel,
        grid=(D_MODEL // MIX_TN, N_MIX_ROW_STEPS),
        in_specs=[
            pl.BlockSpec((MIX_STEP_ROWS, D_MODEL), lambda cj, i: (i, 0)),
            pl.BlockSpec((MIX_STEP_ROWS, MIX_TN), lambda cj, i: (i, cj)),
            *_mix_weight_specs(lambda base, cj, i: (base + cj, 0)),
            pl.BlockSpec((3, MIX_TN), top),
            pl.BlockSpec((8, MIX_TN), lambda cj, i: (CHUNK // 8 - 1, cj)),
        ],
        out_specs=[pl.BlockSpec((MIX_STEP_ROWS, MIX_TN), lambda cj, i: (i, cj)), pl.BlockSpec((8, MIX_TN), top)],
        out_shape=[
            jax.ShapeDtypeStruct((SEQ, D_MODEL), jnp.bfloat16),
            jax.ShapeDtypeStruct((8, D_MODEL), jnp.float32),
        ],
        scratch_shapes=[pltpu.VMEM((8, MIX_TN), jnp.float32)],
        compiler_params=pltpu.CompilerParams(
            dimension_semantics=("arbitrary", "arbitrary"), vmem_limit_bytes=VMEM_LIMIT),
        name="proj_conv_mix",
    )(xb, ya_prompt, *([w_mix_bf] * N_MIX_GROUPS), conv_w, c_tail)


SUB_ROWS = 128


def _out_proj_kernel(mp_ref, mt_ref, xn_ref, w_ref, lg_ref, lb_ref, h_ref, hb_ref):
    is_prompt = pl.program_id(0) < N_PROMPT_TILES
    for r in range(0, ROW_TILE, SUB_ROWS):
        rows = slice(r, r + SUB_ROWS)
        m = jnp.where(is_prompt, mp_ref[rows, :], mt_ref[rows, :])
        mix = jnp.dot(m, w_ref[...], preferred_element_type=jnp.float32)
        h = _layer_norm(ALPHA * xn_ref[rows, :] + mix, lg_ref[...], lb_ref[...])
        h_ref[rows, :] = h
        hb_ref[rows, :] = h.astype(jnp.bfloat16)


def _out_proj(m_prompt, m_tail, xn, w_out, ln_g, ln_b):
    row = lambda i: (i, 0)
    full = lambda i: (0, 0)
    return pl.pallas_call(
        _out_proj_kernel,
        grid=(N_TILES,),
        in_specs=[
            pl.BlockSpec((ROW_TILE, D_MODEL), lambda i: (jnp.minimum(i, N_PROMPT_TILES - 1), 0)),
            pl.BlockSpec((ROW_TILE, D_MODEL), full),
            pl.BlockSpec((ROW_TILE, D_MODEL), row),
            pl.BlockSpec((D_MODEL, D_MODEL), full),
            pl.BlockSpec((1, D_MODEL), full),
            pl.BlockSpec((1, D_MODEL), full),
        ],
        out_specs=[pl.BlockSpec((ROW_TILE, D_MODEL), row), pl.BlockSpec((ROW_TILE, D_MODEL), row)],
        out_shape=[
            jax.ShapeDtypeStruct((ROWS, D_MODEL), jnp.float32),
            jax.ShapeDtypeStruct((ROWS, D_MODEL), jnp.bfloat16),
        ],
        compiler_params=pltpu.CompilerParams(
            dimension_semantics=("arbitrary",), vmem_limit_bytes=VMEM_LIMIT),
        name="out_proj_ln",
    )(m_prompt, m_tail, xn, w_out, ln_g, ln_b)


FF_TILE = 512
N_FF_TILES = D_FF // FF_TILE
FFN_UP_ROWS = 2 * BIG_ROW_TILE


N_FFN_UP_STEPS = N_FF_TILES * (ROWS // FFN_UP_ROWS)
DOWN_W_SLAB = D_FF // N_FFN_UP_STEPS


def _ffn_up_kernel(hb_ref, wg_ref, wu_ref, wd_ref, act_ref, wd_bf_ref, wg_bf, wu_bf):
    @pl.when(pl.program_id(1) == 0)
    def _cast():
        for r in range(0, D_MODEL, W_SLAB):
            wg_bf[r:r + W_SLAB, :] = wg_ref[r:r + W_SLAB, :].astype(jnp.bfloat16)
            wu_bf[r:r + W_SLAB, :] = wu_ref[r:r + W_SLAB, :].astype(jnp.bfloat16)

    for r in range(0, FFN_UP_ROWS, BIG_ROW_TILE):
        rows = slice(r, r + BIG_ROW_TILE)
        hb = hb_ref[rows, :]
        gate = jnp.dot(hb, wg_bf[...], preferred_element_type=jnp.float32)
        up = jnp.dot(hb, wu_bf[...], preferred_element_type=jnp.float32)
        if r == 0:
            wd_bf_ref[...] = wd_ref[...].astype(jnp.bfloat16)
        act_ref[rows, :] = (gate * _sigmoid(gate) * up).astype(jnp.bfloat16)


def _ffn_up(hb, w_ffn_in, w_ffn_out):
    step = lambda j, i: j * (ROWS // FFN_UP_ROWS) + i
    return pl.pallas_call(
        _ffn_up_kernel,
        grid=(N_FF_TILES, ROWS // FFN_UP_ROWS),
        in_specs=[
            pl.BlockSpec((FFN_UP_ROWS, D_MODEL), lambda j, i: (i, 0)),
            pl.BlockSpec((None, D_MODEL, FF_TILE), lambda j, i: (0, 0, j)),
            pl.BlockSpec((None, D_MODEL, FF_TILE), lambda j, i: (0, 0, N_FF_TILES + j)),
            pl.BlockSpec((None, DOWN_W_SLAB, D_MODEL), lambda j, i: (0, step(j, i), 0)),
        ],
        out_specs=[
            pl.BlockSpec((FFN_UP_ROWS, FF_TILE), lambda j, i: (i, j)),
            pl.BlockSpec((DOWN_W_SLAB, D_MODEL), lambda j, i: (step(j, i), 0)),
        ],
        out_shape=[
            jax.ShapeDtypeStruct((ROWS, D_FF), jnp.bfloat16),
            jax.ShapeDtypeStruct((D_FF, D_MODEL), jnp.bfloat16),
        ],
        scratch_shapes=[pltpu.VMEM((D_MODEL, FF_TILE), jnp.bfloat16),
                        pltpu.VMEM((D_MODEL, FF_TILE), jnp.bfloat16)],
        compiler_params=pltpu.CompilerParams(
            dimension_semantics=("arbitrary", "arbitrary"), vmem_limit_bytes=VMEM_LIMIT),
        name="ffn_up",
    )(hb, w_ffn_in, w_ffn_in, w_ffn_out)


DOWN_TILE = 256
N_DOWN_PROMPT = SEQ // DOWN_TILE
N_DOWN_TILES = ROWS // DOWN_TILE


def _ffn_down_kernel(act_ref, h_ref, wd_ref, lg_ref, lb_ref, yp_ref, yt_ref):
    i = pl.program_id(0)

    def rows_out(o_ref):
        for r in range(0, DOWN_TILE, SUB_ROWS):
            rows = slice(r, r + SUB_ROWS)
            f = jnp.dot(act_ref[rows, :], wd_ref[...], preferred_element_type=jnp.float32)
            o_ref[rows, :] = _layer_norm(ALPHA * h_ref[rows, :] + f, lg_ref[...], lb_ref[...])

    @pl.when(i < N_DOWN_PROMPT)
    def _prompt():
        rows_out(yp_ref)

    @pl.when(i >= N_DOWN_PROMPT)
    def _tail():
        rows_out(yt_ref)


def _ffn_down(act, h, w_down, ln_g, ln_b):
    row = lambda i: (i, 0)
    full = lambda i: (0, 0)
    return pl.pallas_call(
        _ffn_down_kernel,
        grid=(N_DOWN_TILES,),
        in_specs=[
            pl.BlockSpec((DOWN_TILE, D_FF), row),
            pl.BlockSpec((DOWN_TILE, D_MODEL), row),
            pl.BlockSpec((D_FF, D_MODEL), full, pipeline_mode=pl.Buffered(1)),
            pl.BlockSpec((1, D_MODEL), full),
            pl.BlockSpec((1, D_MODEL), full),
        ],
        out_specs=[
            pl.BlockSpec((None, DOWN_TILE, D_MODEL), lambda i: (0, jnp.minimum(i, N_DOWN_PROMPT - 1), 0)),
            pl.BlockSpec((DOWN_TILE, D_MODEL), lambda i: (jnp.maximum(i - N_DOWN_PROMPT, 0), 0)),
        ],
        out_shape=[
            jax.ShapeDtypeStruct((1, SEQ, D_MODEL), jnp.float32),
            jax.ShapeDtypeStruct((ROW_TILE, D_MODEL), jnp.float32),
        ],
        compiler_params=pltpu.CompilerParams(
            dimension_semantics=("arbitrary",), vmem_limit_bytes=VMEM_LIMIT),
        name="ffn_down_ln",
    )(act, h, w_down, ln_g, ln_b)


def kernel(x_prompt, x_sample, state_gla, cache_conv, meta_tokens, ln_in_g, ln_in_b, w_in, w_gate_up, b_gate, gla_norm, conv_w, w_out, ln1_g, ln1_b, w_ffn_in, w_ffn_out, ln2_g, ln2_b):
    f32, bf16 = jnp.float32, jnp.bfloat16
    tail = jnp.concatenate([
        jnp.zeros((META_ROW0, D_MODEL), f32),
        meta_tokens.astype(f32),
        x_sample.reshape(SAMPLE_ROWS, D_MODEL),
        jnp.zeros((ROW_TILE - SAMPLE_ROW0 - SAMPLE_ROWS, D_MODEL), f32),
    ], axis=0)
    w_gu = jnp.pad(w_gate_up[0], ((0, LANE - GATE_RANK), (0, 0))).astype(bf16)
    vec = lambda p: p.reshape(1, -1).astype(f32)

    w_in_t = jnp.swapaxes(w_in, 1, 2)
    xn, xb, g = _ln_gate(x_prompt, tail, vec(ln_in_g), vec(ln_in_b), w_in_t, w_gu, vec(b_gate[0]))
    ya_tail, s_meta, s_sample = _gla_tail(xb, w_in_t, g, vec(gla_norm[0]), state_gla[0])
    ya_prompt, s_prompt, w_mix_bf, w_out_bf = _gla_prompt(xb, w_in_t, g, vec(gla_norm[0]), s_meta, w_out)
    m_tail, c_tail = _mix_tail(xb, ya_tail, w_mix_bf, conv_w[0], cache_conv[0].reshape(2 * DEC_BATCH, D_MODEL))
    m_prompt, c_last = _mix_prompt(xb, ya_prompt, w_mix_bf, conv_w[0], c_tail)
    h, hb = _out_proj(m_prompt, m_tail, xn, w_out_bf, vec(ln1_g[0]), vec(ln1_b[0]))
    act, w_down_bf = _ffn_up(hb, w_ffn_in, w_ffn_out)
    y_prompt, y_tail = _ffn_down(act, h, w_down_bf, vec(ln2_g[0]), vec(ln2_b[0]))

    y_sample = y_tail[SAMPLE_ROW0:SAMPLE_ROW0 + SAMPLE_ROWS].reshape(DEC_BATCH, DEC_SEQ, D_MODEL)
    cache_prompt = c_last[6:8][None, None]
    cache_sample = c_tail[SAMPLE_ROW0:SAMPLE_ROW0 + SAMPLE_ROWS].reshape(
        DEC_BATCH, DEC_SEQ, D_MODEL)[:, DEC_SEQ - 2:][None]
    return (y_prompt, y_sample, s_prompt[None, None], cache_prompt, s_sample[None], cache_sample)
```

```python
import functools

import jax
import jax.numpy as jnp
from jax import lax
from jax.experimental import pallas as pl
from jax.experimental.pallas import tpu as pltpu

D_MODEL = 2048
SEQ = 8192
DEC_BATCH = 16
DEC_SEQ = 16
CHUNK = 64
N_META = 16
GLA_HEADS = 4
HEAD_K = 256
HEAD_V = 512
GLA_DK = GLA_HEADS * HEAD_K
GATE_RANK = 16
GATE_TEMP = 16.0
D_FF = 5632
ALPHA = 2.0 ** 0.25
LN_EPS = 1e-5
RMS_EPS = 1e-6

ROW_TILE = 512
N_PROMPT_TILES = SEQ // ROW_TILE
N_TILES = N_PROMPT_TILES + 1
ROWS = N_TILES * ROW_TILE
META_ROW0 = CHUNK - N_META
SAMPLE_ROW0 = CHUNK
SAMPLE_ROWS = DEC_BATCH * DEC_SEQ
LANE = 128
GATE_COL0 = 2 * GLA_DK + 2 * D_MODEL
VMEM_LIMIT = 56 * 1024 * 1024

_NT = (((1,), (1,)), ((), ()))
_TN = (((0,), (0,)), ((), ()))


def _tile_first_tail(i):
    return (i + N_PROMPT_TILES) % N_TILES


def _layer_norm(x, g, b):
    mu = jnp.mean(x, axis=-1, keepdims=True)
    xc = x - mu
    var = jnp.mean(xc * xc, axis=-1, keepdims=True)
    return xc * lax.rsqrt(var + LN_EPS) * g + b


def _log_sigmoid(x):
    return jnp.minimum(x, 0.0) - jnp.log1p(jnp.exp(-jnp.abs(x)))


def _sigmoid(x):
    return 1.0 / (1.0 + jnp.exp(-x))


def _ln_gate_kernel(xp_ref, xt_ref, lg_ref, lb_ref, wa_ref, wgu_ref, bg_ref, xn_ref, xb_ref, g_ref):
    x = jnp.where(pl.program_id(0) < N_PROMPT_TILES, xp_ref[...], xt_ref[...])
    xn = _layer_norm(x, lg_ref[...], lb_ref[...])
    xn_ref[...] = xn
    xb = xn.astype(jnp.bfloat16)
    xb_ref[...] = xb
    a = lax.dot_general(xb, wa_ref[...].astype(jnp.bfloat16), _NT, preferred_element_type=jnp.float32)
    z = jnp.dot(a.astype(jnp.bfloat16), wgu_ref[...], preferred_element_type=jnp.float32)
    g_ref[...] = _log_sigmoid(z + bg_ref[...]) * (1.0 / GATE_TEMP)


def _ln_gate(x_prompt, x_tail, ln_g, ln_b, w_in_t, w_gu, b_gate):
    row = lambda i: (i, 0)
    full = lambda i: (0, 0)
    return pl.pallas_call(
        _ln_gate_kernel,
        grid=(N_TILES,),
        in_specs=[
            pl.BlockSpec((None, ROW_TILE, D_MODEL), lambda i: (0, jnp.minimum(i, N_PROMPT_TILES - 1), 0)),
            pl.BlockSpec((ROW_TILE, D_MODEL), full),
            pl.BlockSpec((1, D_MODEL), full),
            pl.BlockSpec((1, D_MODEL), full),
            pl.BlockSpec((None, LANE, D_MODEL), lambda i: (0, GATE_COL0 // LANE, 0)),
            pl.BlockSpec((LANE, GLA_DK), full),
            pl.BlockSpec((1, GLA_DK), full),
        ],
        out_specs=[
            pl.BlockSpec((ROW_TILE, D_MODEL), row),
            pl.BlockSpec((ROW_TILE, D_MODEL), row),
            pl.BlockSpec((ROW_TILE, GLA_DK), row),
        ],
        out_shape=[
            jax.ShapeDtypeStruct((ROWS, D_MODEL), jnp.float32),
            jax.ShapeDtypeStruct((ROWS, D_MODEL), jnp.bfloat16),
            jax.ShapeDtypeStruct((ROWS, GLA_DK), jnp.float32),
        ],
        compiler_params=pltpu.CompilerParams(
            dimension_semantics=("arbitrary",), vmem_limit_bytes=VMEM_LIMIT),
        name="ln_gate",
    )(x_prompt, x_tail, ln_g, ln_b, w_in_t, w_gu, b_gate)


W_SLAB = 256
BIG_ROW_TILE = 1088
N_BIG_TILES = ROWS // BIG_ROW_TILE


PROMPT_CHUNK = 256
HEAD_ROWS = 2 * HEAD_K + 2 * HEAD_V


def _cast_head_weights(w_refs, wbf):
    off = 0
    for w_ref in w_refs:
        for r in range(0, w_ref.shape[0], W_SLAB):
            wbf[off + r:off + r + W_SLAB, :] = w_ref[r:r + W_SLAB, :].astype(jnp.bfloat16)
        off += w_ref.shape[0]


def _project_head(x, wbf):
    qk = lax.dot_general(x, wbf[0:2 * HEAD_K, :], _NT, preferred_element_type=jnp.float32)
    vr = lax.dot_general(x, wbf[2 * HEAD_K:, :], _NT, preferred_element_type=jnp.float32)
    return qk[:, :HEAD_K], qk[:, HEAD_K:], vr[:, :HEAD_V], vr[:, HEAD_V:]


def _head_weight_specs(head_of=lambda h, *_: h):
    kq = GLA_DK // HEAD_K
    return [
        pl.BlockSpec((None, HEAD_K, D_MODEL), lambda *ix: (0, head_of(*ix), 0)),
        pl.BlockSpec((None, HEAD_K, D_MODEL), lambda *ix: (0, kq + head_of(*ix), 0)),
        pl.BlockSpec((None, HEAD_V, D_MODEL), lambda *ix: (0, kq + head_of(*ix), 0)),
        pl.BlockSpec((None, HEAD_V, D_MODEL), lambda *ix: (0, 2 * kq + head_of(*ix), 0)),
    ]


def _row_of_block(b, block, r):
    n, w = b.shape
    b3 = b.reshape(n // block, block, w)
    return jnp.broadcast_to(b3[:, r:r + 1, :], b3.shape).reshape(n, w)


def _half_sizes(chunk):
    return tuple(chunk >> (j + 1) for j in range(chunk.bit_length() - 1))


def _gla_prepare(q, k, g, chunk):
    n, w = q.shape
    row = lax.broadcasted_iota(jnp.int32, (n, w), 0)
    pos = row & (chunk - 1)
    b = g
    for sh in reversed(_half_sizes(chunk)):
        b = b + jnp.where(pos >= sh, pltpu.roll(b, sh, 0), 0.0)
    qs = q * (HEAD_K ** -0.5)
    b_last = _row_of_block(b, chunk, chunk - 1)
    q_inter = (qs * jnp.exp(b)).astype(jnp.bfloat16)
    k_dec = (k * jnp.exp(b_last - b)).astype(jnp.bfloat16)
    levels = []
    for s in _half_sizes(chunk):
        upper = (row & (2 * s - 1)) >= s
        if s >= 4:
            b_mid = _row_of_block(b, 2 * s, s - 1)
            d = jnp.where(upper, b - b_mid, b_mid - b)
        elif s == 2:
            p4 = row & 3
            g_prev = pltpu.roll(g, 1, 0)
            g_next = pltpu.roll(g, n - 1, 0)
            d = jnp.where(p4 == 0, g_next, jnp.where(p4 == 1, 0.0, jnp.where(p4 == 2, g, g + g_prev)))
        else:
            d = jnp.where(upper, g, 0.0)
        x = (jnp.where(upper, qs, k) * jnp.exp(d)).astype(jnp.bfloat16)
        levels.append((x, x))
    levels.append((qs.astype(jnp.bfloat16), k.astype(jnp.bfloat16)))
    return q_inter, k_dec, b, levels


def _score_masks(chunk):
    ii = lax.broadcasted_iota(jnp.int32, (chunk, chunk), 0)
    jj = lax.broadcasted_iota(jnp.int32, (chunk, chunk), 1)
    masks = []
    for s in _half_sizes(chunk):
        blk = 2 * s
        same = (ii & ~(blk - 1)) == (jj & ~(blk - 1))
        masks.append(same & ((ii & (blk - 1)) >= s) & ((jj & (blk - 1)) < s))
    masks.append(ii == jj)
    return masks


def _gla_chunk(c, chunk, q_inter, k_dec, b, levels, v_bf, masks, s_val):
    r0 = c * chunk
    sl = slice(r0, r0 + chunk)
    scores = jnp.zeros((chunk, chunk), jnp.float32)
    for (qe, ke), m in zip(levels, masks):
        p = lax.dot_general(qe[sl], ke[sl], _NT, preferred_element_type=jnp.float32)
        scores = jnp.where(m, p, scores)
    vc = v_bf[sl]
    o = jnp.dot(scores.astype(jnp.bfloat16), vc, preferred_element_type=jnp.float32)
    o = o + jnp.dot(q_inter[sl], s_val.astype(jnp.bfloat16), preferred_element_type=jnp.float32)
    eb = jnp.exp(b[r0 + chunk - 1:r0 + chunk, :])
    eb_t = jnp.transpose(jnp.broadcast_to(eb, (LANE, HEAD_K)))
    decay = jnp.concatenate([eb_t] * (HEAD_V // LANE), axis=1)
    ds = lax.dot_general(k_dec[sl], vc, _TN, preferred_element_type=jnp.float32)
    return o, decay * s_val + ds


def _gla_finish(o, r, gn):
    o = o * lax.rsqrt(jnp.mean(o * o, axis=-1, keepdims=True) + RMS_EPS)
    return o * gn * (r * _sigmoid(r))


TAIL_HALF = DEC_BATCH // 2
TAIL_SPLIT_ROW = SAMPLE_ROW0 + TAIL_HALF * DEC_SEQ


def _gla_tail_kernel(x_ref, wq_ref, wk_ref, wv_ref, wr_ref, g_ref, gn_ref, sin_ref,
                     y_ref, sm_ref, ss_ref, wbf, u_scr):
    half = pl.program_id(1)
    masks = _score_masks(CHUNK)
    kv0, r0 = 2 * HEAD_K, 2 * HEAD_K + HEAD_V

    @pl.when(half == 0)
    def _project():
        _cast_head_weights((wq_ref, wk_ref, wv_ref, wr_ref), wbf)
        q, k, v, r = _project_head(x_ref[...], wbf)
        u_scr[...] = jnp.concatenate([q, k, v, r], axis=1)

    def run(first_stream, with_meta, row_lo, row_hi):
        def padded(c0, width):
            parts = [u_scr[0:CHUNK, c0:c0 + width]] if with_meta else []
            zeros = jnp.zeros((META_ROW0, width), jnp.float32)
            for s in range(first_stream, first_stream + TAIL_HALF):
                lo = SAMPLE_ROW0 + s * DEC_SEQ
                parts += [zeros, u_scr[lo:lo + DEC_SEQ, c0:c0 + width]]
            return jnp.concatenate(parts, axis=0)

        def padded_g():
            parts = [g_ref[0:CHUNK, :]] if with_meta else []
            zeros = jnp.zeros((META_ROW0, HEAD_K), jnp.float32)
            for s in range(first_stream, first_stream + TAIL_HALF):
                lo = SAMPLE_ROW0 + s * DEC_SEQ
                parts += [zeros, g_ref[lo:lo + DEC_SEQ, :]]
            return jnp.concatenate(parts, axis=0)

        q = padded(0, HEAD_K)
        k = padded(HEAD_K, HEAD_K)
        v = padded(kv0, HEAD_V)
        g = padded_g()
        live = (lax.broadcasted_iota(jnp.int32, q.shape, 0) & (CHUNK - 1)) >= META_ROW0
        k = jnp.where(live, k, 0.0)
        g = jnp.where(live, g, 0.0)
        q_inter, k_dec, b, levels = _gla_prepare(q, k, g, CHUNK)
        v_bf = v.astype(jnp.bfloat16)
        outs, c = [], 0
        if with_meta:
            outs.append(jnp.zeros((META_ROW0, HEAD_V), jnp.float32))
            o, s_meta = _gla_chunk(0, CHUNK, q_inter, k_dec, b, levels, v_bf, masks,
                                   jnp.zeros((HEAD_K, HEAD_V), jnp.float32))
            sm_ref[...] = s_meta
            outs.append(o[META_ROW0:])
            c = 1
        for s in range(TAIL_HALF):
            o, s_new = _gla_chunk(c + s, CHUNK, q_inter, k_dec, b, levels, v_bf, masks, sin_ref[s])
            ss_ref[s] = s_new
            outs.append(o[META_ROW0:])
        if not with_meta:
            outs.append(jnp.zeros((ROW_TILE - SAMPLE_ROW0 - SAMPLE_ROWS, HEAD_V), jnp.float32))
        y_ref[row_lo:row_hi, :] = _gla_finish(
            jnp.concatenate(outs, axis=0), u_scr[row_lo:row_hi, r0:r0 + HEAD_V], gn_ref[...])

    @pl.when(half == 0)
    def _first():
        run(0, True, 0, TAIL_SPLIT_ROW)

    @pl.when(half == 1)
    def _second():
        run(TAIL_HALF, False, TAIL_SPLIT_ROW, ROW_TILE)


def _gla_tail(xb, w_in_t, g, gla_norm, state_gla):
    states = pl.BlockSpec((TAIL_HALF, None, HEAD_K, HEAD_V), lambda h, half: (half, h, 0, 0))
    next_head_early = lambda h, half: jnp.minimum(h + half, GLA_HEADS - 1)
    return pl.pallas_call(
        _gla_tail_kernel,
        grid=(GLA_HEADS, 2),
        in_specs=[
            pl.BlockSpec((ROW_TILE, D_MODEL), lambda h, half: (N_PROMPT_TILES, 0)),
            *_head_weight_specs(next_head_early),
            pl.BlockSpec((ROW_TILE, HEAD_K), lambda h, half: (N_PROMPT_TILES, h)),
            pl.BlockSpec((1, HEAD_V), lambda h, half: (0, h)),
            states,
        ],
        out_specs=[
            pl.BlockSpec((ROW_TILE, HEAD_V), lambda h, half: (0, h)),
            pl.BlockSpec((None, HEAD_K, HEAD_V), lambda h, half: (h, 0, 0)),
            states,
        ],
        out_shape=[
            jax.ShapeDtypeStruct((ROW_TILE, D_MODEL), jnp.float32),
            jax.ShapeDtypeStruct((GLA_HEADS, HEAD_K, HEAD_V), jnp.float32),
            jax.ShapeDtypeStruct((DEC_BATCH, GLA_HEADS, HEAD_K, HEAD_V), jnp.float32),
        ],
        scratch_shapes=[pltpu.VMEM((HEAD_ROWS, D_MODEL), jnp.bfloat16),
                        pltpu.VMEM((ROW_TILE, HEAD_ROWS), jnp.float32)],
        compiler_params=pltpu.CompilerParams(
            dimension_semantics=("arbitrary", "arbitrary"), vmem_limit_bytes=VMEM_LIMIT),
        name="gla_tail",
    )(xb, *([w_in_t] * 4), g, gla_norm, state_gla)


GLA_STEP_ROWS = ROW_TILE
N_GLA_ROW_STEPS = SEQ // GLA_STEP_ROWS
N_GLA_STEPS = GLA_HEADS * N_GLA_ROW_STEPS
MIX_W_ROWS = 5 * D_MODEL
MIX_W_SLAB = MIX_W_ROWS // N_GLA_STEPS
OUT_W_SLAB = D_MODEL // N_GLA_STEPS


def _gla_prompt_kernel(x_ref, wq_ref, wk_ref, wv_ref, wr_ref, g_ref, gn_ref, s0_ref, wmix_ref, wout_ref,
                       y_ref, sp_ref, wmix_bf_ref, wout_bf_ref, wbf, s_scr):
    i = pl.program_id(1)

    @pl.when(i == 0)
    def _start_head():
        _cast_head_weights((wq_ref, wk_ref, wv_ref, wr_ref), wbf)
        s_scr[...] = s0_ref[...]

    masks = _score_masks(PROMPT_CHUNK)
    for r0 in range(0, GLA_STEP_ROWS, ROW_TILE):
        rows = slice(r0, r0 + ROW_TILE)
        q, k, v, r = _project_head(x_ref[rows, :], wbf)
        if r0 == 0:
            wmix_bf_ref[...] = wmix_ref[0].astype(jnp.bfloat16)
            wout_bf_ref[...] = wout_ref[...].astype(jnp.bfloat16)
        q_inter, k_dec, b, levels = _gla_prepare(q, k, g_ref[rows, :], PROMPT_CHUNK)
        v_bf = v.astype(jnp.bfloat16)
        outs = []
        for c in range(ROW_TILE // PROMPT_CHUNK):
            o, s_new = _gla_chunk(c, PROMPT_CHUNK, q_inter, k_dec, b, levels, v_bf, masks, s_scr[...])
            s_scr[...] = s_new
            outs.append(o)
        y_ref[rows, :] = _gla_finish(jnp.concatenate(outs, axis=0), r, gn_ref[...])

    @pl.when(i == N_GLA_ROW_STEPS - 1)
    def _final_state():
        sp_ref[...] = s_scr[...]


def _gla_prompt(xb, w_in_t, g, gla_norm, s_meta, w_out):
    step = lambda h, i: h * N_GLA_ROW_STEPS + i
    return pl.pallas_call(
        _gla_prompt_kernel,
        grid=(GLA_HEADS, N_GLA_ROW_STEPS),
        in_specs=[
            pl.BlockSpec((GLA_STEP_ROWS, D_MODEL), lambda h, i: (i, 0)),
            *_head_weight_specs(),
            pl.BlockSpec((GLA_STEP_ROWS, HEAD_K), lambda h, i: (i, h)),
            pl.BlockSpec((1, HEAD_V), lambda h, i: (0, h)),
            pl.BlockSpec((None, HEAD_K, HEAD_V), lambda h, i: (h, 0, 0)),
            pl.BlockSpec((pl.Element(1), pl.Element(MIX_W_SLAB), pl.Element(D_MODEL)),
                         lambda h, i: (0, pl.multiple_of(GATE_COL0 + GATE_RANK + step(h, i) * MIX_W_SLAB,
                                                         GATE_RANK), 0)),
            pl.BlockSpec((None, OUT_W_SLAB, D_MODEL), lambda h, i: (0, step(h, i), 0)),
        ],
        out_specs=[
            pl.BlockSpec((GLA_STEP_ROWS, HEAD_V), lambda h, i: (i, h)),
            pl.BlockSpec((None, HEAD_K, HEAD_V), lambda h, i: (h, 0, 0)),
            pl.BlockSpec((MIX_W_SLAB, D_MODEL), lambda h, i: (step(h, i), 0)),
            pl.BlockSpec((OUT_W_SLAB, D_MODEL), lambda h, i: (step(h, i), 0)),
        ],
        out_shape=[
            jax.ShapeDtypeStruct((SEQ, D_MODEL), jnp.float32),
            jax.ShapeDtypeStruct((GLA_HEADS, HEAD_K, HEAD_V), jnp.float32),
            jax.ShapeDtypeStruct((MIX_W_ROWS, D_MODEL), jnp.bfloat16),
            jax.ShapeDtypeStruct((D_MODEL, D_MODEL), jnp.bfloat16),
        ],
        scratch_shapes=[pltpu.VMEM((HEAD_ROWS, D_MODEL), jnp.bfloat16),
                        pltpu.VMEM((HEAD_K, HEAD_V), jnp.float32)],
        compiler_params=pltpu.CompilerParams(
            dimension_semantics=("arbitrary", "arbitrary"), vmem_limit_bytes=VMEM_LIMIT),
        name="gla_prompt",
    )(xb, *([w_in_t] * 4), g, gla_norm, s_meta, w_in_t, w_out)


MIX_TN = 512
N_MIX_GROUPS = 5


MIX_STEP_ROWS = 2 * ROW_TILE
N_MIX_ROW_STEPS = SEQ // MIX_STEP_ROWS


def _mix_products(x, w_refs):
    return [lax.dot_general(x, w_ref[...], _NT, preferred_element_type=jnp.float32) for w_ref in w_refs]


def _mix_finish(c, p1, p2, cb, ga, gb, ya, cw_ref):
    conv = cw_ref[0:1, :] * p2 + cw_ref[1:2, :] * p1 + cw_ref[2:3, :] * c
    return (_sigmoid(ga) * ya + _sigmoid(gb) * (cb * conv)).astype(jnp.bfloat16)


def _mix_tail_kernel(x_ref, ya_ref, wcb_ref, wcc_ref, wcx_ref, wga_ref, wgb_ref, cw_ref, cache_ref,
                     m_ref, ctail_ref):
    cb, cc, cx, ga, gb = _mix_products(x_ref[...], (wcb_ref, wcc_ref, wcx_ref, wga_ref, wgb_ref))
    c = cc * cx
    row = lax.broadcasted_iota(jnp.int32, (ROW_TILE, MIX_TN), 0)
    prev1 = pltpu.roll(c, 1, 0)
    prev2 = pltpu.roll(c, 2, 0)

    def stream_rows(j):
        parts = [jnp.zeros((SAMPLE_ROW0, MIX_TN), jnp.float32)]
        for s in range(DEC_BATCH):
            parts.append(jnp.broadcast_to(cache_ref[2 * s + j:2 * s + j + 1, :], (DEC_SEQ, MIX_TN)))
        parts.append(jnp.zeros((ROW_TILE - SAMPLE_ROW0 - SAMPLE_ROWS, MIX_TN), jnp.float32))
        return jnp.concatenate(parts, axis=0)

    old = stream_rows(0)
    new = stream_rows(1)
    in_sample = (row >= SAMPLE_ROW0) & (row < SAMPLE_ROW0 + SAMPLE_ROWS)
    p16 = (row - SAMPLE_ROW0) & (DEC_SEQ - 1)
    first = in_sample & (p16 == 0)
    second = in_sample & (p16 == 1)
    p1 = jnp.where(row == META_ROW0, 0.0, jnp.where(first, new, prev1))
    p2 = jnp.where((row == META_ROW0) | (row == META_ROW0 + 1), 0.0,
                   jnp.where(first, old, jnp.where(second, new, prev2)))
    m_ref[...] = _mix_finish(c, p1, p2, cb, ga, gb, ya_ref[...], cw_ref)
    ctail_ref[...] = c


def _mix_prompt_kernel(x_ref, ya_ref, wcb_ref, wcc_ref, wcx_ref, wga_ref, wgb_ref, cw_ref, c0_ref,
                       m_ref, clast_ref, carry):
    i = pl.program_id(1)

    @pl.when(i == 0)
    def _start():
        carry[...] = c0_ref[...]

    hist = carry[...]
    row = lax.broadcasted_iota(jnp.int32, (ROW_TILE, MIX_TN), 0)
    for r0 in range(0, MIX_STEP_ROWS, ROW_TILE):
        rows = slice(r0, r0 + ROW_TILE)
        cb, cc, cx, ga, gb = _mix_products(x_ref[rows, :], (wcb_ref, wcc_ref, wcx_ref, wga_ref, wgb_ref))
        c = cc * cx
        last = jnp.broadcast_to(hist[7:8], c.shape)
        p1 = jnp.where(row == 0, last, pltpu.roll(c, 1, 0))
        p2 = jnp.where(row == 0, jnp.broadcast_to(hist[6:7], c.shape),
                       jnp.where(row == 1, last, pltpu.roll(c, 2, 0)))
        m_ref[rows, :] = _mix_finish(c, p1, p2, cb, ga, gb, ya_ref[rows, :], cw_ref)
        hist = c[ROW_TILE - 8:]
    carry[...] = hist

    @pl.when(i == N_MIX_ROW_STEPS - 1)
    def _last():
        clast_ref[...] = hist


def _mix_weight_specs(index_args):
    nc = D_MODEL // MIX_TN
    return [pl.BlockSpec((MIX_TN, D_MODEL), functools.partial(index_args, grp * nc))
            for grp in range(N_MIX_GROUPS)]


def _mix_tail(xb, ya_tail, w_mix_bf, conv_w, cache_rows):
    top = lambda cj: (0, cj)
    return pl.pallas_call(
        _mix_tail_kernel,
        grid=(D_MODEL // MIX_TN,),
        in_specs=[
            pl.BlockSpec((ROW_TILE, D_MODEL), lambda cj: (N_PROMPT_TILES, 0)),
            pl.BlockSpec((ROW_TILE, MIX_TN), top),
            *_mix_weight_specs(lambda base, cj: (base + cj, 0)),
            pl.BlockSpec((3, MIX_TN), top),
            pl.BlockSpec((2 * DEC_BATCH, MIX_TN), top),
        ],
        out_specs=[pl.BlockSpec((ROW_TILE, MIX_TN), top), pl.BlockSpec((ROW_TILE, MIX_TN), top)],
        out_shape=[
            jax.ShapeDtypeStruct((ROW_TILE, D_MODEL), jnp.bfloat16),
            jax.ShapeDtypeStruct((ROW_TILE, D_MODEL), jnp.float32),
        ],
        compiler_params=pltpu.CompilerParams(
            dimension_semantics=("arbitrary",), vmem_limit_bytes=VMEM_LIMIT),
        name="proj_conv_mix_tail",
    )(xb, ya_tail, *([w_mix_bf] * N_MIX_GROUPS), conv_w, cache_rows)


def _mix_prompt(xb, ya_prompt, w_mix_bf, conv_w, c_tail):
    top = lambda cj, i: (0, cj)
    return pl.pallas_call(
        _mix_prompt_kernel,
        grid=(D_MODEL // MIX_TN, N_MIX_ROW_STEPS),
        in_specs=[
            pl.BlockSpec((MIX_STEP_ROWS, D_MODEL), lambda cj, i: (i, 0)),
            pl.BlockSpec((MIX_STEP_ROWS, MIX_TN), lambda cj, i: (i, cj)),
            *_mix_weight_specs(lambda base, cj, i: (base + cj, 0)),
            pl.BlockSpec((3, MIX_TN), top),
            pl.BlockSpec((8, MIX_TN), lambda cj, i: (CHUNK // 8 - 1, cj)),
        ],
        out_specs=[pl.BlockSpec((MIX_STEP_ROWS, MIX_TN), lambda cj, i: (i, cj)), pl.BlockSpec((8, MIX_TN), top)],
        out_shape=[
            jax.ShapeDtypeStruct((SEQ, D_MODEL), jnp.bfloat16),
            jax.ShapeDtypeStruct((8, D_MODEL), jnp.float32),
        ],
        scratch_shapes=[pltpu.VMEM((8, MIX_TN), jnp.float32)],
        compiler_params=pltpu.CompilerParams(
            dimension_semantics=("arbitrary", "arbitrary"), vmem_limit_bytes=VMEM_LIMIT),
        name="proj_conv_mix",
    )(xb, ya_prompt, *([w_mix_bf] * N_MIX_GROUPS), conv_w, c_tail)


SUB_ROWS = 128


def _out_proj_kernel(mp_ref, mt_ref, xn_ref, w_ref, lg_ref, lb_ref, h_ref, hb_ref):
    is_prompt = pl.program_id(0) < N_PROMPT_TILES
    for r in range(0, ROW_TILE, SUB_ROWS):
        rows = slice(r, r + SUB_ROWS)
        m = jnp.where(is_prompt, mp_ref[rows, :], mt_ref[rows, :])
        mix = jnp.dot(m, w_ref[...], preferred_element_type=jnp.float32)
        h = _layer_norm(ALPHA * xn_ref[rows, :] + mix, lg_ref[...], lb_ref[...])
        h_ref[rows, :] = h
        hb_ref[rows, :] = h.astype(jnp.bfloat16)


def _out_proj(m_prompt, m_tail, xn, w_out, ln_g, ln_b):
    row = lambda i: (i, 0)
    full = lambda i: (0, 0)
    return pl.pallas_call(
        _out_proj_kernel,
        grid=(N_TILES,),
        in_specs=[
            pl.BlockSpec((ROW_TILE, D_MODEL), lambda i: (jnp.minimum(i, N_PROMPT_TILES - 1), 0)),
            pl.BlockSpec((ROW_TILE, D_MODEL), full),
            pl.BlockSpec((ROW_TILE, D_MODEL), row),
            pl.BlockSpec((D_MODEL, D_MODEL), full),
            pl.BlockSpec((1, D_MODEL), full),
            pl.BlockSpec((1, D_MODEL), full),
        ],
        out_specs=[pl.BlockSpec((ROW_TILE, D_MODEL), row), pl.BlockSpec((ROW_TILE, D_MODEL), row)],
        out_shape=[
            jax.ShapeDtypeStruct((ROWS, D_MODEL), jnp.float32),
            jax.ShapeDtypeStruct((ROWS, D_MODEL), jnp.bfloat16),
        ],
        compiler_params=pltpu.CompilerParams(
            dimension_semantics=("arbitrary",), vmem_limit_bytes=VMEM_LIMIT),
        name="out_proj_ln",
    )(m_prompt, m_tail, xn, w_out, ln_g, ln_b)


FF_TILE = 512
N_FF_TILES = D_FF // FF_TILE
FFN_UP_ROWS = 2 * BIG_ROW_TILE


N_FFN_UP_STEPS = N_FF_TILES * (ROWS // FFN_UP_ROWS)
DOWN_W_SLAB = D_FF // N_FFN_UP_STEPS


def _ffn_up_kernel(hb_ref, wg_ref, wu_ref, wd_ref, act_ref, wd_bf_ref, wg_bf, wu_bf):
    @pl.when(pl.program_id(1) == 0)
    def _cast():
        for r in range(0, D_MODEL, W_SLAB):
            wg_bf[r:r + W_SLAB, :] = wg_ref[r:r + W_SLAB, :].astype(jnp.bfloat16)
            wu_bf[r:r + W_SLAB, :] = wu_ref[r:r + W_SLAB, :].astype(jnp.bfloat16)

    for r in range(0, FFN_UP_ROWS, BIG_ROW_TILE):
        rows = slice(r, r + BIG_ROW_TILE)
        hb = hb_ref[rows, :]
        gate = jnp.dot(hb, wg_bf[...], preferred_element_type=jnp.float32)
        up = jnp.dot(hb, wu_bf[...], preferred_element_type=jnp.float32)
        if r == 0:
            wd_bf_ref[...] = wd_ref[...].astype(jnp.bfloat16)
        act_ref[rows, :] = (gate * _sigmoid(gate) * up).astype(jnp.bfloat16)


def _ffn_up(hb, w_ffn_in, w_ffn_out):
    step = lambda j, i: j * (ROWS // FFN_UP_ROWS) + i
    return pl.pallas_call(
        _ffn_up_kernel,
        grid=(N_FF_TILES, ROWS // FFN_UP_ROWS),
        in_specs=[
            pl.BlockSpec((FFN_UP_ROWS, D_MODEL), lambda j, i: (i, 0)),
            pl.BlockSpec((None, D_MODEL, FF_TILE), lambda j, i: (0, 0, j)),
            pl.BlockSpec((None, D_MODEL, FF_TILE), lambda j, i: (0, 0, N_FF_TILES + j)),
            pl.BlockSpec((None, DOWN_W_SLAB, D_MODEL), lambda j, i: (0, step(j, i), 0)),
        ],
        out_specs=[
            pl.BlockSpec((FFN_UP_ROWS, FF_TILE), lambda j, i: (i, j)),
            pl.BlockSpec((DOWN_W_SLAB, D_MODEL), lambda j, i: (step(j, i), 0)),
        ],
        out_shape=[
            jax.ShapeDtypeStruct((ROWS, D_FF), jnp.bfloat16),
            jax.ShapeDtypeStruct((D_FF, D_MODEL), jnp.bfloat16),
        ],
        scratch_shapes=[pltpu.VMEM((D_MODEL, FF_TILE), jnp.bfloat16),
                        pltpu.VMEM((D_MODEL, FF_TILE), jnp.bfloat16)],
        compiler_params=pltpu.CompilerParams(
            dimension_semantics=("arbitrary", "arbitrary"), vmem_limit_bytes=VMEM_LIMIT),
        name="ffn_up",
    )(hb, w_ffn_in, w_ffn_in, w_ffn_out)


def _ffn_down_kernel(act_ref, h_ref, wd_ref, lg_ref, lb_ref, y_ref):
    for r in range(0, ROW_TILE, SUB_ROWS):
        rows = slice(r, r + SUB_ROWS)
        f = jnp.dot(act_ref[rows, :], wd_ref[...], preferred_element_type=jnp.float32)
        y_ref[rows, :] = _layer_norm(ALPHA * h_ref[rows, :] + f, lg_ref[...], lb_ref[...])


def _ffn_down(act, h, w_down, ln_g, ln_b, tail):
    first = N_PROMPT_TILES if tail else 0
    row = lambda i: (first + i, 0)
    full = lambda i: (0, 0)
    if tail:
        out_spec = pl.BlockSpec((ROW_TILE, D_MODEL), full)
        out_shape = jax.ShapeDtypeStruct((ROW_TILE, D_MODEL), jnp.float32)
    else:
        out_spec = pl.BlockSpec((None, ROW_TILE, D_MODEL), lambda i: (0, i, 0))
        out_shape = jax.ShapeDtypeStruct((1, SEQ, D_MODEL), jnp.float32)
    return pl.pallas_call(
        _ffn_down_kernel,
        grid=(1 if tail else N_PROMPT_TILES,),
        in_specs=[
            pl.BlockSpec((ROW_TILE, D_FF), row),
            pl.BlockSpec((ROW_TILE, D_MODEL), row),
            pl.BlockSpec((D_FF, D_MODEL), full, pipeline_mode=pl.Buffered(1)),
            pl.BlockSpec((1, D_MODEL), full),
            pl.BlockSpec((1, D_MODEL), full),
        ],
        out_specs=out_spec,
        out_shape=out_shape,
        compiler_params=pltpu.CompilerParams(
            dimension_semantics=("arbitrary",), vmem_limit_bytes=VMEM_LIMIT),
        name="ffn_down_ln_tail" if tail else "ffn_down_ln",
    )(act, h, w_down, ln_g, ln_b)


def kernel(x_prompt, x_sample, state_gla, cache_conv, meta_tokens, ln_in_g, ln_in_b, w_in, w_gate_up, b_gate, gla_norm, conv_w, w_out, ln1_g, ln1_b, w_ffn_in, w_ffn_out, ln2_g, ln2_b):
    f32, bf16 = jnp.float32, jnp.bfloat16
    tail = jnp.concatenate([
        jnp.zeros((META_ROW0, D_MODEL), f32),
        meta_tokens.astype(f32),
        x_sample.reshape(SAMPLE_ROWS, D_MODEL),
        jnp.zeros((ROW_TILE - SAMPLE_ROW0 - SAMPLE_ROWS, D_MODEL), f32),
    ], axis=0)
    w_gu = jnp.pad(w_gate_up[0], ((0, LANE - GATE_RANK), (0, 0))).astype(bf16)
    vec = lambda p: p.reshape(1, -1).astype(f32)

    w_in_t = jnp.swapaxes(w_in, 1, 2)
    xn, xb, g = _ln_gate(x_prompt, tail, vec(ln_in_g), vec(ln_in_b), w_in_t, w_gu, vec(b_gate[0]))
    ya_tail, s_meta, s_sample = _gla_tail(xb, w_in_t, g, vec(gla_norm[0]), state_gla[0])
    ya_prompt, s_prompt, w_mix_bf, w_out_bf = _gla_prompt(xb, w_in_t, g, vec(gla_norm[0]), s_meta, w_out)
    m_tail, c_tail = _mix_tail(xb, ya_tail, w_mix_bf, conv_w[0], cache_conv[0].reshape(2 * DEC_BATCH, D_MODEL))
    m_prompt, c_last = _mix_prompt(xb, ya_prompt, w_mix_bf, conv_w[0], c_tail)
    h, hb = _out_proj(m_prompt, m_tail, xn, w_out_bf, vec(ln1_g[0]), vec(ln1_b[0]))
    act, w_down_bf = _ffn_up(hb, w_ffn_in, w_ffn_out)
    y_prompt = _ffn_down(act, h, w_down_bf, vec(ln2_g[0]), vec(ln2_b[0]), tail=False)
    y_tail = _ffn_down(act, h, w_down_bf, vec(ln2_g[0]), vec(ln2_b[0]), tail=True)

    y_sample = y_tail[SAMPLE_ROW0:SAMPLE_ROW0 + SAMPLE_ROWS].reshape(DEC_BATCH, DEC_SEQ, D_MODEL)
    cache_prompt = c_last[6:8][None, None]
    cache_sample = c_tail[SAMPLE_ROW0:SAMPLE_ROW0 + SAMPLE_ROWS].reshape(
        DEC_BATCH, DEC_SEQ, D_MODEL)[:, DEC_SEQ - 2:][None]
    return (y_prompt, y_sample, s_prompt[None, None], cache_prompt, s_sample[None], cache_sample)
```

```python
import functools

import jax
import jax.numpy as jnp
from jax import lax
from jax.experimental import pallas as pl
from jax.experimental.pallas import tpu as pltpu

D_MODEL = 2048
SEQ = 8192
DEC_BATCH = 16
DEC_SEQ = 16
CHUNK = 64
N_META = 16
GLA_HEADS = 4
HEAD_K = 256
HEAD_V = 512
GLA_DK = GLA_HEADS * HEAD_K
GATE_RANK = 16
GATE_TEMP = 16.0
D_FF = 5632
ALPHA = 2.0 ** 0.25
LN_EPS = 1e-5
RMS_EPS = 1e-6

ROW_TILE = 512
N_PROMPT_TILES = SEQ // ROW_TILE
N_TILES = N_PROMPT_TILES + 1
ROWS = N_TILES * ROW_TILE
META_ROW0 = CHUNK - N_META
SAMPLE_ROW0 = CHUNK
SAMPLE_ROWS = DEC_BATCH * DEC_SEQ
LANE = 128
GATE_COL0 = 2 * GLA_DK + 2 * D_MODEL
VMEM_LIMIT = 56 * 1024 * 1024

_NT = (((1,), (1,)), ((), ()))
_TN = (((0,), (0,)), ((), ()))


def _tile_first_tail(i):
    return (i + N_PROMPT_TILES) % N_TILES


def _layer_norm(x, g, b):
    mu = jnp.mean(x, axis=-1, keepdims=True)
    xc = x - mu
    var = jnp.mean(xc * xc, axis=-1, keepdims=True)
    return xc * lax.rsqrt(var + LN_EPS) * g + b


def _log_sigmoid(x):
    return jnp.minimum(x, 0.0) - jnp.log1p(jnp.exp(-jnp.abs(x)))


def _sigmoid(x):
    return 1.0 / (1.0 + jnp.exp(-x))


def _ln_gate_kernel(xp_ref, xt_ref, lg_ref, lb_ref, wa_ref, wgu_ref, bg_ref, xn_ref, xb_ref, g_ref):
    x = jnp.where(pl.program_id(0) < N_PROMPT_TILES, xp_ref[...], xt_ref[...])
    xn = _layer_norm(x, lg_ref[...], lb_ref[...])
    xn_ref[...] = xn
    xb = xn.astype(jnp.bfloat16)
    xb_ref[...] = xb
    a = lax.dot_general(xb, wa_ref[...].astype(jnp.bfloat16), _NT, preferred_element_type=jnp.float32)
    z = jnp.dot(a.astype(jnp.bfloat16), wgu_ref[...], preferred_element_type=jnp.float32)
    g_ref[...] = _log_sigmoid(z + bg_ref[...]) * (1.0 / GATE_TEMP)


def _ln_gate(x_prompt, x_tail, ln_g, ln_b, w_in_t, w_gu, b_gate):
    row = lambda i: (i, 0)
    full = lambda i: (0, 0)
    return pl.pallas_call(
        _ln_gate_kernel,
        grid=(N_TILES,),
        in_specs=[
            pl.BlockSpec((None, ROW_TILE, D_MODEL), lambda i: (0, jnp.minimum(i, N_PROMPT_TILES - 1), 0)),
            pl.BlockSpec((ROW_TILE, D_MODEL), full),
            pl.BlockSpec((1, D_MODEL), full),
            pl.BlockSpec((1, D_MODEL), full),
            pl.BlockSpec((None, LANE, D_MODEL), lambda i: (0, GATE_COL0 // LANE, 0)),
            pl.BlockSpec((LANE, GLA_DK), full),
            pl.BlockSpec((1, GLA_DK), full),
        ],
        out_specs=[
            pl.BlockSpec((ROW_TILE, D_MODEL), row),
            pl.BlockSpec((ROW_TILE, D_MODEL), row),
            pl.BlockSpec((ROW_TILE, GLA_DK), row),
        ],
        out_shape=[
            jax.ShapeDtypeStruct((ROWS, D_MODEL), jnp.float32),
            jax.ShapeDtypeStruct((ROWS, D_MODEL), jnp.bfloat16),
            jax.ShapeDtypeStruct((ROWS, GLA_DK), jnp.float32),
        ],
        compiler_params=pltpu.CompilerParams(
            dimension_semantics=("arbitrary",), vmem_limit_bytes=VMEM_LIMIT,
            allow_input_fusion=[False, True, False, False, False, False, False]),
        name="ln_gate",
    )(x_prompt, x_tail, ln_g, ln_b, w_in_t, w_gu, b_gate)


W_SLAB = 256
BIG_ROW_TILE = 1088
N_BIG_TILES = ROWS // BIG_ROW_TILE


PROMPT_CHUNK = 256
HEAD_ROWS = 2 * HEAD_K + 2 * HEAD_V


def _cast_head_weights(w_refs, wbf):
    off = 0
    for w_ref in w_refs:
        for r in range(0, w_ref.shape[0], W_SLAB):
            wbf[off + r:off + r + W_SLAB, :] = w_ref[r:r + W_SLAB, :].astype(jnp.bfloat16)
        off += w_ref.shape[0]


def _project_head(x, wbf):
    qk = lax.dot_general(x, wbf[0:2 * HEAD_K, :], _NT, preferred_element_type=jnp.float32)
    vr = lax.dot_general(x, wbf[2 * HEAD_K:, :], _NT, preferred_element_type=jnp.float32)
    return qk[:, :HEAD_K], qk[:, HEAD_K:], vr[:, :HEAD_V], vr[:, HEAD_V:]


def _head_weight_specs(head_of=lambda h, *_: h):
    kq = GLA_DK // HEAD_K
    return [
        pl.BlockSpec((None, HEAD_K, D_MODEL), lambda *ix: (0, head_of(*ix), 0)),
        pl.BlockSpec((None, HEAD_K, D_MODEL), lambda *ix: (0, kq + head_of(*ix), 0)),
        pl.BlockSpec((None, HEAD_V, D_MODEL), lambda *ix: (0, kq + head_of(*ix), 0)),
        pl.BlockSpec((None, HEAD_V, D_MODEL), lambda *ix: (0, 2 * kq + head_of(*ix), 0)),
    ]


def _row_of_block(b, block, r):
    n, w = b.shape
    b3 = b.reshape(n // block, block, w)
    return jnp.broadcast_to(b3[:, r:r + 1, :], b3.shape).reshape(n, w)


def _half_sizes(chunk):
    return tuple(chunk >> (j + 1) for j in range(chunk.bit_length() - 1))


def _gla_prepare(q, k, g, chunk):
    n, w = q.shape
    row = lax.broadcasted_iota(jnp.int32, (n, w), 0)
    pos = row & (chunk - 1)
    b = g
    for sh in reversed(_half_sizes(chunk)):
        b = b + jnp.where(pos >= sh, pltpu.roll(b, sh, 0), 0.0)
    qs = q * (HEAD_K ** -0.5)
    b_last = _row_of_block(b, chunk, chunk - 1)
    q_inter = (qs * jnp.exp(b)).astype(jnp.bfloat16)
    k_dec = (k * jnp.exp(b_last - b)).astype(jnp.bfloat16)
    levels = []
    for s in _half_sizes(chunk):
        upper = (row & (2 * s - 1)) >= s
        if s >= 4:
            b_mid = _row_of_block(b, 2 * s, s - 1)
            d = jnp.where(upper, b - b_mid, b_mid - b)
        elif s == 2:
            p4 = row & 3
            g_prev = pltpu.roll(g, 1, 0)
            g_next = pltpu.roll(g, n - 1, 0)
            d = jnp.where(p4 == 0, g_next, jnp.where(p4 == 1, 0.0, jnp.where(p4 == 2, g, g + g_prev)))
        else:
            d = jnp.where(upper, g, 0.0)
        x = (jnp.where(upper, qs, k) * jnp.exp(d)).astype(jnp.bfloat16)
        levels.append((x, x))
    levels.append((qs.astype(jnp.bfloat16), k.astype(jnp.bfloat16)))
    return q_inter, k_dec, b, levels


def _score_masks(chunk):
    ii = lax.broadcasted_iota(jnp.int32, (chunk, chunk), 0)
    jj = lax.broadcasted_iota(jnp.int32, (chunk, chunk), 1)
    masks = []
    for s in _half_sizes(chunk):
        blk = 2 * s
        same = (ii & ~(blk - 1)) == (jj & ~(blk - 1))
        masks.append(same & ((ii & (blk - 1)) >= s) & ((jj & (blk - 1)) < s))
    masks.append(ii == jj)
    return masks


def _gla_chunk(c, chunk, q_inter, k_dec, b, levels, v_bf, masks, s_val):
    r0 = c * chunk
    sl = slice(r0, r0 + chunk)
    scores = jnp.zeros((chunk, chunk), jnp.float32)
    for (qe, ke), m in zip(levels, masks):
        p = lax.dot_general(qe[sl], ke[sl], _NT, preferred_element_type=jnp.float32)
        scores = jnp.where(m, p, scores)
    vc = v_bf[sl]
    o = jnp.dot(scores.astype(jnp.bfloat16), vc, preferred_element_type=jnp.float32)
    o = o + jnp.dot(q_inter[sl], s_val.astype(jnp.bfloat16), preferred_element_type=jnp.float32)
    eb = jnp.exp(b[r0 + chunk - 1:r0 + chunk, :])
    eb_t = jnp.transpose(jnp.broadcast_to(eb, (LANE, HEAD_K)))
    decay = jnp.concatenate([eb_t] * (HEAD_V // LANE), axis=1)
    ds = lax.dot_general(k_dec[sl], vc, _TN, preferred_element_type=jnp.float32)
    return o, decay * s_val + ds


def _gla_finish(o, r, gn):
    o = o * lax.rsqrt(jnp.mean(o * o, axis=-1, keepdims=True) + RMS_EPS)
    return o * gn * (r * _sigmoid(r))


TAIL_HALF = DEC_BATCH // 2
TAIL_SPLIT_ROW = SAMPLE_ROW0 + TAIL_HALF * DEC_SEQ


def _gla_tail_kernel(x_ref, wq_ref, wk_ref, wv_ref, wr_ref, g_ref, gn_ref, sin_ref,
                     y_ref, sm_ref, ss_ref, wbf, u_scr):
    half = pl.program_id(1)
    masks = _score_masks(CHUNK)
    kv0, r0 = 2 * HEAD_K, 2 * HEAD_K + HEAD_V

    @pl.when(half == 0)
    def _project():
        _cast_head_weights((wq_ref, wk_ref, wv_ref, wr_ref), wbf)
        q, k, v, r = _project_head(x_ref[...], wbf)
        u_scr[...] = jnp.concatenate([q, k, v, r], axis=1)

    def run(first_stream, with_meta, row_lo, row_hi):
        def padded(c0, width):
            parts = [u_scr[0:CHUNK, c0:c0 + width]] if with_meta else []
            zeros = jnp.zeros((META_ROW0, width), jnp.float32)
            for s in range(first_stream, first_stream + TAIL_HALF):
                lo = SAMPLE_ROW0 + s * DEC_SEQ
                parts += [zeros, u_scr[lo:lo + DEC_SEQ, c0:c0 + width]]
            return jnp.concatenate(parts, axis=0)

        def padded_g():
            parts = [g_ref[0:CHUNK, :]] if with_meta else []
            zeros = jnp.zeros((META_ROW0, HEAD_K), jnp.float32)
            for s in range(first_stream, first_stream + TAIL_HALF):
                lo = SAMPLE_ROW0 + s * DEC_SEQ
                parts += [zeros, g_ref[lo:lo + DEC_SEQ, :]]
            return jnp.concatenate(parts, axis=0)

        q = padded(0, HEAD_K)
        k = padded(HEAD_K, HEAD_K)
        v = padded(kv0, HEAD_V)
        g = padded_g()
        live = (lax.broadcasted_iota(jnp.int32, q.shape, 0) & (CHUNK - 1)) >= META_ROW0
        k = jnp.where(live, k, 0.0)
        g = jnp.where(live, g, 0.0)
        q_inter, k_dec, b, levels = _gla_prepare(q, k, g, CHUNK)
        v_bf = v.astype(jnp.bfloat16)
        outs, c = [], 0
        if with_meta:
            outs.append(jnp.zeros((META_ROW0, HEAD_V), jnp.float32))
            o, s_meta = _gla_chunk(0, CHUNK, q_inter, k_dec, b, levels, v_bf, masks,
                                   jnp.zeros((HEAD_K, HEAD_V), jnp.float32))
            sm_ref[...] = s_meta
            outs.append(o[META_ROW0:])
            c = 1
        for s in range(TAIL_HALF):
            o, s_new = _gla_chunk(c + s, CHUNK, q_inter, k_dec, b, levels, v_bf, masks, sin_ref[s])
            ss_ref[s] = s_new
            outs.append(o[META_ROW0:])
        if not with_meta:
            outs.append(jnp.zeros((ROW_TILE - SAMPLE_ROW0 - SAMPLE_ROWS, HEAD_V), jnp.float32))
        y_ref[row_lo:row_hi, :] = _gla_finish(
            jnp.concatenate(outs, axis=0), u_scr[row_lo:row_hi, r0:r0 + HEAD_V], gn_ref[...])

    @pl.when(half == 0)
    def _first():
        run(0, True, 0, TAIL_SPLIT_ROW)

    @pl.when(half == 1)
    def _second():
        run(TAIL_HALF, False, TAIL_SPLIT_ROW, ROW_TILE)


def _gla_tail(xb, w_in_t, g, gla_norm, state_gla):
    states = pl.BlockSpec((TAIL_HALF, None, HEAD_K, HEAD_V), lambda h, half: (half, h, 0, 0))
    next_head_early = lambda h, half: jnp.minimum(h + half, GLA_HEADS - 1)
    return pl.pallas_call(
        _gla_tail_kernel,
        grid=(GLA_HEADS, 2),
        in_specs=[
            pl.BlockSpec((ROW_TILE, D_MODEL), lambda h, half: (N_PROMPT_TILES, 0)),
            *_head_weight_specs(next_head_early),
            pl.BlockSpec((ROW_TILE, HEAD_K), lambda h, half: (N_PROMPT_TILES, h)),
            pl.BlockSpec((1, HEAD_V), lambda h, half: (0, h)),
            states,
        ],
        out_specs=[
            pl.BlockSpec((ROW_TILE, HEAD_V), lambda h, half: (0, h)),
            pl.BlockSpec((None, HEAD_K, HEAD_V), lambda h, half: (h, 0, 0)),
            states,
        ],
        out_shape=[
            jax.ShapeDtypeStruct((ROW_TILE, D_MODEL), jnp.float32),
            jax.ShapeDtypeStruct((GLA_HEADS, HEAD_K, HEAD_V), jnp.float32),
            jax.ShapeDtypeStruct((DEC_BATCH, GLA_HEADS, HEAD_K, HEAD_V), jnp.float32),
        ],
        scratch_shapes=[pltpu.VMEM((HEAD_ROWS, D_MODEL), jnp.bfloat16),
                        pltpu.VMEM((ROW_TILE, HEAD_ROWS), jnp.float32)],
        compiler_params=pltpu.CompilerParams(
            dimension_semantics=("arbitrary", "arbitrary"), vmem_limit_bytes=VMEM_LIMIT),
        name="gla_tail",
    )(xb, *([w_in_t] * 4), g, gla_norm, state_gla)


GLA_STEP_ROWS = ROW_TILE
N_GLA_ROW_STEPS = SEQ // GLA_STEP_ROWS
N_GLA_STEPS = GLA_HEADS * N_GLA_ROW_STEPS
MIX_W_ROWS = 5 * D_MODEL
MIX_W_SLAB = MIX_W_ROWS // N_GLA_STEPS
OUT_W_SLAB = D_MODEL // N_GLA_STEPS


def _gla_prompt_kernel(x_ref, wq_ref, wk_ref, wv_ref, wr_ref, g_ref, gn_ref, s0_ref, wmix_ref, wout_ref,
                       y_ref, sp_ref, wmix_bf_ref, wout_bf_ref, wbf, s_scr):
    i = pl.program_id(1)

    @pl.when(i == 0)
    def _start_head():
        _cast_head_weights((wq_ref, wk_ref, wv_ref, wr_ref), wbf)
        s_scr[...] = s0_ref[...]

    masks = _score_masks(PROMPT_CHUNK)
    for r0 in range(0, GLA_STEP_ROWS, ROW_TILE):
        rows = slice(r0, r0 + ROW_TILE)
        q, k, v, r = _project_head(x_ref[rows, :], wbf)
        if r0 == 0:
            wmix_bf_ref[...] = wmix_ref[0].astype(jnp.bfloat16)
            wout_bf_ref[...] = wout_ref[...].astype(jnp.bfloat16)
        q_inter, k_dec, b, levels = _gla_prepare(q, k, g_ref[rows, :], PROMPT_CHUNK)
        v_bf = v.astype(jnp.bfloat16)
        outs = []
        for c in range(ROW_TILE // PROMPT_CHUNK):
            o, s_new = _gla_chunk(c, PROMPT_CHUNK, q_inter, k_dec, b, levels, v_bf, masks, s_scr[...])
            s_scr[...] = s_new
            outs.append(o)
        y_ref[rows, :] = _gla_finish(jnp.concatenate(outs, axis=0), r, gn_ref[...])

    @pl.when(i == N_GLA_ROW_STEPS - 1)
    def _final_state():
        sp_ref[...] = s_scr[...]


def _gla_prompt(xb, w_in_t, g, gla_norm, s_meta, w_out):
    step = lambda h, i: h * N_GLA_ROW_STEPS + i
    return pl.pallas_call(
        _gla_prompt_kernel,
        grid=(GLA_HEADS, N_GLA_ROW_STEPS),
        in_specs=[
            pl.BlockSpec((GLA_STEP_ROWS, D_MODEL), lambda h, i: (i, 0)),
            *_head_weight_specs(),
            pl.BlockSpec((GLA_STEP_ROWS, HEAD_K), lambda h, i: (i, h)),
            pl.BlockSpec((1, HEAD_V), lambda h, i: (0, h)),
            pl.BlockSpec((None, HEAD_K, HEAD_V), lambda h, i: (h, 0, 0)),
            pl.BlockSpec((pl.Element(1), pl.Element(MIX_W_SLAB), pl.Element(D_MODEL)),
                         lambda h, i: (0, pl.multiple_of(GATE_COL0 + GATE_RANK + step(h, i) * MIX_W_SLAB,
                                                         GATE_RANK), 0)),
            pl.BlockSpec((None, OUT_W_SLAB, D_MODEL), lambda h, i: (0, step(h, i), 0)),
        ],
        out_specs=[
            pl.BlockSpec((GLA_STEP_ROWS, HEAD_V), lambda h, i: (i, h)),
            pl.BlockSpec((None, HEAD_K, HEAD_V), lambda h, i: (h, 0, 0)),
            pl.BlockSpec((MIX_W_SLAB, D_MODEL), lambda h, i: (step(h, i), 0)),
            pl.BlockSpec((OUT_W_SLAB, D_MODEL), lambda h, i: (step(h, i), 0)),
        ],
        out_shape=[
            jax.ShapeDtypeStruct((SEQ, D_MODEL), jnp.float32),
            jax.ShapeDtypeStruct((GLA_HEADS, HEAD_K, HEAD_V), jnp.float32),
            jax.ShapeDtypeStruct((MIX_W_ROWS, D_MODEL), jnp.bfloat16),
            jax.ShapeDtypeStruct((D_MODEL, D_MODEL), jnp.bfloat16),
        ],
        scratch_shapes=[pltpu.VMEM((HEAD_ROWS, D_MODEL), jnp.bfloat16),
                        pltpu.VMEM((HEAD_K, HEAD_V), jnp.float32)],
        compiler_params=pltpu.CompilerParams(
            dimension_semantics=("arbitrary", "arbitrary"), vmem_limit_bytes=VMEM_LIMIT),
        name="gla_prompt",
    )(xb, *([w_in_t] * 4), g, gla_norm, s_meta, w_in_t, w_out)


MIX_TN = 512
N_MIX_GROUPS = 5


MIX_STEP_ROWS = 2 * ROW_TILE
N_MIX_ROW_STEPS = SEQ // MIX_STEP_ROWS


def _mix_products(x, w_refs):
    return [lax.dot_general(x, w_ref[...], _NT, preferred_element_type=jnp.float32) for w_ref in w_refs]


def _mix_finish(c, p1, p2, cb, ga, gb, ya, cw_ref):
    conv = cw_ref[0:1, :] * p2 + cw_ref[1:2, :] * p1 + cw_ref[2:3, :] * c
    return (_sigmoid(ga) * ya + _sigmoid(gb) * (cb * conv)).astype(jnp.bfloat16)


def _mix_tail_kernel(x_ref, ya_ref, wcb_ref, wcc_ref, wcx_ref, wga_ref, wgb_ref, cw_ref, cache_ref,
                     m_ref, ctail_ref):
    cb, cc, cx, ga, gb = _mix_products(x_ref[...], (wcb_ref, wcc_ref, wcx_ref, wga_ref, wgb_ref))
    c = cc * cx
    row = lax.broadcasted_iota(jnp.int32, (ROW_TILE, MIX_TN), 0)
    prev1 = pltpu.roll(c, 1, 0)
    prev2 = pltpu.roll(c, 2, 0)

    def stream_rows(j):
        parts = [jnp.zeros((SAMPLE_ROW0, MIX_TN), jnp.float32)]
        for s in range(DEC_BATCH):
            parts.append(jnp.broadcast_to(cache_ref[2 * s + j:2 * s + j + 1, :], (DEC_SEQ, MIX_TN)))
        parts.append(jnp.zeros((ROW_TILE - SAMPLE_ROW0 - SAMPLE_ROWS, MIX_TN), jnp.float32))
        return jnp.concatenate(parts, axis=0)

    old = stream_rows(0)
    new = stream_rows(1)
    in_sample = (row >= SAMPLE_ROW0) & (row < SAMPLE_ROW0 + SAMPLE_ROWS)
    p16 = (row - SAMPLE_ROW0) & (DEC_SEQ - 1)
    first = in_sample & (p16 == 0)
    second = in_sample & (p16 == 1)
    p1 = jnp.where(row == META_ROW0, 0.0, jnp.where(first, new, prev1))
    p2 = jnp.where((row == META_ROW0) | (row == META_ROW0 + 1), 0.0,
                   jnp.where(first, old, jnp.where(second, new, prev2)))
    m_ref[...] = _mix_finish(c, p1, p2, cb, ga, gb, ya_ref[...], cw_ref)
    ctail_ref[...] = c


def _mix_prompt_kernel(x_ref, ya_ref, wcb_ref, wcc_ref, wcx_ref, wga_ref, wgb_ref, cw_ref, c0_ref,
                       m_ref, clast_ref, carry):
    i = pl.program_id(1)

    @pl.when(i == 0)
    def _start():
        carry[...] = c0_ref[...]

    hist = carry[...]
    row = lax.broadcasted_iota(jnp.int32, (ROW_TILE, MIX_TN), 0)
    for r0 in range(0, MIX_STEP_ROWS, ROW_TILE):
        rows = slice(r0, r0 + ROW_TILE)
        cb, cc, cx, ga, gb = _mix_products(x_ref[rows, :], (wcb_ref, wcc_ref, wcx_ref, wga_ref, wgb_ref))
        c = cc * cx
        last = jnp.broadcast_to(hist[7:8], c.shape)
        p1 = jnp.where(row == 0, last, pltpu.roll(c, 1, 0))
        p2 = jnp.where(row == 0, jnp.broadcast_to(hist[6:7], c.shape),
                       jnp.where(row == 1, last, pltpu.roll(c, 2, 0)))
        m_ref[rows, :] = _mix_finish(c, p1, p2, cb, ga, gb, ya_ref[rows, :], cw_ref)
        hist = c[ROW_TILE - 8:]
    carry[...] = hist

    @pl.when(i == N_MIX_ROW_STEPS - 1)
    def _last():
        clast_ref[...] = hist


def _mix_weight_specs(index_args):
    nc = D_MODEL // MIX_TN
    return [pl.BlockSpec((MIX_TN, D_MODEL), functools.partial(index_args, grp * nc))
            for grp in range(N_MIX_GROUPS)]


def _mix_tail(xb, ya_tail, w_mix_bf, conv_w, cache_rows):
    top = lambda cj: (0, cj)
    return pl.pallas_call(
        _mix_tail_kernel,
        grid=(D_MODEL // MIX_TN,),
        in_specs=[
            pl.BlockSpec((ROW_TILE, D_MODEL), lambda cj: (N_PROMPT_TILES, 0)),
            pl.BlockSpec((ROW_TILE, MIX_TN), top),
            *_mix_weight_specs(lambda base, cj: (base + cj, 0)),
            pl.BlockSpec((3, MIX_TN), top),
            pl.BlockSpec((2 * DEC_BATCH, MIX_TN), top),
        ],
        out_specs=[pl.BlockSpec((ROW_TILE, MIX_TN), top), pl.BlockSpec((ROW_TILE, MIX_TN), top)],
        out_shape=[
            jax.ShapeDtypeStruct((ROW_TILE, D_MODEL), jnp.bfloat16),
            jax.ShapeDtypeStruct((ROW_TILE, D_MODEL), jnp.float32),
        ],
        compiler_params=pltpu.CompilerParams(
            dimension_semantics=("arbitrary",), vmem_limit_bytes=VMEM_LIMIT),
        name="proj_conv_mix_tail",
    )(xb, ya_tail, *([w_mix_bf] * N_MIX_GROUPS), conv_w, cache_rows)


def _mix_prompt(xb, ya_prompt, w_mix_bf, conv_w, c_tail):
    top = lambda cj, i: (0, cj)
    return pl.pallas_call(
        _mix_prompt_kernel,
        grid=(D_MODEL // MIX_TN, N_MIX_ROW_STEPS),
        in_specs=[
            pl.BlockSpec((MIX_STEP_ROWS, D_MODEL), lambda cj, i: (i, 0)),
            pl.BlockSpec((MIX_STEP_ROWS, MIX_TN), lambda cj, i: (i, cj)),
            *_mix_weight_specs(lambda base, cj, i: (base + cj, 0)),
            pl.BlockSpec((3, MIX_TN), top),
            pl.BlockSpec((8, MIX_TN), lambda cj, i: (CHUNK // 8 - 1, cj)),
        ],
        out_specs=[pl.BlockSpec((MIX_STEP_ROWS, MIX_TN), lambda cj, i: (i, cj)), pl.BlockSpec((8, MIX_TN), top)],
        out_shape=[
            jax.ShapeDtypeStruct((SEQ, D_MODEL), jnp.bfloat16),
            jax.ShapeDtypeStruct((8, D_MODEL), jnp.float32),
        ],
        scratch_shapes=[pltpu.VMEM((8, MIX_TN), jnp.float32)],
        compiler_params=pltpu.CompilerParams(
            dimension_semantics=("arbitrary", "arbitrary"), vmem_limit_bytes=VMEM_LIMIT),
        name="proj_conv_mix",
    )(xb, ya_prompt, *([w_mix_bf] * N_MIX_GROUPS), conv_w, c_tail)


SUB_ROWS = 128


def _out_proj_kernel(mp_ref, mt_ref, xn_ref, w_ref, lg_ref, lb_ref, h_ref, hb_ref):
    is_prompt = pl.program_id(0) < N_PROMPT_TILES
    for r in range(0, ROW_TILE, SUB_ROWS):
        rows = slice(r, r + SUB_ROWS)
        m = jnp.where(is_prompt, mp_ref[rows, :], mt_ref[rows, :])
        mix = jnp.dot(m, w_ref[...], preferred_element_type=jnp.float32)
        h = _layer_norm(ALPHA * xn_ref[rows, :] + mix, lg_ref[...], lb_ref[...])
        h_ref[rows, :] = h
        hb_ref[rows, :] = h.astype(jnp.bfloat16)


def _out_proj(m_prompt, m_tail, xn, w_out, ln_g, ln_b):
    row = lambda i: (i, 0)
    full = lambda i: (0, 0)
    return pl.pallas_call(
        _out_proj_kernel,
        grid=(N_TILES,),
        in_specs=[
            pl.BlockSpec((ROW_TILE, D_MODEL), lambda i: (jnp.minimum(i, N_PROMPT_TILES - 1), 0)),
            pl.BlockSpec((ROW_TILE, D_MODEL), full),
            pl.BlockSpec((ROW_TILE, D_MODEL), row),
            pl.BlockSpec((D_MODEL, D_MODEL), full),
            pl.BlockSpec((1, D_MODEL), full),
            pl.BlockSpec((1, D_MODEL), full),
        ],
        out_specs=[pl.BlockSpec((ROW_TILE, D_MODEL), row), pl.BlockSpec((ROW_TILE, D_MODEL), row)],
        out_shape=[
            jax.ShapeDtypeStruct((ROWS, D_MODEL), jnp.float32),
            jax.ShapeDtypeStruct((ROWS, D_MODEL), jnp.bfloat16),
        ],
        compiler_params=pltpu.CompilerParams(
            dimension_semantics=("arbitrary",), vmem_limit_bytes=VMEM_LIMIT),
        name="out_proj_ln",
    )(m_prompt, m_tail, xn, w_out, ln_g, ln_b)


FF_TILE = 512
N_FF_TILES = D_FF // FF_TILE
FFN_UP_ROWS = 2 * BIG_ROW_TILE


N_FFN_UP_STEPS = N_FF_TILES * (ROWS // FFN_UP_ROWS)
DOWN_W_SLAB = D_FF // N_FFN_UP_STEPS


def _ffn_up_kernel(hb_ref, wg_ref, wu_ref, wd_ref, act_ref, wd_bf_ref, wg_bf, wu_bf):
    @pl.when(pl.program_id(1) == 0)
    def _cast():
        for r in range(0, D_MODEL, W_SLAB):
            wg_bf[r:r + W_SLAB, :] = wg_ref[r:r + W_SLAB, :].astype(jnp.bfloat16)
            wu_bf[r:r + W_SLAB, :] = wu_ref[r:r + W_SLAB, :].astype(jnp.bfloat16)

    for r in range(0, FFN_UP_ROWS, BIG_ROW_TILE):
        rows = slice(r, r + BIG_ROW_TILE)
        hb = hb_ref[rows, :]
        gate = jnp.dot(hb, wg_bf[...], preferred_element_type=jnp.float32)
        up = jnp.dot(hb, wu_bf[...], preferred_element_type=jnp.float32)
        if r == 0:
            wd_bf_ref[...] = wd_ref[...].astype(jnp.bfloat16)
        act_ref[rows, :] = (gate * _sigmoid(gate) * up).astype(jnp.bfloat16)


def _ffn_up(hb, w_ffn_in, w_ffn_out):
    step = lambda j, i: j * (ROWS // FFN_UP_ROWS) + i
    return pl.pallas_call(
        _ffn_up_kernel,
        grid=(N_FF_TILES, ROWS // FFN_UP_ROWS),
        in_specs=[
            pl.BlockSpec((FFN_UP_ROWS, D_MODEL), lambda j, i: (i, 0)),
            pl.BlockSpec((None, D_MODEL, FF_TILE), lambda j, i: (0, 0, j)),
            pl.BlockSpec((None, D_MODEL, FF_TILE), lambda j, i: (0, 0, N_FF_TILES + j)),
            pl.BlockSpec((None, DOWN_W_SLAB, D_MODEL), lambda j, i: (0, step(j, i), 0)),
        ],
        out_specs=[
            pl.BlockSpec((FFN_UP_ROWS, FF_TILE), lambda j, i: (i, j)),
            pl.BlockSpec((DOWN_W_SLAB, D_MODEL), lambda j, i: (step(j, i), 0)),
        ],
        out_shape=[
            jax.ShapeDtypeStruct((ROWS, D_FF), jnp.bfloat16),
            jax.ShapeDtypeStruct((D_FF, D_MODEL), jnp.bfloat16),
        ],
        scratch_shapes=[pltpu.VMEM((D_MODEL, FF_TILE), jnp.bfloat16),
                        pltpu.VMEM((D_MODEL, FF_TILE), jnp.bfloat16)],
        compiler_params=pltpu.CompilerParams(
            dimension_semantics=("arbitrary", "arbitrary"), vmem_limit_bytes=VMEM_LIMIT),
        name="ffn_up",
    )(hb, w_ffn_in, w_ffn_in, w_ffn_out)


DOWN_TILE = 256
N_DOWN_PROMPT = SEQ // DOWN_TILE
N_DOWN_TILES = ROWS // DOWN_TILE


def _ffn_down_kernel(act_ref, h_ref, wd_ref, lg_ref, lb_ref, yp_ref, yt_ref):
    i = pl.program_id(0)

    def rows_out(o_ref):
        for r in range(0, DOWN_TILE, SUB_ROWS):
            rows = slice(r, r + SUB_ROWS)
            f = jnp.dot(act_ref[rows, :], wd_ref[...], preferred_element_type=jnp.float32)
            o_ref[rows, :] = _layer_norm(ALPHA * h_ref[rows, :] + f, lg_ref[...], lb_ref[...])

    @pl.when(i < N_DOWN_PROMPT)
    def _prompt():
        rows_out(yp_ref)

    @pl.when(i >= N_DOWN_PROMPT)
    def _tail():
        rows_out(yt_ref)


def _ffn_down(act, h, w_down, ln_g, ln_b):
    row = lambda i: (i, 0)
    full = lambda i: (0, 0)
    return pl.pallas_call(
        _ffn_down_kernel,
        grid=(N_DOWN_TILES,),
        in_specs=[
            pl.BlockSpec((DOWN_TILE, D_FF), row),
            pl.BlockSpec((DOWN_TILE, D_MODEL), row),
            pl.BlockSpec((D_FF, D_MODEL), full, pipeline_mode=pl.Buffered(1)),
            pl.BlockSpec((1, D_MODEL), full),
            pl.BlockSpec((1, D_MODEL), full),
        ],
        out_specs=[
            pl.BlockSpec((None, DOWN_TILE, D_MODEL), lambda i: (0, jnp.minimum(i, N_DOWN_PROMPT - 1), 0)),
            pl.BlockSpec((DOWN_TILE, D_MODEL), lambda i: (jnp.maximum(i - N_DOWN_PROMPT, 0), 0)),
        ],
        out_shape=[
            jax.ShapeDtypeStruct((1, SEQ, D_MODEL), jnp.float32),
            jax.ShapeDtypeStruct((ROW_TILE, D_MODEL), jnp.float32),
        ],
        compiler_params=pltpu.CompilerParams(
            dimension_semantics=("arbitrary",), vmem_limit_bytes=VMEM_LIMIT),
        name="ffn_down_ln",
    )(act, h, w_down, ln_g, ln_b)


def kernel(x_prompt, x_sample, state_gla, cache_conv, meta_tokens, ln_in_g, ln_in_b, w_in, w_gate_up, b_gate, gla_norm, conv_w, w_out, ln1_g, ln1_b, w_ffn_in, w_ffn_out, ln2_g, ln2_b):
    f32, bf16 = jnp.float32, jnp.bfloat16
    tail = jnp.concatenate([
        jnp.zeros((META_ROW0, D_MODEL), f32),
        meta_tokens.astype(f32),
        x_sample.reshape(SAMPLE_ROWS, D_MODEL),
        jnp.zeros((ROW_TILE - SAMPLE_ROW0 - SAMPLE_ROWS, D_MODEL), f32),
    ], axis=0)
    w_gu = jnp.pad(w_gate_up[0], ((0, LANE - GATE_RANK), (0, 0))).astype(bf16)
    vec = lambda p: p.reshape(1, -1).astype(f32)

    w_in_t = jnp.swapaxes(w_in, 1, 2)
    xn, xb, g = _ln_gate(x_prompt, tail, vec(ln_in_g), vec(ln_in_b), w_in_t, w_gu, vec(b_gate[0]))
    ya_tail, s_meta, s_sample = _gla_tail(xb, w_in_t, g, vec(gla_norm[0]), state_gla[0])
    ya_prompt, s_prompt, w_mix_bf, w_out_bf = _gla_prompt(xb, w_in_t, g, vec(gla_norm[0]), s_meta, w_out)
    m_tail, c_tail = _mix_tail(xb, ya_tail, w_mix_bf, conv_w[0], cache_conv[0].reshape(2 * DEC_BATCH, D_MODEL))
    m_prompt, c_last = _mix_prompt(xb, ya_prompt, w_mix_bf, conv_w[0], c_tail)
    h, hb = _out_proj(m_prompt, m_tail, xn, w_out_bf, vec(ln1_g[0]), vec(ln1_b[0]))
    act, w_down_bf = _ffn_up(hb, w_ffn_in, w_ffn_out)
    y_prompt, y_tail = _ffn_down(act, h, w_down_bf, vec(ln2_g[0]), vec(ln2_b[0]))

    y_sample = y_tail[SAMPLE_ROW0:SAMPLE_ROW0 + SAMPLE_ROWS].reshape(DEC_BATCH, DEC_SEQ, D_MODEL)
    cache_prompt = c_last[6:8][None, None]
    cache_sample = c_tail[SAMPLE_ROW0:SAMPLE_ROW0 + SAMPLE_ROWS].reshape(
        DEC_BATCH, DEC_SEQ, D_MODEL)[:, DEC_SEQ - 2:][None]
    return (y_prompt, y_sample, s_prompt[None, None], cache_prompt, s_sample[None], cache_sample)
```
